```python
import jax, jax.numpy as jnp
from jax import lax
import numpy as np

D_MODEL = 1024
BATCH = 8
SEQ = 2048
DEPTH = 2
DEC_BATCH = 8
DEC_SEQ = 64
PAST_LEN = 2048

CHUNK = 64
H_A = 8
DH_A = 64
BACK_A = 8
REL_CLIP = 128
H_B = 4
DK_B = 128
DV_B = 256
CONV_B = 4
HQ_C = 8
HKV_C = 2
DH_C = 64
WINDOW_C = 128
BACK_C = WINDOW_C // CHUNK
ROT_DIM = DH_C // 4
ROPE_THETA = 500000.0
D_FF = 2816
N_EXPERTS = 8
TOP_K = 2
D_FF_E = 3584
EPS = 1e-6
NEG = -1e30

WA = H_A * DH_A
WB_QK = H_B * DK_B
WB_V = H_B * DV_B
WC_Q = HQ_C * DH_C
WC_KV = HKV_C * DH_C
SPLIT_SIZES = (WA, WA, WA, 2 * WB_QK, WB_V, 2 * H_B, WB_V, WC_Q, WC_KV, WC_KV, 3 * D_MODEL)
D_IN = WA * 3 + 2 * WB_QK + 2 * WB_V + 2 * H_B + WC_Q + 2 * WC_KV + 3 * D_MODEL
STATE_NAMES = ('a_k', 'a_v', 'b_C', 'b_n', 'b_m', 'b_conv', 'c_k', 'c_v')
F32 = jnp.float32

kernel_name = 'hybrid_streaming_encoder_step'


def rmsnorm(x, g):
    xf = x.astype(F32)
    y = xf * lax.rsqrt(jnp.mean(xf * xf, axis=-1, keepdims=True) + EPS)
    return (y * g.astype(F32)).astype(x.dtype)


def split_cols(u):
    out, start = [], 0
    for size in SPLIT_SIZES:
        out.append(u[..., start:start + size])
        start += size
    return out


def rope_partial(x, pos):
    half = ROT_DIM // 2
    inv_freq = 1.0 / (ROPE_THETA ** (jnp.arange(half, dtype=F32) * (2.0 / ROT_DIM)))
    ang = pos.astype(F32)[:, None] * inv_freq[None, :]
    cos = jnp.cos(ang)[None, :, None, :]
    sin = jnp.sin(ang)[None, :, None, :]
    xr = x[..., :ROT_DIM].astype(F32)
    x1, x2 = xr[..., :half], xr[..., half:]
    rot = jnp.concatenate([x1 * cos - x2 * sin, x2 * cos + x1 * sin], axis=-1).astype(x.dtype)
    return jnp.concatenate([rot, x[..., ROT_DIM:]], axis=-1)


def chunk_band(t, n_back):
    B, S = t.shape[0], t.shape[1]
    nc = S // CHUNK
    tc = t.reshape((B, nc, CHUNK) + t.shape[2:])
    tp = jnp.concatenate([jnp.zeros((B, n_back, CHUNK) + t.shape[2:], t.dtype), tc], axis=1)
    band = jnp.stack([tp[:, j:j + nc] for j in range(n_back + 1)], axis=2)
    return band.reshape((B, nc, (n_back + 1) * CHUNK) + t.shape[2:])


def band_attention(q, k, v, k_cache, v_cache, n_back, rel_table, sink):
    B, S, Hq, dh = q.shape
    Hk = k.shape[2]
    G = Hq // Hk
    if k_cache is None:
        nc = S // CHUNK
        kb, vb = chunk_band(k, n_back), chunk_band(v, n_back)
        qb = q.reshape(B, nc, CHUNK, Hk, G, dh)
        cidx = jnp.arange(nc)[:, None]
        q_pos = cidx * CHUNK + jnp.arange(CHUNK)[None, :]
        k_pos = (cidx - n_back) * CHUNK + jnp.arange((n_back + 1) * CHUNK)[None, :]
    else:
        W = k_cache.shape[1]
        kb = jnp.concatenate([k_cache.astype(k.dtype), k], axis=1)[:, None]
        vb = jnp.concatenate([v_cache.astype(v.dtype), v], axis=1)[:, None]
        qb = q.reshape(B, 1, S, Hk, G, dh)
        q_pos = (PAST_LEN + jnp.arange(S))[None, :]
        k_pos = (PAST_LEN - W + jnp.arange(W + S))[None, :]
    valid = k_pos >= 0
    s = jnp.einsum('bnqhgd,bnkhd->bnhgqk', qb, kb).astype(F32) * (dh ** -0.5)
    if rel_table is not None:
        rel = jnp.clip(q_pos[:, :, None] - k_pos[:, None, :], -REL_CLIP, REL_CLIP) + REL_CLIP
        bias = jnp.moveaxis(rel_table.astype(F32)[rel], -1, 1)
        s = s + bias.reshape(bias.shape[0], Hk, G, bias.shape[2], bias.shape[3])[None]
    s = jnp.where(valid[None, :, None, None, None, :], s, NEG)
    if sink is not None:
        sk = jnp.broadcast_to(sink.astype(F32).reshape(1, 1, Hk, G, 1, 1), s.shape[:-1] + (1,))
        p = jax.nn.softmax(jnp.concatenate([s, sk], axis=-1), axis=-1)[..., :-1]
    else:
        p = jax.nn.softmax(s, axis=-1)
    o = jnp.einsum('bnhgqk,bnkhd->bnqhgd', p.astype(vb.dtype), vb)
    return o.reshape(B, S, Hq, dh)


def mlstm_chunk(carry, xs):
    C, n, m = carry
    q, k, v, ig, lf = xs
    L = q.shape[1]
    b = jnp.cumsum(lf, axis=1)
    causal = jnp.tril(jnp.ones((L, L), dtype=bool))
    log_d = jnp.where(causal[None, :, :, None], b[:, :, None, :] - b[:, None, :, :] + ig[:, None, :, :], NEG)
    log_inter = b + m[:, None, :]
    m_t = jnp.maximum(log_inter, jnp.max(log_d, axis=2))
    d = jnp.exp(log_d - m_t[:, :, None, :])
    w_inter = jnp.exp(log_inter - m_t)
    a = jnp.einsum('bthd,bshd->btsh', q, k) * d
    num = jnp.einsum('btsh,bshe->bthe', a, v) + w_inter[..., None] * jnp.einsum('bthd,bhde->bthe', q, C)
    den = jnp.sum(a, axis=2) + w_inter * jnp.einsum('bthd,bhd->bth', q, n)
    h = num / jnp.maximum(jnp.abs(den), jnp.exp(-m_t))[..., None]
    w_last = d[:, -1]
    w_prev = w_inter[:, -1]
    C_new = w_prev[..., None, None] * C + jnp.einsum('bsh,bshd,bshe->bhde', w_last, k, v)
    n_new = w_prev[..., None] * n + jnp.einsum('bsh,bshd->bhd', w_last, k)
    return (C_new, n_new, m_t[:, -1]), h


def swiglu(t, w1, w3, w2):
    return (jax.nn.silu(t @ w1) * (t @ w3)) @ w2


def moe(h, w_r, b_r, w1, w3, w2):
    B, S, D = h.shape
    t = h.reshape(B * S, D)
    logits = (t @ w_r).astype(F32) + b_r.astype(F32)
    top_v, top_i = lax.top_k(logits, TOP_K)
    wts = jax.nn.softmax(top_v, axis=-1)
    gate = jnp.sum(jax.nn.one_hot(top_i, N_EXPERTS, dtype=F32) * wts[..., None], axis=1).astype(t.dtype)
    out = jnp.zeros_like(t)
    for e in range(N_EXPERTS):
        out = out + gate[:, e:e + 1] * swiglu(t, w1[e], w3[e], w2[e])
    return out.reshape(B, S, D)


def trunk(x, c, cache, P):
    B, S, _ = x.shape
    prompt = cache is None
    pos = jnp.arange(S) + (0 if prompt else PAST_LEN)
    new = {name: [] for name in STATE_NAMES}
    for l in range(DEPTH):
        mod = jax.nn.silu(c) @ P['w_ada'][l] + P['b_ada'][l]
        sh1, sc1, g1, sh2, sc2, g2 = jnp.split(mod[:, None, :], 6, axis=-1)
        h = rmsnorm(x, P['norm1_g'][l]) * (1 + sc1) + sh1
        qa, ka, va, qk_b, v_b, if_b, o_b, qc, kc, vc, gates = split_cols(h @ P['w_in'][l])
        qa = qa.reshape(B, S, H_A, DH_A)
        ka = ka.reshape(B, S, H_A, DH_A)
        va = va.reshape(B, S, H_A, DH_A)
        ya = band_attention(qa, ka, va, None if prompt else cache['a_k'][l],
                            None if prompt else cache['a_v'][l], BACK_A, P['rel_a'][l], None)
        keep_a = min(BACK_A * CHUNK, S)
        new['a_k'].append(ka[:, -keep_a:] if prompt else ka)
        new['a_v'].append(va[:, -keep_a:] if prompt else va)
        hist = jnp.zeros((B, CONV_B - 1, 2 * WB_QK), x.dtype) if prompt else cache['b_conv'][l].astype(x.dtype)
        xp = jnp.concatenate([hist, qk_b], axis=1)
        cw = P['conv_w_b'][l]
        conv = P['conv_b_b'][l] + xp[:, 0:S] * cw[0]
        for j in range(1, CONV_B):
            conv = conv + xp[:, j:j + S] * cw[j]
        new['b_conv'].append(xp[:, -(CONV_B - 1):])
        qk_act = jax.nn.silu(conv).astype(F32)
        qm = qk_act[..., :WB_QK].reshape(B, S, H_B, DK_B)
        km = qk_act[..., WB_QK:].reshape(B, S, H_B, DK_B) * (DK_B ** -0.5)
        vm = v_b.astype(F32).reshape(B, S, H_B, DV_B)
        gif = if_b.astype(F32) + P['b_if_b'][l].astype(F32)
        ig = gif[..., :H_B]
        lf = jax.nn.log_sigmoid(gif[..., H_B:])
        if prompt:
            carry0 = (jnp.zeros((B, H_B, DK_B, DV_B), F32), jnp.zeros((B, H_B, DK_B), F32), jnp.zeros((B, H_B), F32))
            L = CHUNK
        else:
            carry0 = (cache['b_C'][l].astype(F32), cache['b_n'][l].astype(F32), cache['b_m'][l].astype(F32))
            L = S
        nc = S // L
        xs = tuple(jnp.moveaxis(t.reshape((B, nc, L) + t.shape[2:]), 1, 0) for t in (qm, km, vm, ig, lf))
        (C_f, n_f, m_f), hs = lax.scan(mlstm_chunk, carry0, xs)
        new['b_C'].append(C_f)
        new['b_n'].append(n_f)
        new['b_m'].append(m_f)
        hm = jnp.moveaxis(hs, 0, 1).reshape(B, S, H_B, DV_B)
        hm = hm * lax.rsqrt(jnp.mean(hm * hm, axis=-1, keepdims=True) + EPS)
        hm = hm * P['norm_b_g'][l].astype(F32).reshape(H_B, DV_B)
        yb = (hm.reshape(B, S, WB_V) * jax.nn.sigmoid(o_b.astype(F32))).astype(x.dtype)
        qc = rope_partial(qc.reshape(B, S, HQ_C, DH_C), pos)
        kc = rope_partial(kc.reshape(B, S, HKV_C, DH_C), pos)
        vc = vc.reshape(B, S, HKV_C, DH_C)
        yc = band_attention(qc, kc, vc, None if prompt else cache['c_k'][l],
                            None if prompt else cache['c_v'][l], BACK_C, None, P['sink_c'][l])
        keep_c = min(WINDOW_C, S)
        new['c_k'].append(kc[:, -keep_c:] if prompt else kc)
        new['c_v'].append(vc[:, -keep_c:] if prompt else vc)
        g = jax.nn.sigmoid(gates).reshape(B, S, 3, D_MODEL)
        merged = (g[:, :, 0] * (ya.reshape(B, S, WA) @ P['w_br_a'][l])
                  + g[:, :, 1] * (yb @ P['w_br_b'][l])
                  + g[:, :, 2] * (yc.reshape(B, S, WC_Q) @ P['w_br_c'][l]))
        x = x + g1 * (merged @ P['w_o'][l])
        h2 = rmsnorm(x, P['norm2_g'][l]) * (1 + sc2) + sh2
        if l % 2 == 0:
            i = l // 2
            f = swiglu(h2, P['w_ff1'][i], P['w_ff3'][i], P['w_ff2'][i])
        else:
            i = l // 2
            f = moe(h2, P['w_router'][i], P['b_router'][i], P['w_e1'][i], P['w_e3'][i], P['w_e2'][i])
        x = x + g2 * f
    y = rmsnorm(x, P['norm_f_g'])
    return y, tuple(jnp.stack(new[name]) for name in STATE_NAMES)


def setup_inputs(seed: int = 0) -> dict:
    key = jax.random.key(seed)
    ks = iter(jax.random.split(key, 48))

    def nrm(shape, scale):
        return jax.random.normal(next(ks), shape, F32) * scale

    n_dense = (DEPTH + 1) // 2
    n_moe = DEPTH // 2
    wa_cache = min(BACK_A * CHUNK, PAST_LEN)
    wc_cache = min(WINDOW_C, PAST_LEN)
    D = D_MODEL
    forget_bias = jnp.linspace(3.0, 6.0, H_B, dtype=F32)[None, :] + nrm((DEPTH, H_B), 0.1)
    b_if_b = jnp.concatenate([nrm((DEPTH, H_B), 0.1), forget_bias], axis=-1)
    return {
        'x_prompt': nrm((BATCH, SEQ, D), 1.0),
        'x_sample': nrm((DEC_BATCH, DEC_SEQ, D), 1.0),
        'c_prompt': nrm((BATCH, D), 1.0),
        'c_sample': nrm((DEC_BATCH, D), 1.0),
        'cache_a_k': nrm((DEPTH, DEC_BATCH, wa_cache, H_A, DH_A), 1.0),
        'cache_a_v': nrm((DEPTH, DEC_BATCH, wa_cache, H_A, DH_A), 1.0),
        'state_b_C': nrm((DEPTH, DEC_BATCH, H_B, DK_B, DV_B), 0.1),
        'state_b_n': nrm((DEPTH, DEC_BATCH, H_B, DK_B), 0.1),
        'state_b_m': nrm((DEPTH, DEC_BATCH, H_B), 1.0),
        'state_b_conv': nrm((DEPTH, DEC_BATCH, CONV_B - 1, 2 * WB_QK), 1.0),
        'cache_c_k': nrm((DEPTH, DEC_BATCH, wc_cache, HKV_C, DH_C), 1.0),
        'cache_c_v': nrm((DEPTH, DEC_BATCH, wc_cache, HKV_C, DH_C), 1.0),
        'norm1_g': 1.0 + nrm((DEPTH, D), 0.1),
        'norm2_g': 1.0 + nrm((DEPTH, D), 0.1),
        'w_ada': nrm((DEPTH, D, 6 * D), D ** -0.5),
        'b_ada': nrm((DEPTH, 6 * D), 0.02),
        'w_in': nrm((DEPTH, D, D_IN), D ** -0.5),
        'b_if_b': b_if_b,
        'conv_w_b': nrm((DEPTH, CONV_B, 2 * WB_QK), CONV_B ** -0.5),
        'conv_b_b': nrm((DEPTH, 2 * WB_QK), 0.02),
        'norm_b_g': 1.0 + nrm((DEPTH, WB_V), 0.1),
        'rel_a': nrm((DEPTH, 2 * REL_CLIP + 1, H_A), 0.5),
        'sink_c': nrm((DEPTH, HQ_C), 1.0),
        'w_br_a': nrm((DEPTH, WA, D), WA ** -0.5),
        'w_br_b': nrm((DEPTH, WB_V, D), WB_V ** -0.5),
        'w_br_c': nrm((DEPTH, WC_Q, D), WC_Q ** -0.5),
        'w_o': nrm((DEPTH, D, D), D ** -0.5),
        'w_ff1': nrm((n_dense, D, D_FF), D ** -0.5),
        'w_ff3': nrm((n_dense, D, D_FF), D ** -0.5),
        'w_ff2': nrm((n_dense, D_FF, D), D_FF ** -0.5),
        'w_router': nrm((n_moe, D, N_EXPERTS), D ** -0.5),
        'b_router': nrm((n_moe, N_EXPERTS), 0.01),
        'w_e1': nrm((n_moe, N_EXPERTS, D, D_FF_E), D ** -0.5),
        'w_e3': nrm((n_moe, N_EXPERTS, D, D_FF_E), D ** -0.5),
        'w_e2': nrm((n_moe, N_EXPERTS, D_FF_E, D), D_FF_E ** -0.5),
        'norm_f_g': 1.0 + nrm((D,), 0.1),
    }


def reference(x_prompt, x_sample, c_prompt, c_sample, cache_a_k, cache_a_v, state_b_C, state_b_n,
              state_b_m, state_b_conv, cache_c_k, cache_c_v, norm1_g, norm2_g, w_ada, b_ada, w_in,
              b_if_b, conv_w_b, conv_b_b, norm_b_g, rel_a, sink_c, w_br_a, w_br_b, w_br_c, w_o,
              w_ff1, w_ff3, w_ff2, w_router, b_router, w_e1, w_e3, w_e2, norm_f_g):
    P = dict(norm1_g=norm1_g, norm2_g=norm2_g, w_ada=w_ada, b_ada=b_ada, w_in=w_in, b_if_b=b_if_b,
             conv_w_b=conv_w_b, conv_b_b=conv_b_b, norm_b_g=norm_b_g, rel_a=rel_a, sink_c=sink_c,
             w_br_a=w_br_a, w_br_b=w_br_b, w_br_c=w_br_c, w_o=w_o, w_ff1=w_ff1, w_ff3=w_ff3,
             w_ff2=w_ff2, w_router=w_router, b_router=b_router, w_e1=w_e1, w_e3=w_e3, w_e2=w_e2,
             norm_f_g=norm_f_g)
    y_prompt, (p_a_k, p_a_v, p_b_C, p_b_n, p_b_m, p_b_conv, p_c_k, p_c_v) = trunk(x_prompt, c_prompt, None, P)
    cache = dict(a_k=cache_a_k, a_v=cache_a_v, b_C=state_b_C, b_n=state_b_n, b_m=state_b_m,
                 b_conv=state_b_conv, c_k=cache_c_k, c_v=cache_c_v)
    y_sample, (s_a_k, s_a_v, s_b_C, s_b_n, s_b_m, s_b_conv, s_c_k, s_c_v) = trunk(x_sample, c_sample, cache, P)
    return (y_prompt, y_sample, p_a_k, p_a_v, p_b_C, p_b_n, p_b_m, p_b_conv, p_c_k, p_c_v,
            s_a_k, s_a_v, s_b_C, s_b_n, s_b_m, s_b_conv, s_c_k, s_c_v)
```

```python
import functools

import numpy as np
import jax
import jax.numpy as jnp
from jax import lax
from jax.experimental import pallas as pl
from jax.experimental.pallas import tpu as pltpu

F32 = jnp.float32
BF16 = jnp.bfloat16

D_MODEL = 1024
DEPTH = 2
PAST_LEN = 2048
CHUNK = 64
H_A, DH_A, BACK_A, REL_CLIP = 8, 64, 8, 128
H_B, DK_B, DV_B, CONV_B = 4, 128, 256, 4
HQ_C, HKV_C, DH_C, BACK_C = 8, 2, 64, 2
ROT_DIM = DH_C // 4
ROPE_THETA = 500000.0
D_FF = 2816
N_EXPERTS = 8
D_FF_E = 3584
EPS = 1e-6
NEG = -1e30

WA = H_A * DH_A
WB_QK = H_B * DK_B
WB_V = H_B * DV_B
WC_Q = HQ_C * DH_C
WC_KV = HKV_C * DH_C
PRE_A = BACK_A * CHUNK
PRE_C = BACK_C * CHUNK

COL_QA, COL_KA, COL_VA, COL_QC = 0, 512, 1024, 1536
COL_QKB = 2048
COL_GATES = 3072
COL_VB = 6144
COL_OB = 7168
COL_KC = 8192
COL_VC = 8320
N_MAIN = 8448

LANE = 128
VMEM_LIMIT = 48 * 1024 * 1024


def _cparams(*sem):
    return pltpu.CompilerParams(dimension_semantics=sem, vmem_limit_bytes=VMEM_LIMIT)


def _row_tile(t):
    for tm in (512, 256, 128, 64):
        if t % tm == 0:
            return tm
    raise ValueError(f"token count {t} is not a multiple of 64")


def _ln_mod(x, g, sc, sh):
    tm, d = x.shape
    y = x * lax.rsqrt(jnp.mean(x * x, axis=-1, keepdims=True) + EPS) * g
    y = y.reshape(tm // CHUNK, CHUNK, d) * (1.0 + sc) + sh
    return y.reshape(tm, d)


def _mod_spec(tm, kind):
    return pl.BlockSpec((tm // CHUNK, 1, D_MODEL), lambda i, *_, k=kind: (i, 0, k))


def _ada_kernel(c_ref, w_ref, b_ref, o_ref):
    c = c_ref[...]
    a = (c * jax.nn.sigmoid(c)).astype(BF16)
    o_ref[...] = jnp.dot(a, w_ref[...].astype(BF16), preferred_element_type=F32) + b_ref[...]


def _ada(c, w, b):
    nb, d = c.shape
    n = w.shape[1]
    tn = 1536
    return pl.pallas_call(
        _ada_kernel,
        grid=(n // tn,),
        in_specs=[pl.BlockSpec((nb, d), lambda j: (0, 0)),
                  pl.BlockSpec((d, tn), lambda j: (0, j)),
                  pl.BlockSpec((1, tn), lambda j: (0, j))],
        out_specs=pl.BlockSpec((nb, tn), lambda j: (0, j)),
        out_shape=jax.ShapeDtypeStruct((nb, n), F32),
        compiler_params=_cparams("arbitrary"),
        name="adaln",
    )(c, w, b.reshape(1, n))


def _inproj_kernel(x_ref, sh_ref, sc_ref, g_ref, w_ref, wif_ref, u_ref, gt_ref, h_scr):
    @pl.when(pl.program_id(1) == 0)
    def _():
        h = _ln_mod(x_ref[...], g_ref[...], sc_ref[...], sh_ref[...]).astype(BF16)
        h_scr[...] = h
        gt_ref[...] = lax.dot_general(wif_ref[...], h, (((1,), (1,)), ((), ())),
                                      preferred_element_type=F32)

    u_ref[...] = jnp.dot(h_scr[...], w_ref[...], preferred_element_type=F32)


def _inproj(x, mod_g, g, w_main, w_if_t):
    t, d = x.shape
    tm = _row_tile(t)
    tn = 768
    return pl.pallas_call(
        _inproj_kernel,
        grid=(t // tm, N_MAIN // tn),
        in_specs=[pl.BlockSpec((tm, d), lambda i, j: (i, 0)),
                  _mod_spec(tm, 0), _mod_spec(tm, 1),
                  pl.BlockSpec((1, d), lambda i, j: (0, 0)),
                  pl.BlockSpec((d, tn), lambda i, j: (0, j)),
                  pl.BlockSpec((2 * H_B, d), lambda i, j: (0, 0))],
        out_specs=[pl.BlockSpec((tm, tn), lambda i, j: (i, j)),
                   pl.BlockSpec((2 * H_B, tm), lambda i, j: (0, i))],
        out_shape=[jax.ShapeDtypeStruct((t, N_MAIN), F32),
                   jax.ShapeDtypeStruct((2 * H_B, t), F32)],
        scratch_shapes=[pltpu.VMEM((tm, d), BF16)],
        compiler_params=_cparams("parallel", "arbitrary"),
        name="inproj",
    )(x, mod_g, mod_g, g.reshape(1, d), w_main, w_if_t)


def _band_table(bq, pre, back):
    qi = np.arange(bq)[:, None] // CHUNK
    kj = np.arange(pre + bq)[None, :] // CHUNK
    return (kj >= qi) & (kj <= qi + back)


def _attn_a_kernel(*refs, bq, s, has_prefix):
    if has_prefix:
        q_ref, k_ref, v_ref, pk_ref, pv_ref, bias_ref, o_ref, kpad, vpad = refs
    else:
        q_ref, k_ref, v_ref, bias_ref, o_ref, kpad, vpad = refs
    n = pl.program_id(1)
    kw = PRE_A + bq

    @pl.when(n == 0)
    def _():
        if has_prefix:
            kpad[0:PRE_A, :] = pk_ref[0].astype(BF16)
            vpad[0:PRE_A, :] = pv_ref[0].astype(BF16)
        else:
            kpad[0:PRE_A, :] = jnp.zeros((PRE_A, WA), BF16)
            vpad[0:PRE_A, :] = jnp.zeros((PRE_A, WA), BF16)
        kpad[PRE_A:PRE_A + s, :] = k_ref[...].astype(BF16)
        vpad[PRE_A:PRE_A + s, :] = v_ref[...].astype(BF16)

    r0 = pl.multiple_of(n * bq, bq)
    q = (q_ref[...] * (DH_A ** -0.5)).astype(BF16)
    if not has_prefix:
        col = lax.broadcasted_iota(jnp.int32, (1, kw), 1)
        maskrow = jnp.where(col >= PRE_A - r0, 0.0, NEG).astype(F32)
    for h in range(H_A):
        hs = slice(h * DH_A, (h + 1) * DH_A)
        kh = kpad[pl.ds(r0, kw), hs]
        vh = vpad[pl.ds(r0, kw), hs]
        sc = lax.dot_general(q[:, hs], kh, (((1,), (1,)), ((), ())), preferred_element_type=F32)
        sc = sc + bias_ref[h]
        if not has_prefix:
            sc = sc + maskrow
        m = jnp.max(sc, axis=-1, keepdims=True)
        e = jnp.exp(sc - m)
        l = jnp.sum(e, axis=-1, keepdims=True)
        o = jnp.dot(e.astype(BF16), vh, preferred_element_type=F32)
        o_ref[:, hs] = o / l


def _attn_a(u, rel, *, nb, s, row0, bq, prefix=None, out=None):
    t = u.shape[0]
    nq = s // bq
    kw = PRE_A + bq
    i = np.arange(bq)[:, None]
    j = np.arange(kw)[None, :]
    relidx = np.clip(PRE_A + i - j, -REL_CLIP, REL_CLIP) + REL_CLIP
    bias = jnp.moveaxis(rel.astype(F32)[relidx], -1, 0)
    bias = jnp.where(_band_table(bq, PRE_A, BACK_A)[None], bias, NEG)
    rb, sb = row0 // bq, row0 // s
    in_specs = [pl.BlockSpec((bq, WA), lambda b, n: (rb + b * nq + n, COL_QA // WA)),
                pl.BlockSpec((s, WA), lambda b, n: (sb + b, COL_KA // WA)),
                pl.BlockSpec((s, WA), lambda b, n: (sb + b, COL_VA // WA))]
    args = [u, u, u]
    if prefix is not None:
        in_specs += [pl.BlockSpec((1, PRE_A, WA), lambda b, n: (b, 0, 0))] * 2
        args += [prefix[0], prefix[1]]
    in_specs.append(pl.BlockSpec((H_A, bq, kw), lambda b, n: (0, 0, 0)))
    args.append(bias)
    aliases = {}
    if out is not None:
        in_specs.append(pl.BlockSpec(memory_space=pl.ANY))
        args.append(out)
        aliases = {len(args) - 1: 0}
    body = functools.partial(_attn_a_kernel, bq=bq, s=s, has_prefix=prefix is not None)
    if out is not None:
        body = _drop_last_input(body, len(args))
    return pl.pallas_call(
        body,
        grid=(nb, nq),
        in_specs=in_specs,
        out_specs=pl.BlockSpec((bq, WA), lambda b, n: (rb + b * nq + n, 0)),
        out_shape=jax.ShapeDtypeStruct((t, WA), F32),
        scratch_shapes=[pltpu.VMEM((PRE_A + s, WA), BF16), pltpu.VMEM((PRE_A + s, WA), BF16)],
        input_output_aliases=aliases,
        compiler_params=_cparams("parallel", "arbitrary"),
        name="mixer_a_sample" if prefix is not None else "mixer_a_prompt",
    )(*args)


def _drop_last_input(body, n_in):
    def wrapped(*refs):
        return body(*refs[:n_in - 1], *refs[n_in:])
    return wrapped


def _rope(x, c, s1, s2):
    w = x.shape[1]
    return x * c + pltpu.roll(x, 8, 1) * s1 + pltpu.roll(x, w - 8, 1) * s2


def _attn_c_kernel(*refs, bq, s, has_prefix):
    if has_prefix:
        (q_ref, k_ref, v_ref, pk_ref, pv_ref, rc_ref, rs1_ref, rs2_ref, band_ref, sink_ref,
         o_ref, kr_ref, kpad, vpad) = refs
    else:
        (q_ref, k_ref, v_ref, rc_ref, rs1_ref, rs2_ref, band_ref, sink_ref,
         o_ref, kr_ref, kpad, vpad) = refs
    n = pl.program_id(1)
    kw = PRE_C + bq

    @pl.when(n == 0)
    def _():
        if has_prefix:
            kpad[0:PRE_C, :] = pk_ref[0].astype(BF16)
            vpad[0:PRE_C, :] = pv_ref[0].astype(BF16)
        else:
            kpad[0:PRE_C, :] = jnp.zeros((PRE_C, WC_KV), BF16)
            vpad[0:PRE_C, :] = jnp.zeros((PRE_C, WC_KV), BF16)
        kr = _rope(k_ref[...], rc_ref[...], rs1_ref[...], rs2_ref[...])
        kr_ref[...] = kr
        kpad[PRE_C:PRE_C + s, :] = kr.astype(BF16)
        vpad[PRE_C:PRE_C + s, :] = v_ref[...].astype(BF16)

    r0 = pl.multiple_of(n * bq, bq)
    rep = WC_Q // WC_KV
    rc = jnp.concatenate([rc_ref[pl.ds(r0, bq), :]] * rep, axis=1)
    rs1 = jnp.concatenate([rs1_ref[pl.ds(r0, bq), :]] * rep, axis=1)
    rs2 = jnp.concatenate([rs2_ref[pl.ds(r0, bq), :]] * rep, axis=1)
    q = (_rope(q_ref[...], rc, rs1, rs2) * (DH_C ** -0.5)).astype(BF16)
    mask = band_ref[...]
    if not has_prefix:
        col = lax.broadcasted_iota(jnp.int32, (1, kw), 1)
        mask = mask + jnp.where(col >= PRE_C - r0, 0.0, NEG).astype(F32)
    for h in range(HQ_C):
        g = h // (HQ_C // HKV_C)
        hs = slice(h * DH_C, (h + 1) * DH_C)
        gs = slice(g * DH_C, (g + 1) * DH_C)
        kh = kpad[pl.ds(r0, kw), gs]
        vh = vpad[pl.ds(r0, kw), gs]
        sc = lax.dot_general(q[:, hs], kh, (((1,), (1,)), ((), ())), preferred_element_type=F32)
        sc = sc + mask
        sk = sink_ref[h:h + 1, 0:1]
        m = jnp.maximum(jnp.max(sc, axis=-1, keepdims=True), sk)
        e = jnp.exp(sc - m)
        l = jnp.sum(e, axis=-1, keepdims=True) + jnp.exp(sk - m)
        o = jnp.dot(e.astype(BF16), vh, preferred_element_type=F32)
        o_ref[:, hs] = o / l


def _rope_tables(pos):
    half = ROT_DIM // 2
    inv_freq = 1.0 / (ROPE_THETA ** (jnp.arange(half, dtype=F32) * (2.0 / ROT_DIM)))
    ang = pos.astype(F32)[:, None] * inv_freq[None, :]
    cos, sin = jnp.cos(ang), jnp.sin(ang)
    n = pos.shape[0]
    one = jnp.ones((n, DH_C - ROT_DIM), F32)
    zero = jnp.zeros((n, DH_C - ROT_DIM), F32)
    zh = jnp.zeros((n, half), F32)
    c = jnp.concatenate([cos, cos, one], axis=1)
    s1 = jnp.concatenate([zh, sin, zero], axis=1)
    s2 = jnp.concatenate([-sin, zh, zero], axis=1)
    rep = WC_KV // DH_C
    return tuple(jnp.concatenate([a] * rep, axis=1) for a in (c, s1, s2))


def _attn_c(u, sink, *, nb, s, row0, bq, pos0, prefix=None, out=None, kr_out=None):
    t = u.shape[0]
    nq = s // bq
    kw = PRE_C + bq
    tables = _rope_tables(pos0 + jnp.arange(s))
    band = jnp.where(_band_table(bq, PRE_C, BACK_C), 0.0, NEG).astype(F32)
    sink_t = jnp.broadcast_to(sink.astype(F32)[:, None], (HQ_C, LANE))
    rb, sb = row0 // bq, row0 // s
    in_specs = [pl.BlockSpec((bq, WC_Q), lambda b, n: (rb + b * nq + n, COL_QC // WC_Q)),
                pl.BlockSpec((s, WC_KV), lambda b, n: (sb + b, COL_KC // WC_KV)),
                pl.BlockSpec((s, WC_KV), lambda b, n: (sb + b, COL_VC // WC_KV))]
    args = [u, u, u]
    if prefix is not None:
        in_specs += [pl.BlockSpec((1, PRE_C, WC_KV), lambda b, n: (b, 0, 0))] * 2
        args += [prefix[0], prefix[1]]
    in_specs += [pl.BlockSpec((s, WC_KV), lambda b, n: (0, 0))] * 3
    args += list(tables)
    in_specs += [pl.BlockSpec((bq, kw), lambda b, n: (0, 0)),
                 pl.BlockSpec((HQ_C, LANE), lambda b, n: (0, 0))]
    args += [band, sink_t]
    n_real = len(args)
    aliases = {}
    if out is not None:
        in_specs += [pl.BlockSpec(memory_space=pl.ANY)] * 2
        args += [out, kr_out]
        aliases = {n_real: 0, n_real + 1: 1}
    body = functools.partial(_attn_c_kernel, bq=bq, s=s, has_prefix=prefix is not None)
    if out is not None:
        inner = body

        def body(*refs):
            return inner(*refs[:n_real], *refs[n_real + 2:])
    return pl.pallas_call(
        body,
        grid=(nb, nq),
        in_specs=in_specs,
        out_specs=[pl.BlockSpec((bq, WC_Q), lambda b, n: (rb + b * nq + n, 0)),
                   pl.BlockSpec((s, WC_KV), lambda b, n: (sb + b, 0))],
        out_shape=[jax.ShapeDtypeStruct((t, WC_Q), F32), jax.ShapeDtypeStruct((t, WC_KV), F32)],
        scratch_shapes=[pltpu.VMEM((PRE_C + s, WC_KV), BF16), pltpu.VMEM((PRE_C + s, WC_KV), BF16)],
        input_output_aliases=aliases,
        compiler_params=_cparams("parallel", "arbitrary"),
        name="mixer_c_sample" if prefix is not None else "mixer_c_prompt",
    )(*args)


def _log_sigmoid(x):
    return jnp.minimum(x, 0.0) - jnp.log(1.0 + jnp.exp(-jnp.abs(x)))


def _lane_cumsum(x):
    n = x.shape[1]
    col = lax.broadcasted_iota(jnp.int32, x.shape, 1)
    k = 1
    while k < n:
        x = x + jnp.where(col >= k, pltpu.roll(x, k, 1), 0.0)
        k *= 2
    return x


def _mlstm_kernel(qk_ref, v_ref, og_ref, g_ref, hist_ref, cw_ref, cb_ref, bif_ref, ng_ref,
                  c0_ref, n0_ref, m0_ref, y_ref, cf_ref, nf_ref, mf_ref,
                  xpad, c_s, n_s, m_s, *, L):
    c = pl.program_id(1)

    @pl.when(c == 0)
    def _():
        c_s[...] = c0_ref[0]
        n_s[...] = n0_ref[0]
        m_s[...] = m0_ref[0]
        xpad[0:8, :] = hist_ref[0]

    x = qk_ref[...]
    xpad[8:8 + L, :] = x
    conv = (cb_ref[...] + x * cw_ref[3:4, :] + xpad[7:7 + L, :] * cw_ref[2:3, :]
            + xpad[6:6 + L, :] * cw_ref[1:2, :] + xpad[5:5 + L, :] * cw_ref[0:1, :])
    xpad[0:8, :] = xpad[L:L + 8, :]
    act = conv * jax.nn.sigmoid(conv)

    gb = g_ref[...] + bif_ref[:, 0:1]
    ig_rows = gb[0:H_B, :]
    b_rows = _lane_cumsum(_log_sigmoid(gb))[H_B:2 * H_B, :]

    row = lax.broadcasted_iota(jnp.int32, (L, L), 0)
    colm = lax.broadcasted_iota(jnp.int32, (L, L), 1)
    eye = row == colm
    causal = colm <= row

    def to_col(r):
        return jnp.sum(jnp.where(eye, r, 0.0), axis=1, keepdims=True)

    for h in range(H_B):
        br = b_rows[h:h + 1, 0:L]
        igr = ig_rows[h:h + 1, 0:L]
        bc = to_col(br)
        igc = to_col(igr)
        mprev = m_s[h][:, 0:1]
        logd = jnp.where(causal, bc + (igr - br), NEG)
        li = bc + mprev
        mt = jnp.maximum(li, jnp.max(logd, axis=1, keepdims=True))
        d = jnp.exp(logd - mt)
        wi = jnp.exp(li - mt)
        qh = act[:, h * DK_B:(h + 1) * DK_B]
        kh = act[:, WB_QK + h * DK_B:WB_QK + (h + 1) * DK_B] * (DK_B ** -0.5)
        vb = v_ref[:, h * DV_B:(h + 1) * DV_B].astype(BF16)
        qb = qh.astype(BF16)
        a = lax.dot_general(qb, kh.astype(BF16), (((1,), (1,)), ((), ())), preferred_element_type=F32) * d
        cmat = c_s[h]
        num = (jnp.dot(a.astype(BF16), vb, preferred_element_type=F32)
               + wi * jnp.dot(qb, cmat.astype(BF16), preferred_element_type=F32))
        nvec = n_s[h]
        den = jnp.sum(a, axis=1, keepdims=True) + wi * jnp.sum(qh * nvec, axis=1, keepdims=True)
        hh = num / jnp.maximum(jnp.abs(den), jnp.exp(-mt))
        hn = hh * lax.rsqrt(jnp.mean(hh * hh, axis=-1, keepdims=True) + EPS)
        vs = slice(h * DV_B, (h + 1) * DV_B)
        y_ref[:, vs] = hn * ng_ref[:, vs] * jax.nn.sigmoid(og_ref[:, vs])
        bl, ml, wprev = bc[L - 1:L, :], mt[L - 1:L, :], wi[L - 1:L, :]
        kwt = kh * jnp.exp(bl + igc - bc - ml)
        c_s[h] = wprev * cmat + lax.dot_general(kwt.astype(BF16), vb, (((0,), (0,)), ((), ())),
                                                preferred_element_type=F32)
        n_s[h] = wprev * nvec + jnp.sum(kwt, axis=0, keepdims=True)
        m_s[h] = jnp.broadcast_to(ml, (1, LANE))

    @pl.when(c == pl.num_programs(1) - 1)
    def _():
        cf_ref[0] = c_s[...]
        nf_ref[0] = n_s[...]
        mf_ref[0] = m_s[...]


def _mlstm(u, g_rows, hist, cw, cb, bif, ng, c0, n0, m0, *, nb, s, row0, L, out=None):
    t = u.shape[0]
    nc = s // L
    lp = g_rows.shape[2] // nc
    rb = row0 // L
    n0 = n0.reshape(nb, H_B, 1, DK_B)
    m0 = jnp.broadcast_to(m0.reshape(nb, H_B, 1, 1), (nb, H_B, 1, LANE))
    bif_t = jnp.broadcast_to(bif.astype(F32)[:, None], (2 * H_B, LANE))
    row_spec = lambda col: pl.BlockSpec((L, WB_V), lambda b, c: (rb + b * nc + c, col))
    full = lambda shape: pl.BlockSpec(shape, lambda b, c: (0,) * len(shape))
    state = lambda shape: pl.BlockSpec((1,) + shape, lambda b, c: (b,) + (0,) * len(shape))
    in_specs = [row_spec(COL_QKB // WB_V), row_spec(COL_VB // WB_V), row_spec(COL_OB // WB_V),
                pl.BlockSpec((None, 2 * H_B, lp), lambda b, c: (b, 0, c)),
                state((8, 2 * WB_QK)),
                full((CONV_B, 2 * WB_QK)), full((1, 2 * WB_QK)), full((2 * H_B, LANE)), full((1, WB_V)),
                state((H_B, DK_B, DV_B)), state((H_B, 1, DK_B)), state((H_B, 1, LANE))]
    args = [u, u, u, g_rows, hist, cw, cb.reshape(1, -1), bif_t, ng.reshape(1, -1), c0, n0, m0]
    n_real = len(args)
    aliases = {}
    body = functools.partial(_mlstm_kernel, L=L)
    if out is not None:
        in_specs.append(pl.BlockSpec(memory_space=pl.ANY))
        args.append(out)
        aliases = {n_real: 0}
        body = _drop_last_input(body, len(args))
    y, cf, nf, mf = pl.pallas_call(
        body,
        grid=(nb, nc),
        in_specs=in_specs,
        out_specs=[pl.BlockSpec((L, WB_V), lambda b, c: (rb + b * nc + c, 0)),
                   state((H_B, DK_B, DV_B)), state((H_B, 1, DK_B)), state((H_B, 1, LANE))],
        out_shape=[jax.ShapeDtypeStruct((t, WB_V), F32),
                   jax.ShapeDtypeStruct((nb, H_B, DK_B, DV_B), F32),
                   jax.ShapeDtypeStruct((nb, H_B, 1, DK_B), F32),
                   jax.ShapeDtypeStruct((nb, H_B, 1, LANE), F32)],
        scratch_shapes=[pltpu.VMEM((L + 8, 2 * WB_QK), F32),
                        pltpu.VMEM((H_B, DK_B, DV_B), F32),
                        pltpu.VMEM((H_B, 1, DK_B), F32),
                        pltpu.VMEM((H_B, 1, LANE), F32)],
        input_output_aliases=aliases,
        compiler_params=_cparams("parallel", "arbitrary"),
        name="mlstm_sample" if out is not None else "mlstm_prompt",
    )(*args)
    return y, cf, nf.reshape(nb, H_B, DK_B), mf[:, :, 0, 0]


def _merge_kernel(ya_ref, yb_ref, yc_ref, gt_ref, x_ref, g1_ref, wa_ref, wb_ref, wc_ref, wo_ref, o_ref):
    tm, d = x_ref.shape
    sg = jax.nn.sigmoid(gt_ref[...])
    m = (sg[:, 0:d] * jnp.dot(ya_ref[...].astype(BF16), wa_ref[...], preferred_element_type=F32)
         + sg[:, d:2 * d] * jnp.dot(yb_ref[...].astype(BF16), wb_ref[...], preferred_element_type=F32)
         + sg[:, 2 * d:3 * d] * jnp.dot(yc_ref[...].astype(BF16), wc_ref[...], preferred_element_type=F32))
    o = jnp.dot(m.astype(BF16), wo_ref[...], preferred_element_type=F32)
    o = (o.reshape(tm // CHUNK, CHUNK, d) * g1_ref[...]).reshape(tm, d)
    o_ref[...] = x_ref[...] + o


def _merge(ya, yb, yc, u, x, mod_g, wa, wb, wc, wo):
    t, d = x.shape
    tm = _row_tile(t)
    row = lambda w, col=0: pl.BlockSpec((tm, w), lambda i: (i, col))
    full = lambda a: pl.BlockSpec(a.shape, lambda i: (0, 0))
    return pl.pallas_call(
        _merge_kernel,
        grid=(t // tm,),
        in_specs=[row(WA), row(WB_V), row(WC_Q), row(3 * d, COL_GATES // (3 * d)), row(d),
                  _mod_spec(tm, 2), full(wa), full(wb), full(wc), full(wo)],
        out_specs=row(d),
        out_shape=jax.ShapeDtypeStruct((t, d), F32),
        compiler_params=_cparams("parallel"),
        name="merge",
    )(ya, yb, yc, u, x, mod_g, wa, wb, wc, wo)


def _ffn_kernel(x_ref, sh_ref, sc_ref, g_ref, g2_ref, w1_ref, w3_ref, w2_ref, o_ref, h_scr, acc):
    j = pl.program_id(1)
    tm, d = x_ref.shape

    @pl.when(j == 0)
    def _():
        h_scr[...] = _ln_mod(x_ref[...], g_ref[...], sc_ref[...], sh_ref[...]).astype(BF16)
        acc[...] = jnp.zeros_like(acc)

    h = h_scr[...]
    a = jnp.dot(h, w1_ref[...], preferred_element_type=F32)
    b = jnp.dot(h, w3_ref[...], preferred_element_type=F32)
    tt = (a * jax.nn.sigmoid(a) * b).astype(BF16)
    acc[...] += jnp.dot(tt, w2_ref[...], preferred_element_type=F32)

    @pl.when(j == pl.num_programs(1) - 1)
    def _():
        f = (acc[...].reshape(tm // CHUNK, CHUNK, d) * g2_ref[...]).reshape(tm, d)
        o_ref[...] = x_ref[...] + f


def _ffn(x, mod_g, g, w1, w3, w2):
    t, d = x.shape
    tm = _row_tile(t)
    ff = w1.shape[1]
    tf = ff // 2
    return pl.pallas_call(
        _ffn_kernel,
        grid=(t // tm, ff // tf),
        in_specs=[pl.BlockSpec((tm, d), lambda i, j: (i, 0)),
                  _mod_spec(tm, 3), _mod_spec(tm, 4),
                  pl.BlockSpec((1, d), lambda i, j: (0, 0)),
                  _mod_spec(tm, 5),
                  pl.BlockSpec((d, tf), lambda i, j: (0, j)),
                  pl.BlockSpec((d, tf), lambda i, j: (0, j)),
                  pl.BlockSpec((tf, d), lambda i, j: (j, 0))],
        out_specs=pl.BlockSpec((tm, d), lambda i, j: (i, 0)),
        out_shape=jax.ShapeDtypeStruct((t, d), F32),
        scratch_shapes=[pltpu.VMEM((tm, d), BF16), pltpu.VMEM((tm, d), F32)],
        compiler_params=_cparams("parallel", "arbitrary"),
        name="ffn_dense",
    )(x, mod_g, mod_g, g.reshape(1, d), mod_g, w1, w3, w2)


def _router_kernel(x_ref, sh_ref, sc_ref, g_ref, wr_ref, br_ref, h_ref, gate_ref):
    h = _ln_mod(x_ref[...], g_ref[...], sc_ref[...], sh_ref[...]).astype(BF16)
    h_ref[...] = h
    lg = jnp.dot(h, wr_ref[...], preferred_element_type=F32) + br_ref[...]
    lane = lax.broadcasted_iota(jnp.int32, lg.shape, 1).astype(F32)
    m1 = jnp.max(lg, axis=1, keepdims=True)
    i1 = jnp.min(jnp.where(lg == m1, lane, float(LANE)), axis=1, keepdims=True)
    lg2 = jnp.where(lane == i1, 2.0 * NEG, lg)
    m2 = jnp.max(lg2, axis=1, keepdims=True)
    i2 = jnp.min(jnp.where(lg2 == m2, lane, float(LANE)), axis=1, keepdims=True)
    e2 = jnp.exp(m2 - m1)
    w1 = 1.0 / (1.0 + e2)
    gate_ref[...] = jnp.where(lane == i1, w1, 0.0) + jnp.where(lane == i2, e2 * w1, 0.0)


def _router(x, mod_g, g, w_r, b_r):
    t, d = x.shape
    tm = _row_tile(t)
    wr = jnp.zeros((d, LANE), BF16).at[:, :N_EXPERTS].set(w_r.astype(BF16))
    br = jnp.full((1, LANE), NEG, F32).at[0, :N_EXPERTS].set(b_r.astype(F32))
    return pl.pallas_call(
        _router_kernel,
        grid=(t // tm,),
        in_specs=[pl.BlockSpec((tm, d), lambda i: (i, 0)),
                  _mod_spec(tm, 3), _mod_spec(tm, 4),
                  pl.BlockSpec((1, d), lambda i: (0, 0)),
                  pl.BlockSpec((d, LANE), lambda i: (0, 0)),
                  pl.BlockSpec((1, LANE), lambda i: (0, 0))],
        out_specs=[pl.BlockSpec((tm, d), lambda i: (i, 0)), pl.BlockSpec((tm, LANE), lambda i: (i, 0))],
        out_shape=[jax.ShapeDtypeStruct((t, d), BF16), jax.ShapeDtypeStruct((t, LANE), F32)],
        compiler_params=_cparams("parallel"),
        name="router",
    )(x, mod_g, mod_g, g.reshape(1, d), wr, br)


def _moe_dense_kernel(h_ref, gate_ref, x_ref, g2_ref, w1_ref, w3_ref, w2_ref, o_ref, acc):
    e, j = pl.program_id(1), pl.program_id(2)
    tm, d = x_ref.shape

    @pl.when((e == 0) & (j == 0))
    def _():
        acc[...] = jnp.zeros_like(acc)

    h = h_ref[...]
    a = jnp.dot(h, w1_ref[0].astype(BF16), preferred_element_type=F32)
    b = jnp.dot(h, w3_ref[0].astype(BF16), preferred_element_type=F32)
    tt = (a * jax.nn.sigmoid(a) * b).astype(BF16)
    lane = lax.broadcasted_iota(jnp.int32, gate_ref.shape, 1)
    ge = jnp.sum(jnp.where(lane == e, gate_ref[...], 0.0), axis=1, keepdims=True)
    acc[...] += ge * jnp.dot(tt, w2_ref[0].astype(BF16), preferred_element_type=F32)

    @pl.when((e == pl.num_programs(1) - 1) & (j == pl.num_programs(2) - 1))
    def _():
        f = (acc[...].reshape(tm // CHUNK, CHUNK, d) * g2_ref[...]).reshape(tm, d)
        o_ref[...] = x_ref[...] + f


def _moe_dense(h, gate, x, mod_g, w1, w3, w2):
    t, d = x.shape
    tm = _row_tile(t)
    ne, _, ff = w1.shape
    tf = 512
    return pl.pallas_call(
        _moe_dense_kernel,
        grid=(t // tm, ne, ff // tf),
        in_specs=[pl.BlockSpec((tm, d), lambda i, e, j: (i, 0)),
                  pl.BlockSpec((tm, LANE), lambda i, e, j: (i, 0)),
                  pl.BlockSpec((tm, d), lambda i, e, j: (i, 0)),
                  _mod_spec(tm, 5),
                  pl.BlockSpec((1, d, tf), lambda i, e, j: (e, 0, j)),
                  pl.BlockSpec((1, d, tf), lambda i, e, j: (e, 0, j)),
                  pl.BlockSpec((1, tf, d), lambda i, e, j: (e, j, 0))],
        out_specs=pl.BlockSpec((tm, d), lambda i, e, j: (i, 0)),
        out_shape=jax.ShapeDtypeStruct((t, d), F32),
        scratch_shapes=[pltpu.VMEM((tm, d), F32)],
        compiler_params=_cparams("parallel", "arbitrary", "arbitrary"),
        name="moe_dense",
    )(h, gate, x, mod_g, w1, w3, w2)


def _final_kernel(x_ref, g_ref, o_ref):
    x = x_ref[...]
    o_ref[...] = x * lax.rsqrt(jnp.mean(x * x, axis=-1, keepdims=True) + EPS) * g_ref[...]


def _final_norm(x, g):
    t, d = x.shape
    tm = _row_tile(t)
    return pl.pallas_call(
        _final_kernel,
        grid=(t // tm,),
        in_specs=[pl.BlockSpec((tm, d), lambda i: (i, 0)), pl.BlockSpec((1, d), lambda i: (0, 0))],
        out_specs=pl.BlockSpec((tm, d), lambda i: (i, 0)),
        out_shape=jax.ShapeDtypeStruct((t, d), F32),
        compiler_params=_cparams("parallel"),
        name="final_norm",
    )(x, g.reshape(1, d))


def _split_w_in(w):
    o = np.cumsum([0, WA, WA, WA, 2 * WB_QK, WB_V, 2 * H_B, WB_V, WC_Q, WC_KV, WC_KV, 3 * D_MODEL])
    seg = lambda k: w[:, o[k]:o[k + 1]]
    qa, ka, va, qkb, vb, ifb, ob, qc, kc, vc, gates = (seg(k) for k in range(11))
    main = jnp.concatenate([qa, ka, va, qc, qkb, gates, vb, ob, kc, vc], axis=1).astype(BF16)
    return main, ifb.T.astype(BF16)


def kernel(x_prompt, x_sample, c_prompt, c_sample, cache_a_k, cache_a_v, state_b_C, state_b_n, state_b_m, state_b_conv, cache_c_k, cache_c_v, norm1_g, norm2_g, w_ada, b_ada, w_in, b_if_b, conv_w_b, conv_b_b, norm_b_g, rel_a, sink_c, w_br_a, w_br_b, w_br_c, w_o, w_ff1, w_ff3, w_ff2, w_router, b_router, w_e1, w_e3, w_e2, norm_f_g):
    nbp, sp, d = x_prompt.shape
    nbs, ss, _ = x_sample.shape
    tp, ts = nbp * sp, nbs * ss
    x = jnp.concatenate([x_prompt.reshape(tp, d), x_sample.reshape(ts, d)], axis=0)
    cond = jnp.concatenate([c_prompt, c_sample], axis=0)
    group_batch = np.concatenate([np.repeat(np.arange(nbp), sp // CHUNK), nbp + np.repeat(np.arange(nbs), ss // CHUNK)])

    lp_s = max(ss, LANE)
    l_b = 256 if sp % 256 == 0 else CHUNK
    keep_a, keep_c = min(PRE_A, sp), min(PRE_C, sp)
    zeros_state = (jnp.zeros((nbp, H_B, DK_B, DV_B), F32), jnp.zeros((nbp, H_B, DK_B), F32), jnp.zeros((nbp, H_B), F32))
    new = {k: [] for k in ("pak", "pav", "pbc", "pbn", "pbm", "pbx", "pck", "pcv",
                           "sak", "sav", "sbc", "sbn", "sbm", "sbx", "sck", "scv")}

    for l in range(DEPTH):
        mod = _ada(cond, w_ada[l], b_ada[l])
        mod_g = mod[group_batch].reshape(-1, 1, 6 * d)
        w_main, w_if_t = _split_w_in(w_in[l])
        u, g_t = _inproj(x, mod_g, norm1_g[l], w_main, w_if_t)

        ya = _attn_a(u, rel_a[l], nb=nbp, s=sp, row0=0, bq=2 * CHUNK)
        ya = _attn_a(u, rel_a[l], nb=nbs, s=ss, row0=tp, bq=ss,
                     prefix=(cache_a_k[l].reshape(nbs, PRE_A, WA), cache_a_v[l].reshape(nbs, PRE_A, WA)), out=ya)
        yc, kr = _attn_c(u, sink_c[l], nb=nbp, s=sp, row0=0, bq=2 * CHUNK, pos0=0)
        yc, kr = _attn_c(u, sink_c[l], nb=nbs, s=ss, row0=tp, bq=ss, pos0=PAST_LEN,
                         prefix=(cache_c_k[l].reshape(nbs, PRE_C, WC_KV), cache_c_v[l].reshape(nbs, PRE_C, WC_KV)),
                         out=yc, kr_out=kr)
        g_p = g_t[:, :tp].reshape(2 * H_B, nbp, sp).transpose(1, 0, 2)
        g_s = g_t[:, tp:].reshape(2 * H_B, nbs, ss).transpose(1, 0, 2)
        g_s = jnp.pad(g_s, ((0, 0), (0, 0), (0, lp_s - ss)))
        hist_p = jnp.zeros((nbp, 8, 2 * WB_QK), F32)
        hist_s = jnp.pad(state_b_conv[l], ((0, 0), (8 - (CONV_B - 1), 0), (0, 0)))
        bargs = (conv_w_b[l], conv_b_b[l], b_if_b[l], norm_b_g[l])
        yb, pbc, pbn, pbm = _mlstm(u, g_p, hist_p, *bargs, *zeros_state, nb=nbp, s=sp, row0=0, L=l_b)
        yb, sbc, sbn, sbm = _mlstm(u, g_s, hist_s, *bargs, state_b_C[l], state_b_n[l], state_b_m[l],
                                   nb=nbs, s=ss, row0=tp, L=ss, out=yb)

        x = _merge(ya, yb, yc, u, x, mod_g, w_br_a[l].astype(BF16), w_br_b[l].astype(BF16),
                   w_br_c[l].astype(BF16), w_o[l].astype(BF16))
        i = l // 2
        if l % 2 == 0:
            x = _ffn(x, mod_g, norm2_g[l], w_ff1[i].astype(BF16), w_ff3[i].astype(BF16), w_ff2[i].astype(BF16))
        else:
            h2, gate = _router(x, mod_g, norm2_g[l], w_router[i], b_router[i])
            x = _moe_dense(h2, gate, x, mod_g, w_e1[i], w_e3[i], w_e2[i])

        up = u[:tp].reshape(nbp, sp, N_MAIN)
        us = u[tp:].reshape(nbs, ss, N_MAIN)
        new["pak"].append(up[:, sp - keep_a:, COL_KA:COL_KA + WA].reshape(nbp, keep_a, H_A, DH_A))
        new["pav"].append(up[:, sp - keep_a:, COL_VA:COL_VA + WA].reshape(nbp, keep_a, H_A, DH_A))
        new["pbc"].append(pbc)
        new["pbn"].append(pbn)
        new["pbm"].append(pbm)
        new["pbx"].append(up[:, sp - (CONV_B - 1):, COL_QKB:COL_QKB + 2 * WB_QK])
        new["pck"].append(kr[:tp].reshape(nbp, sp, HKV_C, DH_C)[:, sp - keep_c:])
        new["pcv"].append(up[:, sp - keep_c:, COL_VC:COL_VC + WC_KV].reshape(nbp, keep_c, HKV_C, DH_C))
        new["sak"].append(us[:, :, COL_KA:COL_KA + WA].reshape(nbs, ss, H_A, DH_A))
        new["sav"].append(us[:, :, COL_VA:COL_VA + WA].reshape(nbs, ss, H_A, DH_A))
        new["sbc"].append(sbc)
        new["sbn"].append(sbn)
        new["sbm"].append(sbm)
        new["sbx"].append(us[:, ss - (CONV_B - 1):, COL_QKB:COL_QKB + 2 * WB_QK])
        new["sck"].append(kr[tp:].reshape(nbs, ss, HKV_C, DH_C))
        new["scv"].append(us[:, :, COL_VC:COL_VC + WC_KV].reshape(nbs, ss, HKV_C, DH_C))

    y = _final_norm(x, norm_f_g)
    st = {k: jnp.stack(v) for k, v in new.items()}
    return (y[:tp].reshape(nbp, sp, d), y[tp:].reshape(nbs, ss, d),
            st["pak"], st["pav"], st["pbc"], st["pbn"], st["pbm"], st["pbx"], st["pck"], st["pcv"],
            st["sak"], st["sav"], st["sbc"], st["sbn"], st["sbm"], st["sbx"], st["sck"], st["scv"])
```

```python
import functools

import numpy as np
import jax
import jax.numpy as jnp
from jax import lax
from jax.experimental import pallas as pl
from jax.experimental.pallas import tpu as pltpu

F32 = jnp.float32
BF16 = jnp.bfloat16

D_MODEL = 1024
DEPTH = 2
PAST_LEN = 2048
CHUNK = 64
H_A, DH_A, BACK_A, REL_CLIP = 8, 64, 8, 128
H_B, DK_B, DV_B, CONV_B = 4, 128, 256, 4
HQ_C, HKV_C, DH_C, BACK_C = 8, 2, 64, 2
ROT_DIM = DH_C // 4
ROPE_THETA = 500000.0
D_FF = 2816
N_EXPERTS = 8
D_FF_E = 3584
EPS = 1e-6
NEG = -1e30

WA = H_A * DH_A
WB_QK = H_B * DK_B
WB_V = H_B * DV_B
WC_Q = HQ_C * DH_C
WC_KV = HKV_C * DH_C
PRE_A = BACK_A * CHUNK
PRE_C = BACK_C * CHUNK

COL_QA, COL_KA, COL_VA, COL_QC = 0, 512, 1024, 1536
COL_QKB = 2048
COL_GATES = 3072
COL_VB = 6144
COL_OB = 7168
COL_KC = 8192
COL_VC = 8320
N_MAIN = 8448

LANE = 128
VMEM_LIMIT = 48 * 1024 * 1024


def _cparams(*sem):
    return pltpu.CompilerParams(dimension_semantics=sem, vmem_limit_bytes=VMEM_LIMIT)


def _row_tile(t):
    for tm in (512, 256, 128, 64):
        if t % tm == 0:
            return tm
    raise ValueError(f"token count {t} is not a multiple of 64")


def _ln_mod(x, g, sc, sh):
    tm, d = x.shape
    y = x * lax.rsqrt(jnp.mean(x * x, axis=-1, keepdims=True) + EPS) * g
    y = y.reshape(tm // CHUNK, CHUNK, d) * (1.0 + sc) + sh
    return y.reshape(tm, d)


def _mod_spec(tm, kind):
    return pl.BlockSpec((tm // CHUNK, 1, D_MODEL), lambda i, *_, k=kind: (i, 0, k))


def _ada_kernel(c_ref, w_ref, b_ref, o_ref):
    c = c_ref[...]
    a = (c * jax.nn.sigmoid(c)).astype(BF16)
    o_ref[...] = jnp.dot(a, w_ref[...].astype(BF16), preferred_element_type=F32) + b_ref[...]


def _ada(c, w, b):
    nb, d = c.shape
    n = w.shape[1]
    tn = 1536
    return pl.pallas_call(
        _ada_kernel,
        grid=(n // tn,),
        in_specs=[pl.BlockSpec((nb, d), lambda j: (0, 0)),
                  pl.BlockSpec((d, tn), lambda j: (0, j)),
                  pl.BlockSpec((1, tn), lambda j: (0, j))],
        out_specs=pl.BlockSpec((nb, tn), lambda j: (0, j)),
        out_shape=jax.ShapeDtypeStruct((nb, n), F32),
        compiler_params=_cparams("arbitrary"),
        name="adaln",
    )(c, w, b.reshape(1, n))


def _col_chunks(n, width=768):
    return [(c, min(width, n - c)) for c in range(0, n, width)]


def _resident(shape):
    return pl.BlockSpec(shape, lambda *_: (0,) * len(shape), pipeline_mode=pl.Buffered(1))


def _inproj_kernel(x_ref, sh_ref, sc_ref, g_ref, w_ref, wif_ref, u_ref, gt_ref, h_scr):
    h_scr[...] = _ln_mod(x_ref[...], g_ref[...], sc_ref[...], sh_ref[...]).astype(BF16)
    gt_ref[...] = lax.dot_general(wif_ref[...], h_scr[...], (((1,), (1,)), ((), ())), preferred_element_type=F32)
    for c, w in _col_chunks(N_MAIN):
        u_ref[:, c:c + w] = jnp.dot(h_scr[...], w_ref[:, c:c + w], preferred_element_type=F32)


def _inproj(x, mod_g, g, w_main, w_if_t):
    t, d = x.shape
    tm = min(256, _row_tile(t))
    return pl.pallas_call(
        _inproj_kernel,
        grid=(t // tm,),
        in_specs=[pl.BlockSpec((tm, d), lambda i: (i, 0)),
                  _mod_spec(tm, 0), _mod_spec(tm, 1),
                  _resident((1, d)), _resident((d, N_MAIN)), _resident((2 * H_B, d))],
        out_specs=[pl.BlockSpec((tm, N_MAIN), lambda i: (i, 0)),
                   pl.BlockSpec((2 * H_B, tm), lambda i: (0, i))],
        out_shape=[jax.ShapeDtypeStruct((t, N_MAIN), F32),
                   jax.ShapeDtypeStruct((2 * H_B, t), F32)],
        scratch_shapes=[pltpu.VMEM((tm, d), BF16)],
        compiler_params=_cparams("parallel"),
        name="inproj",
    )(x, mod_g, mod_g, g.reshape(1, d), w_main, w_if_t)


def _band_table(bq, pre, back):
    qi = np.arange(bq)[:, None] // CHUNK
    kj = np.arange(pre + bq)[None, :] // CHUNK
    return (kj >= qi) & (kj <= qi + back)


def _attn_a_kernel(*refs, bq, s, has_prefix):
    if has_prefix:
        q_ref, k_ref, v_ref, pk_ref, pv_ref, bias_ref, o_ref, kpad, vpad = refs
    else:
        q_ref, k_ref, v_ref, bias_ref, o_ref, kpad, vpad = refs
    n = pl.program_id(1)
    kw = PRE_A + bq

    @pl.when(n == 0)
    def _():
        if has_prefix:
            kpad[0:PRE_A, :] = pk_ref[0].astype(BF16)
            vpad[0:PRE_A, :] = pv_ref[0].astype(BF16)
        else:
            kpad[0:PRE_A, :] = jnp.zeros((PRE_A, WA), BF16)
            vpad[0:PRE_A, :] = jnp.zeros((PRE_A, WA), BF16)
        kpad[PRE_A:PRE_A + s, :] = k_ref[...].astype(BF16)
        vpad[PRE_A:PRE_A + s, :] = v_ref[...].astype(BF16)

    r0 = pl.multiple_of(n * bq, bq)
    q = (q_ref[...] * (DH_A ** -0.5)).astype(BF16)
    if not has_prefix:
        col = lax.broadcasted_iota(jnp.int32, (1, kw), 1)
        maskrow = jnp.where(col >= PRE_A - r0, 0.0, NEG).astype(F32)
    for h in range(H_A):
        hs = slice(h * DH_A, (h + 1) * DH_A)
        kh = kpad[pl.ds(r0, kw), hs]
        vh = vpad[pl.ds(r0, kw), hs]
        sc = lax.dot_general(q[:, hs], kh, (((1,), (1,)), ((), ())), preferred_element_type=F32)
        sc = sc + bias_ref[h]
        if not has_prefix:
            sc = sc + maskrow
        m = jnp.max(sc, axis=-1, keepdims=True)
        e = jnp.exp(sc - m)
        l = jnp.sum(e, axis=-1, keepdims=True)
        o = jnp.dot(e.astype(BF16), vh, preferred_element_type=F32)
        o_ref[:, hs] = o / l


def _attn_a(u, rel, *, nb, s, row0, bq, prefix=None, out=None):
    t = u.shape[0]
    nq = s // bq
    kw = PRE_A + bq
    m = np.arange(bq - 1 + kw)
    ext = rel.astype(F32)[np.clip(PRE_A + bq - 1 - m, -REL_CLIP, REL_CLIP) + REL_CLIP].T
    bias = jnp.stack([lax.slice(ext, (0, bq - 1 - i), (H_A, bq - 1 - i + kw)) for i in range(bq)], axis=1)
    bias = jnp.where(_band_table(bq, PRE_A, BACK_A)[None], bias, NEG)
    rb, sb = row0 // bq, row0 // s
    in_specs = [pl.BlockSpec((bq, WA), lambda b, n: (rb + b * nq + n, COL_QA // WA)),
                pl.BlockSpec((s, WA), lambda b, n: (sb + b, COL_KA // WA)),
                pl.BlockSpec((s, WA), lambda b, n: (sb + b, COL_VA // WA))]
    args = [u, u, u]
    if prefix is not None:
        in_specs += [pl.BlockSpec((1, PRE_A, WA), lambda b, n: (b, 0, 0))] * 2
        args += [prefix[0], prefix[1]]
    in_specs.append(pl.BlockSpec((H_A, bq, kw), lambda b, n: (0, 0, 0)))
    args.append(bias)
    aliases = {}
    if out is not None:
        in_specs.append(pl.BlockSpec(memory_space=pl.ANY))
        args.append(out)
        aliases = {len(args) - 1: 0}
    body = functools.partial(_attn_a_kernel, bq=bq, s=s, has_prefix=prefix is not None)
    if out is not None:
        body = _drop_last_input(body, len(args))
    return pl.pallas_call(
        body,
        grid=(nb, nq),
        in_specs=in_specs,
        out_specs=pl.BlockSpec((bq, WA), lambda b, n: (rb + b * nq + n, 0)),
        out_shape=jax.ShapeDtypeStruct((t, WA), F32),
        scratch_shapes=[pltpu.VMEM((PRE_A + s, WA), BF16), pltpu.VMEM((PRE_A + s, WA), BF16)],
        input_output_aliases=aliases,
        compiler_params=_cparams("parallel", "arbitrary"),
        name="mixer_a_sample" if prefix is not None else "mixer_a_prompt",
    )(*args)


def _drop_last_input(body, n_in):
    def wrapped(*refs):
        return body(*refs[:n_in - 1], *refs[n_in:])
    return wrapped


def _rope(x, c, s1, s2):
    w = x.shape[1]
    return x * c + pltpu.roll(x, 8, 1) * s1 + pltpu.roll(x, w - 8, 1) * s2


def _attn_c_kernel(*refs, bq, s, has_prefix):
    if has_prefix:
        (q_ref, k_ref, v_ref, pk_ref, pv_ref, rc_ref, rs1_ref, rs2_ref, band_ref, sink_ref,
         o_ref, kr_ref, kpad, vpad) = refs
    else:
        (q_ref, k_ref, v_ref, rc_ref, rs1_ref, rs2_ref, band_ref, sink_ref,
         o_ref, kr_ref, kpad, vpad) = refs
    n = pl.program_id(1)
    kw = PRE_C + bq

    @pl.when(n == 0)
    def _():
        if has_prefix:
            kpad[0:PRE_C, :] = pk_ref[0].astype(BF16)
            vpad[0:PRE_C, :] = pv_ref[0].astype(BF16)
        else:
            kpad[0:PRE_C, :] = jnp.zeros((PRE_C, WC_KV), BF16)
            vpad[0:PRE_C, :] = jnp.zeros((PRE_C, WC_KV), BF16)
        kr = _rope(k_ref[...], rc_ref[...], rs1_ref[...], rs2_ref[...])
        kr_ref[...] = kr
        kpad[PRE_C:PRE_C + s, :] = kr.astype(BF16)
        vpad[PRE_C:PRE_C + s, :] = v_ref[...].astype(BF16)

    r0 = pl.multiple_of(n * bq, bq)
    rep = WC_Q // WC_KV
    rc = jnp.concatenate([rc_ref[pl.ds(r0, bq), :]] * rep, axis=1)
    rs1 = jnp.concatenate([rs1_ref[pl.ds(r0, bq), :]] * rep, axis=1)
    rs2 = jnp.concatenate([rs2_ref[pl.ds(r0, bq), :]] * rep, axis=1)
    q = (_rope(q_ref[...], rc, rs1, rs2) * (DH_C ** -0.5)).astype(BF16)
    mask = band_ref[...]
    if not has_prefix:
        col = lax.broadcasted_iota(jnp.int32, (1, kw), 1)
        mask = mask + jnp.where(col >= PRE_C - r0, 0.0, NEG).astype(F32)
    for h in range(HQ_C):
        g = h // (HQ_C // HKV_C)
        hs = slice(h * DH_C, (h + 1) * DH_C)
        gs = slice(g * DH_C, (g + 1) * DH_C)
        kh = kpad[pl.ds(r0, kw), gs]
        vh = vpad[pl.ds(r0, kw), gs]
        sc = lax.dot_general(q[:, hs], kh, (((1,), (1,)), ((), ())), preferred_element_type=F32)
        sc = sc + mask
        sk = sink_ref[h:h + 1, 0:1]
        m = jnp.maximum(jnp.max(sc, axis=-1, keepdims=True), sk)
        e = jnp.exp(sc - m)
        l = jnp.sum(e, axis=-1, keepdims=True) + jnp.exp(sk - m)
        o = jnp.dot(e.astype(BF16), vh, preferred_element_type=F32)
        o_ref[:, hs] = o / l


def _rope_tables(pos):
    half = ROT_DIM // 2
    inv_freq = 1.0 / (ROPE_THETA ** (jnp.arange(half, dtype=F32) * (2.0 / ROT_DIM)))
    ang = pos.astype(F32)[:, None] * inv_freq[None, :]
    cos, sin = jnp.cos(ang), jnp.sin(ang)
    n = pos.shape[0]
    one = jnp.ones((n, DH_C - ROT_DIM), F32)
    zero = jnp.zeros((n, DH_C - ROT_DIM), F32)
    zh = jnp.zeros((n, half), F32)
    c = jnp.concatenate([cos, cos, one], axis=1)
    s1 = jnp.concatenate([zh, sin, zero], axis=1)
    s2 = jnp.concatenate([-sin, zh, zero], axis=1)
    rep = WC_KV // DH_C
    return tuple(jnp.concatenate([a] * rep, axis=1) for a in (c, s1, s2))


def _attn_c(u, sink, *, nb, s, row0, bq, pos0, prefix=None, out=None, kr_out=None):
    t = u.shape[0]
    nq = s // bq
    kw = PRE_C + bq
    tables = _rope_tables(pos0 + jnp.arange(s))
    band = jnp.where(_band_table(bq, PRE_C, BACK_C), 0.0, NEG).astype(F32)
    sink_t = jnp.broadcast_to(sink.astype(F32)[:, None], (HQ_C, LANE))
    rb, sb = row0 // bq, row0 // s
    in_specs = [pl.BlockSpec((bq, WC_Q), lambda b, n: (rb + b * nq + n, COL_QC // WC_Q)),
                pl.BlockSpec((s, WC_KV), lambda b, n: (sb + b, COL_KC // WC_KV)),
                pl.BlockSpec((s, WC_KV), lambda b, n: (sb + b, COL_VC // WC_KV))]
    args = [u, u, u]
    if prefix is not None:
        in_specs += [pl.BlockSpec((1, PRE_C, WC_KV), lambda b, n: (b, 0, 0))] * 2
        args += [prefix[0], prefix[1]]
    in_specs += [pl.BlockSpec((s, WC_KV), lambda b, n: (0, 0))] * 3
    args += list(tables)
    in_specs += [pl.BlockSpec((bq, kw), lambda b, n: (0, 0)),
                 pl.BlockSpec((HQ_C, LANE), lambda b, n: (0, 0))]
    args += [band, sink_t]
    n_real = len(args)
    aliases = {}
    if out is not None:
        in_specs += [pl.BlockSpec(memory_space=pl.ANY)] * 2
        args += [out, kr_out]
        aliases = {n_real: 0, n_real + 1: 1}
    body = functools.partial(_attn_c_kernel, bq=bq, s=s, has_prefix=prefix is not None)
    if out is not None:
        inner = body

        def body(*refs):
            return inner(*refs[:n_real], *refs[n_real + 2:])
    return pl.pallas_call(
        body,
        grid=(nb, nq),
        in_specs=in_specs,
        out_specs=[pl.BlockSpec((bq, WC_Q), lambda b, n: (rb + b * nq + n, 0)),
                   pl.BlockSpec((s, WC_KV), lambda b, n: (sb + b, 0))],
        out_shape=[jax.ShapeDtypeStruct((t, WC_Q), F32), jax.ShapeDtypeStruct((t, WC_KV), F32)],
        scratch_shapes=[pltpu.VMEM((PRE_C + s, WC_KV), BF16), pltpu.VMEM((PRE_C + s, WC_KV), BF16)],
        input_output_aliases=aliases,
        compiler_params=_cparams("parallel", "arbitrary"),
        name="mixer_c_sample" if prefix is not None else "mixer_c_prompt",
    )(*args)


def _log_sigmoid(x):
    return jnp.minimum(x, 0.0) - jnp.log(1.0 + jnp.exp(-jnp.abs(x)))


def _lane_cumsum(x):
    n = x.shape[1]
    col = lax.broadcasted_iota(jnp.int32, x.shape, 1)
    k = 1
    while k < n:
        x = x + jnp.where(col >= k, pltpu.roll(x, k, 1), 0.0)
        k *= 2
    return x


def _mlstm_kernel(qk_ref, v_ref, og_ref, g_ref, hist_ref, cw_ref, cb_ref, bif_ref, ng_ref,
                  c0_ref, n0_ref, m0_ref, y_ref, cf_ref, nf_ref, mf_ref,
                  xpad, c_s, n_s, m_s, *, L):
    c = pl.program_id(1)

    @pl.when(c == 0)
    def _():
        c_s[...] = c0_ref[0]
        n_s[...] = n0_ref[0]
        m_s[...] = m0_ref[0]
        xpad[0:8, :] = hist_ref[0]

    x = qk_ref[...]
    xpad[8:8 + L, :] = x
    conv = (cb_ref[...] + x * cw_ref[3:4, :] + xpad[7:7 + L, :] * cw_ref[2:3, :]
            + xpad[6:6 + L, :] * cw_ref[1:2, :] + xpad[5:5 + L, :] * cw_ref[0:1, :])
    xpad[0:8, :] = xpad[L:L + 8, :]
    act = conv * jax.nn.sigmoid(conv)

    gb = g_ref[...] + bif_ref[:, 0:1]
    ig_rows = gb[0:H_B, :]
    b_rows = _lane_cumsum(_log_sigmoid(gb))[H_B:2 * H_B, :]

    row = lax.broadcasted_iota(jnp.int32, (L, L), 0)
    colm = lax.broadcasted_iota(jnp.int32, (L, L), 1)
    eye = row == colm
    causal = colm <= row

    def to_col(r):
        return jnp.sum(jnp.where(eye, r, 0.0), axis=1, keepdims=True)

    for h in range(H_B):
        br = b_rows[h:h + 1, 0:L]
        igr = ig_rows[h:h + 1, 0:L]
        bc = to_col(br)
        igc = to_col(igr)
        mprev = m_s[h][:, 0:1]
        logd = jnp.where(causal, bc + (igr - br), NEG)
        li = bc + mprev
        mt = jnp.maximum(li, jnp.max(logd, axis=1, keepdims=True))
        d = jnp.exp(logd - mt)
        wi = jnp.exp(li - mt)
        qh = act[:, h * DK_B:(h + 1) * DK_B]
        kh = act[:, WB_QK + h * DK_B:WB_QK + (h + 1) * DK_B] * (DK_B ** -0.5)
        vb = v_ref[:, h * DV_B:(h + 1) * DV_B].astype(BF16)
        qb = qh.astype(BF16)
        a = lax.dot_general(qb, kh.astype(BF16), (((1,), (1,)), ((), ())), preferred_element_type=F32) * d
        cmat = c_s[h]
        num = (jnp.dot(a.astype(BF16), vb, preferred_element_type=F32)
               + wi * jnp.dot(qb, cmat.astype(BF16), preferred_element_type=F32))
        nvec = n_s[h]
        den = jnp.sum(a, axis=1, keepdims=True) + wi * jnp.sum(qh * nvec, axis=1, keepdims=True)
        hh = num / jnp.maximum(jnp.abs(den), jnp.exp(-mt))
        hn = hh * lax.rsqrt(jnp.mean(hh * hh, axis=-1, keepdims=True) + EPS)
        vs = slice(h * DV_B, (h + 1) * DV_B)
        y_ref[:, vs] = hn * ng_ref[:, vs] * jax.nn.sigmoid(og_ref[:, vs])
        bl, ml, wprev = bc[L - 1:L, :], mt[L - 1:L, :], wi[L - 1:L, :]
        kwt = kh * jnp.exp(bl + igc - bc - ml)
        c_s[h] = wprev * cmat + lax.dot_general(kwt.astype(BF16), vb, (((0,), (0,)), ((), ())),
                                                preferred_element_type=F32)
        n_s[h] = wprev * nvec + jnp.sum(kwt, axis=0, keepdims=True)
        m_s[h] = jnp.broadcast_to(ml, (1, LANE))

    @pl.when(c == pl.num_programs(1) - 1)
    def _():
        cf_ref[0] = c_s[...]
        nf_ref[0] = n_s[...]
        mf_ref[0] = m_s[...]


def _mlstm(u, g_rows, hist, cw, cb, bif, ng, c0, n0, m0, *, nb, s, row0, L, out=None):
    t = u.shape[0]
    nc = s // L
    lp = g_rows.shape[2] // nc
    rb = row0 // L
    n0 = n0.reshape(nb, H_B, 1, DK_B)
    m0 = jnp.broadcast_to(m0.reshape(nb, H_B, 1, 1), (nb, H_B, 1, LANE))
    bif_t = jnp.broadcast_to(bif.astype(F32)[:, None], (2 * H_B, LANE))
    row_spec = lambda col: pl.BlockSpec((L, WB_V), lambda b, c: (rb + b * nc + c, col))
    full = lambda shape: pl.BlockSpec(shape, lambda b, c: (0,) * len(shape))
    state = lambda shape: pl.BlockSpec((1,) + shape, lambda b, c: (b,) + (0,) * len(shape))
    in_specs = [row_spec(COL_QKB // WB_V), row_spec(COL_VB // WB_V), row_spec(COL_OB // WB_V),
                pl.BlockSpec((None, 2 * H_B, lp), lambda b, c: (b, 0, c)),
                state((8, 2 * WB_QK)),
                full((CONV_B, 2 * WB_QK)), full((1, 2 * WB_QK)), full((2 * H_B, LANE)), full((1, WB_V)),
                state((H_B, DK_B, DV_B)), state((H_B, 1, DK_B)), state((H_B, 1, LANE))]
    args = [u, u, u, g_rows, hist, cw, cb.reshape(1, -1), bif_t, ng.reshape(1, -1), c0, n0, m0]
    n_real = len(args)
    aliases = {}
    body = functools.partial(_mlstm_kernel, L=L)
    if out is not None:
        in_specs.append(pl.BlockSpec(memory_space=pl.ANY))
        args.append(out)
        aliases = {n_real: 0}
        body = _drop_last_input(body, len(args))
    y, cf, nf, mf = pl.pallas_call(
        body,
        grid=(nb, nc),
        in_specs=in_specs,
        out_specs=[pl.BlockSpec((L, WB_V), lambda b, c: (rb + b * nc + c, 0)),
                   state((H_B, DK_B, DV_B)), state((H_B, 1, DK_B)), state((H_B, 1, LANE))],
        out_shape=[jax.ShapeDtypeStruct((t, WB_V), F32),
                   jax.ShapeDtypeStruct((nb, H_B, DK_B, DV_B), F32),
                   jax.ShapeDtypeStruct((nb, H_B, 1, DK_B), F32),
                   jax.ShapeDtypeStruct((nb, H_B, 1, LANE), F32)],
        scratch_shapes=[pltpu.VMEM((L + 8, 2 * WB_QK), F32),
                        pltpu.VMEM((H_B, DK_B, DV_B), F32),
                        pltpu.VMEM((H_B, 1, DK_B), F32),
                        pltpu.VMEM((H_B, 1, LANE), F32)],
        input_output_aliases=aliases,
        compiler_params=_cparams("parallel", "arbitrary"),
        name="mlstm_sample" if out is not None else "mlstm_prompt",
    )(*args)
    return y, cf, nf.reshape(nb, H_B, DK_B), mf[:, :, 0, 0]


def _merge_kernel(ya_ref, yb_ref, yc_ref, gt_ref, x_ref, g1_ref, wa_ref, wb_ref, wc_ref, wo_ref, o_ref):
    tm, d = x_ref.shape
    sg = jax.nn.sigmoid(gt_ref[...])
    m = (sg[:, 0:d] * jnp.dot(ya_ref[...].astype(BF16), wa_ref[...], preferred_element_type=F32)
         + sg[:, d:2 * d] * jnp.dot(yb_ref[...].astype(BF16), wb_ref[...], preferred_element_type=F32)
         + sg[:, 2 * d:3 * d] * jnp.dot(yc_ref[...].astype(BF16), wc_ref[...], preferred_element_type=F32))
    o = jnp.dot(m.astype(BF16), wo_ref[...], preferred_element_type=F32)
    o = (o.reshape(tm // CHUNK, CHUNK, d) * g1_ref[...]).reshape(tm, d)
    o_ref[...] = x_ref[...] + o


def _merge(ya, yb, yc, u, x, mod_g, wa, wb, wc, wo):
    t, d = x.shape
    tm = _row_tile(t)
    row = lambda w, col=0: pl.BlockSpec((tm, w), lambda i: (i, col))
    full = lambda a: _resident(a.shape)
    return pl.pallas_call(
        _merge_kernel,
        grid=(t // tm,),
        in_specs=[row(WA), row(WB_V), row(WC_Q), row(3 * d, COL_GATES // (3 * d)), row(d),
                  _mod_spec(tm, 2), full(wa), full(wb), full(wc), full(wo)],
        out_specs=row(d),
        out_shape=jax.ShapeDtypeStruct((t, d), F32),
        compiler_params=_cparams("parallel"),
        name="merge",
    )(ya, yb, yc, u, x, mod_g, wa, wb, wc, wo)


def _ffn_kernel(x_ref, sh_ref, sc_ref, g_ref, g2_ref, w1_ref, w3_ref, w2_ref, o_ref, h_scr):
    tm, d = x_ref.shape
    h_scr[...] = _ln_mod(x_ref[...], g_ref[...], sc_ref[...], sh_ref[...]).astype(BF16)
    acc = None
    for c, w in _col_chunks(w1_ref.shape[1]):
        a = jnp.dot(h_scr[...], w1_ref[:, c:c + w], preferred_element_type=F32)
        b = jnp.dot(h_scr[...], w3_ref[:, c:c + w], preferred_element_type=F32)
        tt = (a * jax.nn.sigmoid(a) * b).astype(BF16)
        p = jnp.dot(tt, w2_ref[c:c + w, :], preferred_element_type=F32)
        acc = p if acc is None else acc + p
    f = (acc.reshape(tm // CHUNK, CHUNK, d) * g2_ref[...]).reshape(tm, d)
    o_ref[...] = x_ref[...] + f


def _ffn(x, mod_g, g, w1, w3, w2):
    t, d = x.shape
    tm = _row_tile(t)
    ff = w1.shape[1]
    return pl.pallas_call(
        _ffn_kernel,
        grid=(t // tm,),
        in_specs=[pl.BlockSpec((tm, d), lambda i: (i, 0)),
                  _mod_spec(tm, 3), _mod_spec(tm, 4),
                  _resident((1, d)),
                  _mod_spec(tm, 5),
                  _resident((d, ff)), _resident((d, ff)), _resident((ff, d))],
        out_specs=pl.BlockSpec((tm, d), lambda i: (i, 0)),
        out_shape=jax.ShapeDtypeStruct((t, d), F32),
        scratch_shapes=[pltpu.VMEM((tm, d), BF16)],
        compiler_params=_cparams("parallel"),
        name="ffn_dense",
    )(x, mod_g, mod_g, g.reshape(1, d), mod_g, w1, w3, w2)


def _router_kernel(x_ref, sh_ref, sc_ref, g_ref, wr_ref, br_ref, h_ref, sel_ref):
    h = _ln_mod(x_ref[...], g_ref[...], sc_ref[...], sh_ref[...])
    h_ref[...] = h
    lg = jnp.dot(h.astype(BF16), wr_ref[...], preferred_element_type=F32) + br_ref[...]
    lane = lax.broadcasted_iota(jnp.int32, lg.shape, 1).astype(F32)
    m1 = jnp.max(lg, axis=1, keepdims=True)
    i1 = jnp.min(jnp.where(lg == m1, lane, float(LANE)), axis=1, keepdims=True)
    lg2 = jnp.where(lane == i1, 2.0 * NEG, lg)
    m2 = jnp.max(lg2, axis=1, keepdims=True)
    i2 = jnp.min(jnp.where(lg2 == m2, lane, float(LANE)), axis=1, keepdims=True)
    e2 = jnp.exp(m2 - m1)
    w1 = 1.0 / (1.0 + e2)
    sel_ref[...] = (jnp.where(lane == 0.0, i1, 0.0) + jnp.where(lane == 1.0, i2, 0.0)
                    + jnp.where(lane == 2.0, w1, 0.0) + jnp.where(lane == 3.0, e2 * w1, 0.0))


def _router(x, mod_g, g, w_r, b_r):
    t, d = x.shape
    tm = _row_tile(t)
    wr = jnp.zeros((d, LANE), BF16).at[:, :N_EXPERTS].set(w_r.astype(BF16))
    br = jnp.full((1, LANE), NEG, F32).at[0, :N_EXPERTS].set(b_r.astype(F32))
    return pl.pallas_call(
        _router_kernel,
        grid=(t // tm,),
        in_specs=[pl.BlockSpec((tm, d), lambda i: (i, 0)),
                  _mod_spec(tm, 3), _mod_spec(tm, 4),
                  pl.BlockSpec((1, d), lambda i: (0, 0)),
                  pl.BlockSpec((d, LANE), lambda i: (0, 0)),
                  pl.BlockSpec((1, LANE), lambda i: (0, 0))],
        out_specs=[pl.BlockSpec((tm, d), lambda i: (i, 0)), pl.BlockSpec((tm, LANE), lambda i: (i, 0))],
        out_shape=[jax.ShapeDtypeStruct((t, d), F32), jax.ShapeDtypeStruct((t, LANE), F32)],
        compiler_params=_cparams("parallel"),
        name="router",
    )(x, mod_g, mod_g, g.reshape(1, d), wr, br)


def _route(sel, tmr):
    t = sel.shape[0]
    i32 = jnp.int32
    experts = jnp.arange(N_EXPERTS, dtype=i32)[None, :]
    e = sel[:, 0:2].astype(i32).reshape(-1)
    oh = (e[:, None] == experts).astype(i32)
    csum = jnp.cumsum(oh, axis=0)
    cnt = csum[-1]
    rank = jnp.sum(csum * oh, axis=1) - 1
    ntile_e = (cnt + tmr - 1) // tmr
    tile_end = jnp.cumsum(ntile_e)
    tile_start = tile_end - ntile_e
    pos = jnp.sum(oh * (tile_start * tmr)[None, :], axis=1) + rank
    n_tiles = -(-2 * t // tmr) + N_EXPERTS
    n_used = tile_end[-1]
    tiles = jnp.arange(n_tiles, dtype=i32)
    tc = jnp.minimum(tiles, n_used - 1)
    te = jnp.sum((tc[:, None] >= tile_end[None, :]).astype(i32), axis=1)
    ohe = (te[:, None] == experts).astype(i32)
    nvalid = jnp.sum(ohe * cnt[None, :], axis=1) - (tc - jnp.sum(ohe * tile_start[None, :], axis=1)) * tmr
    nvalid = jnp.where(tiles < n_used, jnp.clip(nvalid, 0, tmr), 0)
    return pos.astype(i32), te.astype(i32), nvalid.astype(i32), n_used.reshape(1).astype(i32)


EXPERT_TILE = 1024
EXPERT_SUB = 256
ROUTE_TILE = 256


def _row_copies(n_rows, make_copy):
    def issue(r, carry):
        for k in range(2):
            make_copy(r, k).start()
        return carry

    def drain(r, carry):
        for k in range(2):
            make_copy(r, k).wait()
        return carry

    lax.fori_loop(0, n_rows, issue, 0, unroll=8)
    lax.fori_loop(0, n_rows, drain, 0, unroll=8)


def _dispatch_kernel(pos_ref, h_ref, xs_in_ref, xs_ref, sem):
    del xs_in_ref

    def make_copy(r, k):
        p = pos_ref[0, 0, 2 * r + k]
        return pltpu.make_async_copy(h_ref.at[pl.ds(r, 1), :], xs_ref.at[pl.ds(p, 1), :], sem)

    _row_copies(h_ref.shape[0], make_copy)


def _dispatch(h, pos, n_rows):
    t, d = h.shape
    tm = min(ROUTE_TILE, _row_tile(t))
    xs0 = jnp.zeros((n_rows, d), F32)
    return pl.pallas_call(
        _dispatch_kernel,
        grid=(t // tm,),
        in_specs=[pl.BlockSpec((1, 1, 2 * tm), lambda i: (i, 0, 0), memory_space=pltpu.SMEM),
                  pl.BlockSpec((tm, d), lambda i: (i, 0)),
                  pl.BlockSpec(memory_space=pl.ANY)],
        out_specs=pl.BlockSpec(memory_space=pl.ANY),
        out_shape=jax.ShapeDtypeStruct((n_rows, d), F32),
        scratch_shapes=[pltpu.SemaphoreType.DMA(())],
        input_output_aliases={2: 0},
        compiler_params=_cparams("arbitrary"),
        name="moe_dispatch",
    )(pos.reshape(t // tm, 1, 2 * tm), h, xs0)


def _experts_kernel(te_ref, nv_ref, nu_ref, x_ref, w1_ref, w3_ref, w2_ref, y_ref, xb, w1b, w3b, w2b, acc):
    del te_ref, nu_ref
    i, j = pl.program_id(0), pl.program_id(1)
    nv = nv_ref[i]

    @pl.when(nv > 0)
    def _():
        @pl.when(j == 0)
        def _():
            xb[...] = x_ref[...].astype(BF16)
            acc[...] = jnp.zeros_like(acc)

        w1b[...] = w1_ref[0].astype(BF16)
        w3b[...] = w3_ref[0].astype(BF16)
        w2b[...] = w2_ref[0].astype(BF16)
        for sb in range(EXPERT_TILE // EXPERT_SUB):
            @pl.when(sb * EXPERT_SUB < nv)
            def _(rows=slice(sb * EXPERT_SUB, (sb + 1) * EXPERT_SUB)):
                xs = xb[rows, :]
                a = jnp.dot(xs, w1b[...], preferred_element_type=F32)
                b = jnp.dot(xs, w3b[...], preferred_element_type=F32)
                tt = (a * jax.nn.sigmoid(a) * b).astype(BF16)
                acc[rows, :] += jnp.dot(tt, w2b[...], preferred_element_type=F32)

        @pl.when(j == pl.num_programs(1) - 1)
        def _():
            y_ref[...] = acc[...]


def _experts(xs, te, nvalid, n_used, w1, w3, w2):
    n_rows, d = xs.shape
    ne, _, ff = w1.shape
    tf = 512
    nj = ff // tf
    n_tiles = n_rows // EXPERT_TILE

    def row_map(i, j, te_ref, nv_ref, nu_ref):
        return (jnp.minimum(i, nu_ref[0] - 1), 0)

    def col(i, j, nu_ref):
        return jnp.where(i < nu_ref[0], j, nj - 1)

    grid_spec = pltpu.PrefetchScalarGridSpec(
        num_scalar_prefetch=3,
        grid=(n_tiles, nj),
        in_specs=[pl.BlockSpec((EXPERT_TILE, d), row_map),
                  pl.BlockSpec((1, d, tf), lambda i, j, te_ref, nv_ref, nu_ref: (te_ref[i], 0, col(i, j, nu_ref))),
                  pl.BlockSpec((1, d, tf), lambda i, j, te_ref, nv_ref, nu_ref: (te_ref[i], 0, col(i, j, nu_ref))),
                  pl.BlockSpec((1, tf, d), lambda i, j, te_ref, nv_ref, nu_ref: (te_ref[i], col(i, j, nu_ref), 0))],
        out_specs=pl.BlockSpec((EXPERT_TILE, d), row_map),
        scratch_shapes=[pltpu.VMEM((EXPERT_TILE, d), BF16), pltpu.VMEM((d, tf), BF16), pltpu.VMEM((d, tf), BF16),
                        pltpu.VMEM((tf, d), BF16), pltpu.VMEM((EXPERT_TILE, d), F32)])
    return pl.pallas_call(
        _experts_kernel,
        grid_spec=grid_spec,
        out_shape=jax.ShapeDtypeStruct((n_rows, d), F32),
        compiler_params=_cparams("arbitrary", "arbitrary"),
        name="moe_experts",
    )(te, nvalid, n_used, xs, w1, w3, w2)


def _combine_kernel(pos_ref, sel_ref, x_ref, g2_ref, y_ref, o_ref, rows, sem):
    tm, d = x_ref.shape

    def make_copy(r, k):
        p = pos_ref[0, 0, 2 * r + k]
        return pltpu.make_async_copy(y_ref.at[pl.ds(p, 1), :], rows.at[k, pl.ds(r, 1), :], sem)

    _row_copies(tm, make_copy)
    sel = sel_ref[...]
    f = sel[:, 2:3] * rows[0] + sel[:, 3:4] * rows[1]
    f = (f.reshape(tm // CHUNK, CHUNK, d) * g2_ref[...]).reshape(tm, d)
    o_ref[...] = x_ref[...] + f


def _combine(y, pos, sel, x, mod_g):
    t, d = x.shape
    tm = min(ROUTE_TILE, _row_tile(t))
    return pl.pallas_call(
        _combine_kernel,
        grid=(t // tm,),
        in_specs=[pl.BlockSpec((1, 1, 2 * tm), lambda i: (i, 0, 0), memory_space=pltpu.SMEM),
                  pl.BlockSpec((tm, LANE), lambda i: (i, 0)),
                  pl.BlockSpec((tm, d), lambda i: (i, 0)),
                  _mod_spec(tm, 5),
                  pl.BlockSpec(memory_space=pl.ANY)],
        out_specs=pl.BlockSpec((tm, d), lambda i: (i, 0)),
        out_shape=jax.ShapeDtypeStruct((t, d), F32),
        scratch_shapes=[pltpu.VMEM((2, tm, d), F32), pltpu.SemaphoreType.DMA(())],
        compiler_params=_cparams("arbitrary"),
        name="moe_combine",
    )(pos.reshape(t // tm, 1, 2 * tm), sel, x, mod_g, y)


def _moe(x, mod_g, g, w_r, b_r, w1, w3, w2):
    t = x.shape[0]
    h2, sel = _router(x, mod_g, g, w_r, b_r)
    pos, te, nvalid, n_used = _route(sel, EXPERT_TILE)
    n_rows = te.shape[0] * EXPERT_TILE
    xs = _dispatch(h2, pos, n_rows)
    y = _experts(xs, te, nvalid, n_used, w1, w3, w2)
    return _combine(y, pos, sel, x, mod_g)


def _final_kernel(x_ref, g_ref, o_ref):
    x = x_ref[...]
    o_ref[...] = x * lax.rsqrt(jnp.mean(x * x, axis=-1, keepdims=True) + EPS) * g_ref[...]


def _final_norm(x, g):
    t, d = x.shape
    tm = _row_tile(t)
    return pl.pallas_call(
        _final_kernel,
        grid=(t // tm,),
        in_specs=[pl.BlockSpec((tm, d), lambda i: (i, 0)), pl.BlockSpec((1, d), lambda i: (0, 0))],
        out_specs=pl.BlockSpec((tm, d), lambda i: (i, 0)),
        out_shape=jax.ShapeDtypeStruct((t, d), F32),
        compiler_params=_cparams("parallel"),
        name="final_norm",
    )(x, g.reshape(1, d))


def _split_w_in(w):
    o = np.cumsum([0, WA, WA, WA, 2 * WB_QK, WB_V, 2 * H_B, WB_V, WC_Q, WC_KV, WC_KV, 3 * D_MODEL])
    seg = lambda k: w[:, o[k]:o[k + 1]]
    qa, ka, va, qkb, vb, ifb, ob, qc, kc, vc, gates = (seg(k) for k in range(11))
    main = jnp.concatenate([qa, ka, va, qc, qkb, gates, vb, ob, kc, vc], axis=1).astype(BF16)
    return main, ifb.T.astype(BF16)


def kernel(x_prompt, x_sample, c_prompt, c_sample, cache_a_k, cache_a_v, state_b_C, state_b_n, state_b_m, state_b_conv, cache_c_k, cache_c_v, norm1_g, norm2_g, w_ada, b_ada, w_in, b_if_b, conv_w_b, conv_b_b, norm_b_g, rel_a, sink_c, w_br_a, w_br_b, w_br_c, w_o, w_ff1, w_ff3, w_ff2, w_router, b_router, w_e1, w_e3, w_e2, norm_f_g):
    nbp, sp, d = x_prompt.shape
    nbs, ss, _ = x_sample.shape
    tp, ts = nbp * sp, nbs * ss
    x = jnp.concatenate([x_prompt.reshape(tp, d), x_sample.reshape(ts, d)], axis=0)
    cond = jnp.concatenate([c_prompt, c_sample], axis=0)
    group_batch = np.concatenate([np.repeat(np.arange(nbp), sp // CHUNK), nbp + np.repeat(np.arange(nbs), ss // CHUNK)])

    lp_s = max(ss, LANE)
    l_b = 256 if sp % 256 == 0 else CHUNK
    keep_a, keep_c = min(PRE_A, sp), min(PRE_C, sp)
    zeros_state = (jnp.zeros((nbp, H_B, DK_B, DV_B), F32), jnp.zeros((nbp, H_B, DK_B), F32), jnp.zeros((nbp, H_B), F32))
    new = {k: [] for k in ("pak", "pav", "pbc", "pbn", "pbm", "pbx", "pck", "pcv",
                           "sak", "sav", "sbc", "sbn", "sbm", "sbx", "sck", "scv")}

    for l in range(DEPTH):
        mod = _ada(cond, w_ada[l], b_ada[l])
        mod_g = mod[group_batch].reshape(-1, 1, 6 * d)
        w_main, w_if_t = _split_w_in(w_in[l])
        u, g_t = _inproj(x, mod_g, norm1_g[l], w_main, w_if_t)

        ya = _attn_a(u, rel_a[l], nb=nbp, s=sp, row0=0, bq=2 * CHUNK)
        ya = _attn_a(u, rel_a[l], nb=nbs, s=ss, row0=tp, bq=ss,
                     prefix=(cache_a_k[l].reshape(nbs, PRE_A, WA), cache_a_v[l].reshape(nbs, PRE_A, WA)), out=ya)
        yc, kr = _attn_c(u, sink_c[l], nb=nbp, s=sp, row0=0, bq=2 * CHUNK, pos0=0)
        yc, kr = _attn_c(u, sink_c[l], nb=nbs, s=ss, row0=tp, bq=ss, pos0=PAST_LEN,
                         prefix=(cache_c_k[l].reshape(nbs, PRE_C, WC_KV), cache_c_v[l].reshape(nbs, PRE_C, WC_KV)),
                         out=yc, kr_out=kr)
        g_p = g_t[:, :tp].reshape(2 * H_B, nbp, sp).transpose(1, 0, 2)
        g_s = g_t[:, tp:].reshape(2 * H_B, nbs, ss).transpose(1, 0, 2)
        g_s = jnp.pad(g_s, ((0, 0), (0, 0), (0, lp_s - ss)))
        hist_p = jnp.zeros((nbp, 8, 2 * WB_QK), F32)
        hist_s = jnp.pad(state_b_conv[l], ((0, 0), (8 - (CONV_B - 1), 0), (0, 0)))
        bargs = (conv_w_b[l], conv_b_b[l], b_if_b[l], norm_b_g[l])
        yb, pbc, pbn, pbm = _mlstm(u, g_p, hist_p, *bargs, *zeros_state, nb=nbp, s=sp, row0=0, L=l_b)
        yb, sbc, sbn, sbm = _mlstm(u, g_s, hist_s, *bargs, state_b_C[l], state_b_n[l], state_b_m[l],
                                   nb=nbs, s=ss, row0=tp, L=ss, out=yb)

        x = _merge(ya, yb, yc, u, x, mod_g, w_br_a[l].astype(BF16), w_br_b[l].astype(BF16),
                   w_br_c[l].astype(BF16), w_o[l].astype(BF16))
        i = l // 2
        if l % 2 == 0:
            x = _ffn(x, mod_g, norm2_g[l], w_ff1[i].astype(BF16), w_ff3[i].astype(BF16), w_ff2[i].astype(BF16))
        else:
            x = _moe(x, mod_g, norm2_g[l], w_router[i], b_router[i], w_e1[i], w_e3[i], w_e2[i])

        def tail(a, nb, s, row0, keep, col, width):
            return jnp.stack([lax.slice(a, (row0 + b * s + s - keep, col), (row0 + (b + 1) * s, col + width))
                              for b in range(nb)])

        new["pak"].append(tail(u, nbp, sp, 0, keep_a, COL_KA, WA).reshape(nbp, keep_a, H_A, DH_A))
        new["pav"].append(tail(u, nbp, sp, 0, keep_a, COL_VA, WA).reshape(nbp, keep_a, H_A, DH_A))
        new["pbc"].append(pbc)
        new["pbn"].append(pbn)
        new["pbm"].append(pbm)
        new["pbx"].append(tail(u, nbp, sp, 0, CONV_B - 1, COL_QKB, 2 * WB_QK))
        new["pck"].append(tail(kr, nbp, sp, 0, keep_c, 0, WC_KV).reshape(nbp, keep_c, HKV_C, DH_C))
        new["pcv"].append(tail(u, nbp, sp, 0, keep_c, COL_VC, WC_KV).reshape(nbp, keep_c, HKV_C, DH_C))
        new["sak"].append(tail(u, nbs, ss, tp, ss, COL_KA, WA).reshape(nbs, ss, H_A, DH_A))
        new["sav"].append(tail(u, nbs, ss, tp, ss, COL_VA, WA).reshape(nbs, ss, H_A, DH_A))
        new["sbc"].append(sbc)
        new["sbn"].append(sbn)
        new["sbm"].append(sbm)
        new["sbx"].append(tail(u, nbs, ss, tp, CONV_B - 1, COL_QKB, 2 * WB_QK))
        new["sck"].append(tail(kr, nbs, ss, tp, ss, 0, WC_KV).reshape(nbs, ss, HKV_C, DH_C))
        new["scv"].append(tail(u, nbs, ss, tp, ss, COL_VC, WC_KV).reshape(nbs, ss, HKV_C, DH_C))

    y = _final_norm(x, norm_f_g)
    st = {k: jnp.stack(v) for k, v in new.items()}
    return (y[:tp].reshape(nbp, sp, d), y[tp:].reshape(nbs, ss, d),
            st["pak"], st["pav"], st["pbc"], st["pbn"], st["pbm"], st["pbx"], st["pck"], st["pcv"],
            st["sak"], st["sav"], st["sbc"], st["sbn"], st["sbm"], st["sbx"], st["sck"], st["scv"])
```

```python
import functools

import numpy as np
import jax
import jax.numpy as jnp
from jax import lax
from jax.experimental import pallas as pl
from jax.experimental.pallas import tpu as pltpu

F32 = jnp.float32
BF16 = jnp.bfloat16

D_MODEL = 1024
DEPTH = 2
PAST_LEN = 2048
CHUNK = 64
H_A, DH_A, BACK_A, REL_CLIP = 8, 64, 8, 128
H_B, DK_B, DV_B, CONV_B = 4, 128, 256, 4
HQ_C, HKV_C, DH_C, BACK_C = 8, 2, 64, 2
ROT_DIM = DH_C // 4
ROPE_THETA = 500000.0
D_FF = 2816
N_EXPERTS = 8
D_FF_E = 3584
EPS = 1e-6
NEG = -1e30

WA = H_A * DH_A
WB_QK = H_B * DK_B
WB_V = H_B * DV_B
WC_Q = HQ_C * DH_C
WC_KV = HKV_C * DH_C
PRE_A = BACK_A * CHUNK
PRE_C = BACK_C * CHUNK

COL_QA, COL_KA, COL_VA, COL_QC = 0, 512, 1024, 1536
COL_QKB = 2048
COL_GATES = 3072
COL_VB = 6144
COL_OB = 7168
COL_KC = 8192
COL_VC = 8320
N_MAIN = 8448

LANE = 128
VMEM_LIMIT = 48 * 1024 * 1024


def _cparams(*sem):
    return pltpu.CompilerParams(dimension_semantics=sem, vmem_limit_bytes=VMEM_LIMIT)


def _row_tile(t):
    for tm in (512, 256, 128, 64):
        if t % tm == 0:
            return tm
    raise ValueError(f"token count {t} is not a multiple of 64")


def _ln_mod(x, g, sc, sh):
    tm, d = x.shape
    y = x * lax.rsqrt(jnp.mean(x * x, axis=-1, keepdims=True) + EPS) * g
    y = y.reshape(tm // CHUNK, CHUNK, d) * (1.0 + sc) + sh
    return y.reshape(tm, d)


def _mod_spec(tm, kind):
    return pl.BlockSpec((tm // CHUNK, 1, D_MODEL), lambda i, *_, k=kind: (i, 0, k))


def _ada_kernel(c_ref, w_ref, b_ref, o_ref):
    c = c_ref[...]
    a = (c * jax.nn.sigmoid(c)).astype(BF16)
    o_ref[...] = jnp.dot(a, w_ref[...].astype(BF16), preferred_element_type=F32) + b_ref[...]


def _ada(c, w, b):
    nb, d = c.shape
    n = w.shape[1]
    tn = 1536
    return pl.pallas_call(
        _ada_kernel,
        grid=(n // tn,),
        in_specs=[pl.BlockSpec((nb, d), lambda j: (0, 0)),
                  pl.BlockSpec((d, tn), lambda j: (0, j)),
                  pl.BlockSpec((1, tn), lambda j: (0, j))],
        out_specs=pl.BlockSpec((nb, tn), lambda j: (0, j)),
        out_shape=jax.ShapeDtypeStruct((nb, n), F32),
        compiler_params=_cparams("arbitrary"),
        name="adaln",
    )(c, w, b.reshape(1, n))


def _col_chunks(n, width=768):
    return [(c, min(width, n - c)) for c in range(0, n, width)]


def _resident(shape):
    return pl.BlockSpec(shape, lambda *_: (0,) * len(shape), pipeline_mode=pl.Buffered(1))


def _inproj_kernel(x_ref, sh_ref, sc_ref, g_ref, w_ref, wif_ref, u_ref, gt_ref, h_scr):
    h_scr[...] = _ln_mod(x_ref[...], g_ref[...], sc_ref[...], sh_ref[...]).astype(BF16)
    gt_ref[...] = lax.dot_general(wif_ref[...], h_scr[...], (((1,), (1,)), ((), ())), preferred_element_type=F32)
    for c, w in _col_chunks(N_MAIN):
        u_ref[:, c:c + w] = jnp.dot(h_scr[...], w_ref[:, c:c + w], preferred_element_type=F32)


def _inproj(x, mod_g, g, w_main, w_if_t):
    t, d = x.shape
    tm = min(256, _row_tile(t))
    return pl.pallas_call(
        _inproj_kernel,
        grid=(t // tm,),
        in_specs=[pl.BlockSpec((tm, d), lambda i: (i, 0)),
                  _mod_spec(tm, 0), _mod_spec(tm, 1),
                  _resident((1, d)), _resident((d, N_MAIN)), _resident((2 * H_B, d))],
        out_specs=[pl.BlockSpec((tm, N_MAIN), lambda i: (i, 0)),
                   pl.BlockSpec((2 * H_B, tm), lambda i: (0, i))],
        out_shape=[jax.ShapeDtypeStruct((t, N_MAIN), F32),
                   jax.ShapeDtypeStruct((2 * H_B, t), F32)],
        scratch_shapes=[pltpu.VMEM((tm, d), BF16)],
        compiler_params=_cparams("parallel"),
        name="inproj",
    )(x, mod_g, mod_g, g.reshape(1, d), w_main, w_if_t)


def _band_table(bq, pre, back):
    qi = np.arange(bq)[:, None] // CHUNK
    kj = np.arange(pre + bq)[None, :] // CHUNK
    return (kj >= qi) & (kj <= qi + back)


QBLK = 128
KV_CHUNK = 512


def _pad_rows(x, rows):
    return x if x.shape[0] == rows else jnp.concatenate(
        [x, jnp.zeros((rows - x.shape[0], x.shape[1]), x.dtype)], axis=0)


def _stage_keys_values(k_rows, v_ref, prefix_k, prefix_v, kpad, vt, pre, s):
    w = kpad.shape[1]
    s_pad = kpad.shape[0] - pre
    if prefix_k is None:
        kpad[0:pre, :] = jnp.zeros((pre, w), BF16)
        vt[:, 0:pre] = jnp.zeros((w, pre), BF16)
    else:
        kpad[0:pre, :] = prefix_k.astype(BF16)
        vt[:, 0:pre] = prefix_v.T.astype(BF16)
    for c in range(0, s_pad, KV_CHUNK):
        rows = min(KV_CHUNK, s_pad - c)
        real = max(0, min(rows, s - c))
        kpad[pre + c:pre + c + rows, :] = _pad_rows(k_rows(c, real), rows).astype(BF16)
        vt[:, pre + c:pre + c + rows] = _pad_rows(v_ref[c:c + real, :], rows).T.astype(BF16)


def _pair_store(o_ref, ots, lane0, rows):
    pair = jnp.concatenate(ots, axis=0).T
    o_ref[:, lane0:lane0 + pair.shape[1]] = pair[0:rows, :]


def _attn_a_kernel(*refs, s, has_prefix):
    if has_prefix:
        q_ref, k_ref, v_ref, pk_ref, pv_ref, bias_ref, o_ref, kt_ref, vt_ref, kpad, vt = refs
    else:
        q_ref, k_ref, v_ref, bias_ref, o_ref, kt_ref, vt_ref, kpad, vt = refs
    n = pl.program_id(1)
    kw = PRE_A + QBLK
    keep = kt_ref.shape[1]
    sq = q_ref.shape[0]

    @pl.when(n == 0)
    def _():
        _stage_keys_values(lambda c, rows: k_ref[c:c + rows, :], v_ref,
                           pk_ref[0] if has_prefix else None, pv_ref[0] if has_prefix else None,
                           kpad, vt, PRE_A, s)
        kt_ref[0] = k_ref[s - keep:s, :]
        vt_ref[0] = v_ref[s - keep:s, :]

    r0 = pl.multiple_of(n * QBLK, QBLK)
    q = _pad_rows((q_ref[...] * (DH_A ** -0.5)).astype(BF16), QBLK)
    heads = [slice(h * DH_A, (h + 1) * DH_A) for h in range(H_A)]
    nt = (((1,), (1,)), ((), ()))

    def attend(mask_keys):
        scores = [lax.dot_general(kpad[pl.ds(r0, kw), hs], q[:, hs], nt, preferred_element_type=F32)
                  for hs in heads]
        if mask_keys:
            key = lax.broadcasted_iota(jnp.int32, (kw, QBLK), 0)
            invalid = jnp.where(key >= PRE_A - r0, 0.0, NEG).astype(F32)
        probs, sums = [], []
        for h in range(H_A):
            sc = scores[h] + bias_ref[h]
            if mask_keys:
                sc = sc + invalid
            e = jnp.exp(sc - jnp.max(sc, axis=0, keepdims=True))
            sums.append(jnp.sum(e, axis=0, keepdims=True))
            probs.append(e.astype(BF16))
        for h in range(0, H_A, 2):
            ots = [jnp.dot(vt[heads[i], pl.ds(r0, kw)], probs[i], preferred_element_type=F32) / sums[i]
                   for i in (h, h + 1)]
            _pair_store(o_ref, ots, h * DH_A, sq)

    if has_prefix:
        attend(False)
    else:
        pl.when(r0 < PRE_A)(lambda: attend(True))
        pl.when(r0 >= PRE_A)(lambda: attend(False))


def _rel_bias_table(rel):
    kw = PRE_A + QBLK
    n_ext = QBLK - 1 + kw
    m = np.arange(n_ext)
    ext = rel.astype(F32)[np.clip(PRE_A + QBLK - 1 - m, -REL_CLIP, REL_CLIP) + REL_CLIP].T
    skew = jnp.tile(ext, (1, QBLK + 1))[:, :QBLK * (n_ext + 1)].reshape(H_A, QBLK, n_ext + 1)
    bias = jnp.where(_band_table(QBLK, PRE_A, BACK_A)[None], skew[:, ::-1, :kw], NEG)
    return jnp.swapaxes(bias, 1, 2)


def _attn_a(u, bias_t, *, nb, s, row0, prefix=None, out=None):
    t = u.shape[0]
    bq = min(QBLK, s)
    nq = s // bq
    s_pad = max(s, QBLK)
    kw = PRE_A + QBLK
    keep = min(PRE_A, s)
    rb, sb = row0 // bq, row0 // s
    in_specs = [pl.BlockSpec((bq, WA), lambda b, n: (rb + b * nq + n, COL_QA // WA)),
                pl.BlockSpec((s, WA), lambda b, n: (sb + b, COL_KA // WA)),
                pl.BlockSpec((s, WA), lambda b, n: (sb + b, COL_VA // WA))]
    args = [u, u, u]
    if prefix is not None:
        in_specs += [pl.BlockSpec((1, PRE_A, WA), lambda b, n: (b, 0, 0))] * 2
        args += [prefix[0], prefix[1]]
    in_specs.append(pl.BlockSpec((H_A, kw, QBLK), lambda b, n: (0, 0, 0)))
    args.append(bias_t)
    aliases = {}
    if out is not None:
        in_specs.append(pl.BlockSpec(memory_space=pl.ANY))
        args.append(out)
        aliases = {len(args) - 1: 0}
    body = functools.partial(_attn_a_kernel, s=s, has_prefix=prefix is not None)
    if out is not None:
        body = _drop_last_input(body, len(args))
    return pl.pallas_call(
        body,
        grid=(nb, nq),
        in_specs=in_specs,
        out_specs=[pl.BlockSpec((bq, WA), lambda b, n: (rb + b * nq + n, 0)),
                   pl.BlockSpec((1, keep, WA), lambda b, n: (b, 0, 0)),
                   pl.BlockSpec((1, keep, WA), lambda b, n: (b, 0, 0))],
        out_shape=[jax.ShapeDtypeStruct((t, WA), F32),
                   jax.ShapeDtypeStruct((nb, keep, WA), F32),
                   jax.ShapeDtypeStruct((nb, keep, WA), F32)],
        scratch_shapes=[pltpu.VMEM((PRE_A + s_pad, WA), BF16), pltpu.VMEM((WA, PRE_A + s_pad), BF16)],
        input_output_aliases=aliases,
        compiler_params=_cparams("parallel", "arbitrary"),
        name="mixer_a_sample" if prefix is not None else "mixer_a_prompt",
    )(*args)


def _drop_last_input(body, n_in):
    def wrapped(*refs):
        return body(*refs[:n_in - 1], *refs[n_in:])
    return wrapped


def _rope(x, c, s1, s2):
    w = x.shape[1]
    return x * c + pltpu.roll(x, 8, 1) * s1 + pltpu.roll(x, w - 8, 1) * s2


def _attn_c_kernel(*refs, s, has_prefix):
    if has_prefix:
        (q_ref, k_ref, v_ref, pk_ref, pv_ref, rc_ref, rs1_ref, rs2_ref, band_ref, sink_ref,
         o_ref, kt_ref, vt_ref, kpad, vt) = refs
    else:
        (q_ref, k_ref, v_ref, rc_ref, rs1_ref, rs2_ref, band_ref, sink_ref,
         o_ref, kt_ref, vt_ref, kpad, vt) = refs
    n = pl.program_id(1)
    kw = PRE_C + QBLK
    keep = kt_ref.shape[1]
    sq = q_ref.shape[0]

    def rope_at(x, start, rep=1):
        tables = [jnp.concatenate([t_ref[pl.ds(start, x.shape[0]), :]] * rep, axis=1)
                  for t_ref in (rc_ref, rs1_ref, rs2_ref)]
        return _rope(x, *tables)

    @pl.when(n == 0)
    def _():
        _stage_keys_values(lambda c, rows: rope_at(k_ref[c:c + rows, :], c), v_ref,
                           pk_ref[0] if has_prefix else None, pv_ref[0] if has_prefix else None,
                           kpad, vt, PRE_C, s)
        kt_ref[0] = rope_at(k_ref[s - keep:s, :], s - keep)
        vt_ref[0] = v_ref[s - keep:s, :]

    r0 = pl.multiple_of(n * QBLK, QBLK)
    q = rope_at(q_ref[...], pl.multiple_of(n * sq, sq), rep=WC_Q // WC_KV)
    q = _pad_rows((q * (DH_C ** -0.5)).astype(BF16), QBLK)
    per = HQ_C // HKV_C
    groups = [slice(g * DH_C, (g + 1) * DH_C) for g in range(HKV_C)]
    nt = (((1,), (1,)), ((), ()))

    def attend(mask_keys):
        scores = []
        for g, gs in enumerate(groups):
            qs = jnp.concatenate([q[:, (g * per + i) * DH_C:(g * per + i + 1) * DH_C] for i in range(per)], axis=0)
            scores.append(lax.dot_general(kpad[pl.ds(r0, kw), gs], qs, nt, preferred_element_type=F32))
        mask = band_ref[...]
        if mask_keys:
            key = lax.broadcasted_iota(jnp.int32, mask.shape, 0)
            mask = mask + jnp.where(key >= PRE_C - r0, 0.0, NEG).astype(F32)
        probs, sums = [], []
        for g in range(HKV_C):
            sk = jnp.concatenate([jnp.broadcast_to(sink_ref[g * per + i:g * per + i + 1, 0:1], (1, QBLK))
                                  for i in range(per)], axis=1)
            sc = scores[g] + mask
            m = jnp.maximum(jnp.max(sc, axis=0, keepdims=True), sk)
            e = jnp.exp(sc - m)
            sums.append(jnp.sum(e, axis=0, keepdims=True) + jnp.exp(sk - m))
            probs.append(e.astype(BF16))
        for g, gs in enumerate(groups):
            ot = jnp.dot(vt[gs, pl.ds(r0, kw)], probs[g], preferred_element_type=F32) / sums[g]
            for i in range(0, per, 2):
                _pair_store(o_ref, [ot[:, (i + j) * QBLK:(i + j + 1) * QBLK] for j in (0, 1)],
                            (g * per + i) * DH_C, sq)

    if has_prefix:
        attend(False)
    else:
        pl.when(r0 < PRE_C)(lambda: attend(True))
        pl.when(r0 >= PRE_C)(lambda: attend(False))


def _rope_tables(pos):
    half = ROT_DIM // 2
    inv_freq = 1.0 / (ROPE_THETA ** (jnp.arange(half, dtype=F32) * (2.0 / ROT_DIM)))
    ang = pos.astype(F32)[:, None] * inv_freq[None, :]
    cos, sin = jnp.cos(ang), jnp.sin(ang)
    n = pos.shape[0]
    one = jnp.ones((n, DH_C - ROT_DIM), F32)
    zero = jnp.zeros((n, DH_C - ROT_DIM), F32)
    zh = jnp.zeros((n, half), F32)
    c = jnp.concatenate([cos, cos, one], axis=1)
    s1 = jnp.concatenate([zh, sin, zero], axis=1)
    s2 = jnp.concatenate([-sin, zh, zero], axis=1)
    rep = WC_KV // DH_C
    return tuple(jnp.concatenate([a] * rep, axis=1) for a in (c, s1, s2))


def _attn_c(u, sink, *, nb, s, row0, pos0, prefix=None, out=None):
    t = u.shape[0]
    bq = min(QBLK, s)
    nq = s // bq
    s_pad = max(s, QBLK)
    kw = PRE_C + QBLK
    per = HQ_C // HKV_C
    tables = _rope_tables(pos0 + jnp.arange(s))
    band = np.where(_band_table(QBLK, PRE_C, BACK_C), 0.0, NEG).astype(np.float32)
    band = jnp.asarray(np.tile(band.T, (1, per)))
    sink_t = jnp.broadcast_to(sink.astype(F32)[:, None], (HQ_C, LANE))
    rb, sb = row0 // bq, row0 // s
    in_specs = [pl.BlockSpec((bq, WC_Q), lambda b, n: (rb + b * nq + n, COL_QC // WC_Q)),
                pl.BlockSpec((s, WC_KV), lambda b, n: (sb + b, COL_KC // WC_KV)),
                pl.BlockSpec((s, WC_KV), lambda b, n: (sb + b, COL_VC // WC_KV))]
    args = [u, u, u]
    if prefix is not None:
        in_specs += [pl.BlockSpec((1, PRE_C, WC_KV), lambda b, n: (b, 0, 0))] * 2
        args += [prefix[0], prefix[1]]
    in_specs += [pl.BlockSpec((s, WC_KV), lambda b, n: (0, 0))] * 3
    args += list(tables)
    in_specs += [pl.BlockSpec((kw, per * QBLK), lambda b, n: (0, 0)),
                 pl.BlockSpec((HQ_C, LANE), lambda b, n: (0, 0))]
    args += [band, sink_t]
    aliases = {}
    body = functools.partial(_attn_c_kernel, s=s, has_prefix=prefix is not None)
    if out is not None:
        in_specs.append(pl.BlockSpec(memory_space=pl.ANY))
        args.append(out)
        aliases = {len(args) - 1: 0}
        body = _drop_last_input(body, len(args))
    keep = min(PRE_C, s)
    return pl.pallas_call(
        body,
        grid=(nb, nq),
        in_specs=in_specs,
        out_specs=[pl.BlockSpec((bq, WC_Q), lambda b, n: (rb + b * nq + n, 0)),
                   pl.BlockSpec((1, keep, WC_KV), lambda b, n: (b, 0, 0)),
                   pl.BlockSpec((1, keep, WC_KV), lambda b, n: (b, 0, 0))],
        out_shape=[jax.ShapeDtypeStruct((t, WC_Q), F32),
                   jax.ShapeDtypeStruct((nb, keep, WC_KV), F32),
                   jax.ShapeDtypeStruct((nb, keep, WC_KV), F32)],
        scratch_shapes=[pltpu.VMEM((PRE_C + s_pad, WC_KV), BF16), pltpu.VMEM((WC_KV, PRE_C + s_pad), BF16)],
        input_output_aliases=aliases,
        compiler_params=_cparams("parallel", "arbitrary"),
        name="mixer_c_sample" if prefix is not None else "mixer_c_prompt",
    )(*args)


def _log_sigmoid(x):
    return jnp.minimum(x, 0.0) - jnp.log(1.0 + jnp.exp(-jnp.abs(x)))


def _lane_cumsum(x):
    n = x.shape[1]
    col = lax.broadcasted_iota(jnp.int32, x.shape, 1)
    k = 1
    while k < n:
        x = x + jnp.where(col >= k, pltpu.roll(x, k, 1), 0.0)
        k *= 2
    return x


def _mlstm_kernel(qk_ref, v_ref, og_ref, g_ref, hist_ref, cw_ref, cb_ref, bif_ref, ng_ref,
                  c0_ref, n0_ref, m0_ref, y_ref, cf_ref, nf_ref, mf_ref, xt_ref,
                  xpad, c_s, n_s, m_s, *, L):
    c = pl.program_id(1)

    @pl.when(c == 0)
    def _():
        c_s[...] = c0_ref[0]
        n_s[...] = n0_ref[0]
        m_s[...] = m0_ref[0]
        xpad[0:8, :] = hist_ref[0]

    x = qk_ref[...]
    xpad[8:8 + L, :] = x
    conv = (cb_ref[...] + x * cw_ref[3:4, :] + xpad[7:7 + L, :] * cw_ref[2:3, :]
            + xpad[6:6 + L, :] * cw_ref[1:2, :] + xpad[5:5 + L, :] * cw_ref[0:1, :])
    xpad[0:8, :] = xpad[L:L + 8, :]
    act = conv * jax.nn.sigmoid(conv)

    gb = g_ref[...] + bif_ref[:, 0:1]
    ig_rows = gb[0:H_B, :]
    b_rows = _lane_cumsum(_log_sigmoid(gb))[H_B:2 * H_B, :]

    row = lax.broadcasted_iota(jnp.int32, (L, L), 0)
    colm = lax.broadcasted_iota(jnp.int32, (L, L), 1)
    eye = row == colm
    causal = colm <= row

    def to_col(r):
        return jnp.sum(jnp.where(eye, r, 0.0), axis=1, keepdims=True)

    heads = range(H_B)
    nt = (((1,), (1,)), ((), ()))
    qf = [act[:, h * DK_B:(h + 1) * DK_B] for h in heads]
    kf = [act[:, WB_QK + h * DK_B:WB_QK + (h + 1) * DK_B] * (DK_B ** -0.5) for h in heads]
    qb = [q.astype(BF16) for q in qf]
    vb = [v_ref[:, h * DV_B:(h + 1) * DV_B].astype(BF16) for h in heads]
    cmat = [c_s[h] for h in heads]
    nvec = [n_s[h] for h in heads]
    qk = [lax.dot_general(qb[h], kf[h].astype(BF16), nt, preferred_element_type=F32) for h in heads]
    qc = [jnp.dot(qb[h], cmat[h].astype(BF16), preferred_element_type=F32) for h in heads]

    a, wi, mt, bc, igc = [], [], [], [], []
    for h in heads:
        br = b_rows[h:h + 1, 0:L]
        igr = ig_rows[h:h + 1, 0:L]
        bc.append(to_col(br))
        igc.append(to_col(igr))
        logd = jnp.where(causal, bc[h] + (igr - br), NEG)
        li = bc[h] + m_s[h][:, 0:1]
        mt.append(jnp.maximum(li, jnp.max(logd, axis=1, keepdims=True)))
        wi.append(jnp.exp(li - mt[h]))
        a.append(qk[h] * jnp.exp(logd - mt[h]))

    pv = [jnp.dot(a[h].astype(BF16), vb[h], preferred_element_type=F32) for h in heads]
    for h in heads:
        num = pv[h] + wi[h] * qc[h]
        den = jnp.sum(a[h], axis=1, keepdims=True) + wi[h] * jnp.sum(qf[h] * nvec[h], axis=1, keepdims=True)
        hh = num / jnp.maximum(jnp.abs(den), jnp.exp(-mt[h]))
        hn = hh * lax.rsqrt(jnp.mean(hh * hh, axis=-1, keepdims=True) + EPS)
        vs = slice(h * DV_B, (h + 1) * DV_B)
        y_ref[:, vs] = hn * ng_ref[:, vs] * jax.nn.sigmoid(og_ref[:, vs])

    for h in heads:
        bl, ml, wprev = bc[h][L - 1:L, :], mt[h][L - 1:L, :], wi[h][L - 1:L, :]
        kwt = kf[h] * jnp.exp(bl + igc[h] - bc[h] - ml)
        c_s[h] = wprev * cmat[h] + lax.dot_general(kwt.astype(BF16), vb[h], (((0,), (0,)), ((), ())),
                                                   preferred_element_type=F32)
        n_s[h] = wprev * nvec[h] + jnp.sum(kwt, axis=0, keepdims=True)
        m_s[h] = jnp.broadcast_to(ml, (1, LANE))

    @pl.when(c == pl.num_programs(1) - 1)
    def _():
        cf_ref[0] = c_s[...]
        nf_ref[0] = n_s[...]
        mf_ref[0] = m_s[...]
        xt_ref[0] = xpad[0:8, :]


def _mlstm(u, g_rows, hist, cw, cb, bif, ng, c0, n0, m0, *, nb, s, row0, L, out=None):
    t = u.shape[0]
    nc = s // L
    lp = g_rows.shape[2] // nc
    rb = row0 // L
    n0 = n0.reshape(nb, H_B, 1, DK_B)
    m0 = jnp.broadcast_to(m0.reshape(nb, H_B, 1, 1), (nb, H_B, 1, LANE))
    bif_t = jnp.broadcast_to(bif.astype(F32)[:, None], (2 * H_B, LANE))
    row_spec = lambda col: pl.BlockSpec((L, WB_V), lambda b, c: (rb + b * nc + c, col))
    full = lambda shape: pl.BlockSpec(shape, lambda b, c: (0,) * len(shape))
    state = lambda shape: pl.BlockSpec((1,) + shape, lambda b, c: (b,) + (0,) * len(shape))
    in_specs = [row_spec(COL_QKB // WB_V), row_spec(COL_VB // WB_V), row_spec(COL_OB // WB_V),
                pl.BlockSpec((None, 2 * H_B, lp), lambda b, c: (b, 0, c)),
                state((8, 2 * WB_QK)),
                full((CONV_B, 2 * WB_QK)), full((1, 2 * WB_QK)), full((2 * H_B, LANE)), full((1, WB_V)),
                state((H_B, DK_B, DV_B)), state((H_B, 1, DK_B)), state((H_B, 1, LANE))]
    args = [u, u, u, g_rows, hist, cw, cb.reshape(1, -1), bif_t, ng.reshape(1, -1), c0, n0, m0]
    n_real = len(args)
    aliases = {}
    body = functools.partial(_mlstm_kernel, L=L)
    if out is not None:
        in_specs.append(pl.BlockSpec(memory_space=pl.ANY))
        args.append(out)
        aliases = {n_real: 0}
        body = _drop_last_input(body, len(args))
    y, cf, nf, mf, xt = pl.pallas_call(
        body,
        grid=(nb, nc),
        in_specs=in_specs,
        out_specs=[pl.BlockSpec((L, WB_V), lambda b, c: (rb + b * nc + c, 0)),
                   state((H_B, DK_B, DV_B)), state((H_B, 1, DK_B)), state((H_B, 1, LANE)),
                   state((8, 2 * WB_QK))],
        out_shape=[jax.ShapeDtypeStruct((t, WB_V), F32),
                   jax.ShapeDtypeStruct((nb, H_B, DK_B, DV_B), F32),
                   jax.ShapeDtypeStruct((nb, H_B, 1, DK_B), F32),
                   jax.ShapeDtypeStruct((nb, H_B, 1, LANE), F32),
                   jax.ShapeDtypeStruct((nb, 8, 2 * WB_QK), F32)],
        scratch_shapes=[pltpu.VMEM((L + 8, 2 * WB_QK), F32),
                        pltpu.VMEM((H_B, DK_B, DV_B), F32),
                        pltpu.VMEM((H_B, 1, DK_B), F32),
                        pltpu.VMEM((H_B, 1, LANE), F32)],
        input_output_aliases=aliases,
        compiler_params=_cparams("parallel", "arbitrary"),
        name="mlstm_sample" if out is not None else "mlstm_prompt",
    )(*args)
    return y, cf, nf.reshape(nb, H_B, DK_B), mf[:, :, 0, 0], xt[:, 8 - (CONV_B - 1):, :]


def _merge_kernel(ya_ref, yb_ref, yc_ref, gt_ref, x_ref, g1_ref, wa_ref, wb_ref, wc_ref, wo_ref, o_ref):
    tm, d = x_ref.shape
    sg = jax.nn.sigmoid(gt_ref[...])
    m = (sg[:, 0:d] * jnp.dot(ya_ref[...].astype(BF16), wa_ref[...], preferred_element_type=F32)
         + sg[:, d:2 * d] * jnp.dot(yb_ref[...].astype(BF16), wb_ref[...], preferred_element_type=F32)
         + sg[:, 2 * d:3 * d] * jnp.dot(yc_ref[...].astype(BF16), wc_ref[...], preferred_element_type=F32))
    o = jnp.dot(m.astype(BF16), wo_ref[...], preferred_element_type=F32)
    o = (o.reshape(tm // CHUNK, CHUNK, d) * g1_ref[...]).reshape(tm, d)
    o_ref[...] = x_ref[...] + o


def _merge(ya, yb, yc, u, x, mod_g, wa, wb, wc, wo):
    t, d = x.shape
    tm = _row_tile(t)
    row = lambda w, col=0: pl.BlockSpec((tm, w), lambda i: (i, col))
    full = lambda a: _resident(a.shape)
    return pl.pallas_call(
        _merge_kernel,
        grid=(t // tm,),
        in_specs=[row(WA), row(WB_V), row(WC_Q), row(3 * d, COL_GATES // (3 * d)), row(d),
                  _mod_spec(tm, 2), full(wa), full(wb), full(wc), full(wo)],
        out_specs=row(d),
        out_shape=jax.ShapeDtypeStruct((t, d), F32),
        compiler_params=_cparams("parallel"),
        name="merge",
    )(ya, yb, yc, u, x, mod_g, wa, wb, wc, wo)


def _ffn_kernel(x_ref, sh_ref, sc_ref, g_ref, g2_ref, w1_ref, w3_ref, w2_ref, o_ref, h_scr):
    tm, d = x_ref.shape
    h_scr[...] = _ln_mod(x_ref[...], g_ref[...], sc_ref[...], sh_ref[...]).astype(BF16)
    acc = None
    for c, w in _col_chunks(w1_ref.shape[1]):
        a = jnp.dot(h_scr[...], w1_ref[:, c:c + w], preferred_element_type=F32)
        b = jnp.dot(h_scr[...], w3_ref[:, c:c + w], preferred_element_type=F32)
        tt = (a * jax.nn.sigmoid(a) * b).astype(BF16)
        p = jnp.dot(tt, w2_ref[c:c + w, :], preferred_element_type=F32)
        acc = p if acc is None else acc + p
    f = (acc.reshape(tm // CHUNK, CHUNK, d) * g2_ref[...]).reshape(tm, d)
    o_ref[...] = x_ref[...] + f


def _ffn(x, mod_g, g, w1, w3, w2):
    t, d = x.shape
    tm = _row_tile(t)
    ff = w1.shape[1]
    return pl.pallas_call(
        _ffn_kernel,
        grid=(t // tm,),
        in_specs=[pl.BlockSpec((tm, d), lambda i: (i, 0)),
                  _mod_spec(tm, 3), _mod_spec(tm, 4),
                  _resident((1, d)),
                  _mod_spec(tm, 5),
                  _resident((d, ff)), _resident((d, ff)), _resident((ff, d))],
        out_specs=pl.BlockSpec((tm, d), lambda i: (i, 0)),
        out_shape=jax.ShapeDtypeStruct((t, d), F32),
        scratch_shapes=[pltpu.VMEM((tm, d), BF16)],
        compiler_params=_cparams("parallel"),
        name="ffn_dense",
    )(x, mod_g, mod_g, g.reshape(1, d), mod_g, w1, w3, w2)


def _router_kernel(x_ref, sh_ref, sc_ref, g_ref, wr_ref, br_ref, h_ref, sel_ref):
    h = _ln_mod(x_ref[...], g_ref[...], sc_ref[...], sh_ref[...])
    h_ref[...] = h
    lg = jnp.dot(h.astype(BF16), wr_ref[...], preferred_element_type=F32) + br_ref[...]
    lane = lax.broadcasted_iota(jnp.int32, lg.shape, 1).astype(F32)
    m1 = jnp.max(lg, axis=1, keepdims=True)
    i1 = jnp.min(jnp.where(lg == m1, lane, float(LANE)), axis=1, keepdims=True)
    lg2 = jnp.where(lane == i1, 2.0 * NEG, lg)
    m2 = jnp.max(lg2, axis=1, keepdims=True)
    i2 = jnp.min(jnp.where(lg2 == m2, lane, float(LANE)), axis=1, keepdims=True)
    e2 = jnp.exp(m2 - m1)
    w1 = 1.0 / (1.0 + e2)
    sel_ref[...] = (jnp.where(lane == 0.0, i1, 0.0) + jnp.where(lane == 1.0, i2, 0.0)
                    + jnp.where(lane == 2.0, w1, 0.0) + jnp.where(lane == 3.0, e2 * w1, 0.0))


def _router(x, mod_g, g, w_r, b_r):
    t, d = x.shape
    tm = _row_tile(t)
    wr = jnp.zeros((d, LANE), BF16).at[:, :N_EXPERTS].set(w_r.astype(BF16))
    br = jnp.full((1, LANE), NEG, F32).at[0, :N_EXPERTS].set(b_r.astype(F32))
    return pl.pallas_call(
        _router_kernel,
        grid=(t // tm,),
        in_specs=[pl.BlockSpec((tm, d), lambda i: (i, 0)),
                  _mod_spec(tm, 3), _mod_spec(tm, 4),
                  pl.BlockSpec((1, d), lambda i: (0, 0)),
                  pl.BlockSpec((d, LANE), lambda i: (0, 0)),
                  pl.BlockSpec((1, LANE), lambda i: (0, 0))],
        out_specs=[pl.BlockSpec((tm, d), lambda i: (i, 0)), pl.BlockSpec((tm, LANE), lambda i: (i, 0))],
        out_shape=[jax.ShapeDtypeStruct((t, d), F32), jax.ShapeDtypeStruct((t, LANE), F32)],
        compiler_params=_cparams("parallel"),
        name="router",
    )(x, mod_g, mod_g, g.reshape(1, d), wr, br)


def _route(sel, tmr):
    t = sel.shape[0]
    i32 = jnp.int32
    experts = jnp.arange(N_EXPERTS, dtype=i32)[None, :]
    e = sel[:, 0:2].astype(i32).reshape(-1)
    oh = (e[:, None] == experts).astype(i32)
    csum = jnp.cumsum(oh, axis=0)
    cnt = csum[-1]
    rank = jnp.sum(csum * oh, axis=1) - 1
    ntile_e = (cnt + tmr - 1) // tmr
    tile_end = jnp.cumsum(ntile_e)
    tile_start = tile_end - ntile_e
    pos = jnp.sum(oh * (tile_start * tmr)[None, :], axis=1) + rank
    n_tiles = -(-2 * t // tmr) + N_EXPERTS
    n_used = tile_end[-1]
    tiles = jnp.arange(n_tiles, dtype=i32)
    tc = jnp.minimum(tiles, n_used - 1)
    te = jnp.sum((tc[:, None] >= tile_end[None, :]).astype(i32), axis=1)
    ohe = (te[:, None] == experts).astype(i32)
    nvalid = jnp.sum(ohe * cnt[None, :], axis=1) - (tc - jnp.sum(ohe * tile_start[None, :], axis=1)) * tmr
    nvalid = jnp.where(tiles < n_used, jnp.clip(nvalid, 0, tmr), 0)
    return pos.astype(i32), te.astype(i32), nvalid.astype(i32), n_used.reshape(1).astype(i32)


EXPERT_TILE = 1024
EXPERT_SUB = 256
ROUTE_TILE = 256


def _row_copies(n_rows, make_copy):
    def issue(r, carry):
        for k in range(2):
            make_copy(r, k).start(priority=k)
        return carry

    def drain(r, carry):
        for k in range(2):
            make_copy(r, k).wait()
        return carry

    lax.fori_loop(0, n_rows, issue, 0, unroll=8)
    lax.fori_loop(0, n_rows, drain, 0, unroll=8)


def _dispatch_kernel(pos_ref, h_ref, xs_in_ref, xs_ref, sem):
    del xs_in_ref

    def make_copy(r, k):
        p = pos_ref[0, 0, 2 * r + k]
        return pltpu.make_async_copy(h_ref.at[pl.ds(r, 1), :], xs_ref.at[pl.ds(p, 1), :], sem)

    _row_copies(h_ref.shape[0], make_copy)


def _dispatch(h, pos, n_rows):
    t, d = h.shape
    tm = min(ROUTE_TILE, _row_tile(t))
    xs0 = jnp.zeros((n_rows, d), F32)
    return pl.pallas_call(
        _dispatch_kernel,
        grid=(t // tm,),
        in_specs=[pl.BlockSpec((1, 1, 2 * tm), lambda i: (i, 0, 0), memory_space=pltpu.SMEM),
                  pl.BlockSpec((tm, d), lambda i: (i, 0)),
                  pl.BlockSpec(memory_space=pl.ANY)],
        out_specs=pl.BlockSpec(memory_space=pl.ANY),
        out_shape=jax.ShapeDtypeStruct((n_rows, d), F32),
        scratch_shapes=[pltpu.SemaphoreType.DMA(())],
        input_output_aliases={2: 0},
        compiler_params=_cparams("arbitrary"),
        name="moe_dispatch",
    )(pos.reshape(t // tm, 1, 2 * tm), h, xs0)


def _experts_kernel(te_ref, nv_ref, nu_ref, x_ref, w1_ref, w3_ref, w2_ref, y_ref, xb, w1b, w3b, w2b, acc):
    del te_ref, nu_ref
    i, j = pl.program_id(0), pl.program_id(1)
    nv = nv_ref[i]

    @pl.when(nv > 0)
    def _():
        @pl.when(j == 0)
        def _():
            xb[...] = x_ref[...].astype(BF16)
            acc[...] = jnp.zeros_like(acc)

        w1b[...] = w1_ref[0].astype(BF16)
        w3b[...] = w3_ref[0].astype(BF16)
        w2b[...] = w2_ref[0].astype(BF16)
        def block(rows):
            xs = xb[rows, :]
            a = jnp.dot(xs, w1b[...], preferred_element_type=F32)
            b = jnp.dot(xs, w3b[...], preferred_element_type=F32)
            tt = (a * jax.nn.sigmoid(a) * b).astype(BF16)
            acc[rows, :] += jnp.dot(tt, w2b[...], preferred_element_type=F32)

        @pl.when(nv == EXPERT_TILE)
        def _():
            block(slice(0, EXPERT_TILE))

        for sb in range(EXPERT_TILE // EXPERT_SUB):
            @pl.when((nv < EXPERT_TILE) & (sb * EXPERT_SUB < nv))
            def _(rows=slice(sb * EXPERT_SUB, (sb + 1) * EXPERT_SUB)):
                block(rows)

        @pl.when(j == pl.num_programs(1) - 1)
        def _():
            y_ref[...] = acc[...]


def _experts(xs, te, nvalid, n_used, w1, w3, w2):
    n_rows, d = xs.shape
    ne, _, ff = w1.shape
    tf = 512
    nj = ff // tf
    n_tiles = n_rows // EXPERT_TILE

    def row_map(i, j, te_ref, nv_ref, nu_ref):
        return (jnp.minimum(i, nu_ref[0] - 1), 0)

    def col(i, j, nu_ref):
        return jnp.where(i < nu_ref[0], j, nj - 1)

    grid_spec = pltpu.PrefetchScalarGridSpec(
        num_scalar_prefetch=3,
        grid=(n_tiles, nj),
        in_specs=[pl.BlockSpec((EXPERT_TILE, d), row_map),
                  pl.BlockSpec((1, d, tf), lambda i, j, te_ref, nv_ref, nu_ref: (te_ref[i], 0, col(i, j, nu_ref))),
                  pl.BlockSpec((1, d, tf), lambda i, j, te_ref, nv_ref, nu_ref: (te_ref[i], 0, col(i, j, nu_ref))),
                  pl.BlockSpec((1, tf, d), lambda i, j, te_ref, nv_ref, nu_ref: (te_ref[i], col(i, j, nu_ref), 0))],
        out_specs=pl.BlockSpec((EXPERT_TILE, d), row_map),
        scratch_shapes=[pltpu.VMEM((EXPERT_TILE, d), BF16), pltpu.VMEM((d, tf), BF16), pltpu.VMEM((d, tf), BF16),
                        pltpu.VMEM((tf, d), BF16), pltpu.VMEM((EXPERT_TILE, d), F32)])
    return pl.pallas_call(
        _experts_kernel,
        grid_spec=grid_spec,
        out_shape=jax.ShapeDtypeStruct((n_rows, d), F32),
        compiler_params=_cparams("arbitrary", "arbitrary"),
        name="moe_experts",
    )(te, nvalid, n_used, xs, w1, w3, w2)


def _combine_kernel(pos_ref, sel_ref, x_ref, g2_ref, y_ref, o_ref, rows, sem):
    tm, d = x_ref.shape

    def make_copy(r, k):
        p = pos_ref[0, 0, 2 * r + k]
        return pltpu.make_async_copy(y_ref.at[pl.ds(p, 1), :], rows.at[k, pl.ds(r, 1), :], sem)

    _row_copies(tm, make_copy)
    sel = sel_ref[...]
    f = sel[:, 2:3] * rows[0] + sel[:, 3:4] * rows[1]
    f = (f.reshape(tm // CHUNK, CHUNK, d) * g2_ref[...]).reshape(tm, d)
    o_ref[...] = x_ref[...] + f


def _combine(y, pos, sel, x, mod_g):
    t, d = x.shape
    tm = min(ROUTE_TILE, _row_tile(t))
    return pl.pallas_call(
        _combine_kernel,
        grid=(t // tm,),
        in_specs=[pl.BlockSpec((1, 1, 2 * tm), lambda i: (i, 0, 0), memory_space=pltpu.SMEM),
                  pl.BlockSpec((tm, LANE), lambda i: (i, 0)),
                  pl.BlockSpec((tm, d), lambda i: (i, 0)),
                  _mod_spec(tm, 5),
                  pl.BlockSpec(memory_space=pl.ANY)],
        out_specs=pl.BlockSpec((tm, d), lambda i: (i, 0)),
        out_shape=jax.ShapeDtypeStruct((t, d), F32),
        scratch_shapes=[pltpu.VMEM((2, tm, d), F32), pltpu.SemaphoreType.DMA(())],
        compiler_params=_cparams("arbitrary"),
        name="moe_combine",
    )(pos.reshape(t // tm, 1, 2 * tm), sel, x, mod_g, y)


def _moe(x, mod_g, g, w_r, b_r, w1, w3, w2):
    t = x.shape[0]
    h2, sel = _router(x, mod_g, g, w_r, b_r)
    pos, te, nvalid, n_used = _route(sel, EXPERT_TILE)
    n_rows = te.shape[0] * EXPERT_TILE
    xs = _dispatch(h2, pos, n_rows)
    y = _experts(xs, te, nvalid, n_used, w1, w3, w2)
    return _combine(y, pos, sel, x, mod_g)


def _final_kernel(x_ref, g_ref, op_ref, os_ref, *, n_prompt_tiles):
    x = x_ref[...]
    y = x * lax.rsqrt(jnp.mean(x * x, axis=-1, keepdims=True) + EPS) * g_ref[...]
    i = pl.program_id(0)

    @pl.when(i < n_prompt_tiles)
    def _():
        op_ref[...] = y

    @pl.when(i >= n_prompt_tiles)
    def _():
        os_ref[...] = y


def _final_norm(x, g, tp):
    t, d = x.shape
    tm = _row_tile(np.gcd(tp, t - tp))
    n_p = tp // tm
    return pl.pallas_call(
        functools.partial(_final_kernel, n_prompt_tiles=n_p),
        grid=(t // tm,),
        in_specs=[pl.BlockSpec((tm, d), lambda i: (i, 0)), pl.BlockSpec((1, d), lambda i: (0, 0))],
        out_specs=[pl.BlockSpec((tm, d), lambda i: (jnp.minimum(i, n_p - 1), 0)),
                   pl.BlockSpec((tm, d), lambda i: (jnp.maximum(i - n_p, 0), 0))],
        out_shape=[jax.ShapeDtypeStruct((tp, d), F32), jax.ShapeDtypeStruct((t - tp, d), F32)],
        compiler_params=_cparams("arbitrary"),
        name="final_norm",
    )(x, g.reshape(1, d))


def _split_w_in(w):
    o = np.cumsum([0, WA, WA, WA, 2 * WB_QK, WB_V, 2 * H_B, WB_V, WC_Q, WC_KV, WC_KV, 3 * D_MODEL])
    seg = lambda k: w[:, o[k]:o[k + 1]]
    qa, ka, va, qkb, vb, ifb, ob, qc, kc, vc, gates = (seg(k) for k in range(11))
    main = jnp.concatenate([qa, ka, va, qc, qkb, gates, vb, ob, kc, vc], axis=1).astype(BF16)
    return main, ifb.T.astype(BF16)


def kernel(x_prompt, x_sample, c_prompt, c_sample, cache_a_k, cache_a_v, state_b_C, state_b_n, state_b_m, state_b_conv, cache_c_k, cache_c_v, norm1_g, norm2_g, w_ada, b_ada, w_in, b_if_b, conv_w_b, conv_b_b, norm_b_g, rel_a, sink_c, w_br_a, w_br_b, w_br_c, w_o, w_ff1, w_ff3, w_ff2, w_router, b_router, w_e1, w_e3, w_e2, norm_f_g):
    nbp, sp, d = x_prompt.shape
    nbs, ss, _ = x_sample.shape
    tp, ts = nbp * sp, nbs * ss
    x = jnp.concatenate([x_prompt.reshape(tp, d), x_sample.reshape(ts, d)], axis=0)
    cond = jnp.concatenate([c_prompt, c_sample], axis=0)
    group_batch = np.concatenate([np.repeat(np.arange(nbp), sp // CHUNK), nbp + np.repeat(np.arange(nbs), ss // CHUNK)])

    lp_s = max(ss, LANE)
    l_b = 256 if sp % 256 == 0 else CHUNK
    keep_a, keep_c = min(PRE_A, sp), min(PRE_C, sp)
    zeros_state = (jnp.zeros((nbp, H_B, DK_B, DV_B), F32), jnp.zeros((nbp, H_B, DK_B), F32), jnp.zeros((nbp, H_B), F32))
    new = {k: [] for k in ("pak", "pav", "pbc", "pbn", "pbm", "pbx", "pck", "pcv",
                           "sak", "sav", "sbc", "sbn", "sbm", "sbx", "sck", "scv")}

    for l in range(DEPTH):
        mod = _ada(cond, w_ada[l], b_ada[l])
        mod_g = mod[group_batch].reshape(-1, 1, 6 * d)
        w_main, w_if_t = _split_w_in(w_in[l])
        u, g_t = _inproj(x, mod_g, norm1_g[l], w_main, w_if_t)

        bias_t = _rel_bias_table(rel_a[l])
        ya, pak, pav = _attn_a(u, bias_t, nb=nbp, s=sp, row0=0)
        ya, sak, sav = _attn_a(u, bias_t, nb=nbs, s=ss, row0=tp,
                               prefix=(cache_a_k[l].reshape(nbs, PRE_A, WA), cache_a_v[l].reshape(nbs, PRE_A, WA)),
                               out=ya)
        yc, pck, pcv = _attn_c(u, sink_c[l], nb=nbp, s=sp, row0=0, pos0=0)
        yc, sck, scv = _attn_c(u, sink_c[l], nb=nbs, s=ss, row0=tp, pos0=PAST_LEN,
                               prefix=(cache_c_k[l].reshape(nbs, PRE_C, WC_KV),
                                       cache_c_v[l].reshape(nbs, PRE_C, WC_KV)), out=yc)
        g_p = g_t[:, :tp].reshape(2 * H_B, nbp, sp).transpose(1, 0, 2)
        g_s = g_t[:, tp:].reshape(2 * H_B, nbs, ss).transpose(1, 0, 2)
        g_s = jnp.pad(g_s, ((0, 0), (0, 0), (0, lp_s - ss)))
        hist_p = jnp.zeros((nbp, 8, 2 * WB_QK), F32)
        hist_s = jnp.pad(state_b_conv[l], ((0, 0), (8 - (CONV_B - 1), 0), (0, 0)))
        bargs = (conv_w_b[l], conv_b_b[l], b_if_b[l], norm_b_g[l])
        yb, pbc, pbn, pbm, pbx = _mlstm(u, g_p, hist_p, *bargs, *zeros_state, nb=nbp, s=sp, row0=0, L=l_b)
        yb, sbc, sbn, sbm, sbx = _mlstm(u, g_s, hist_s, *bargs, state_b_C[l], state_b_n[l], state_b_m[l],
                                        nb=nbs, s=ss, row0=tp, L=ss, out=yb)

        x = _merge(ya, yb, yc, u, x, mod_g, w_br_a[l].astype(BF16), w_br_b[l].astype(BF16),
                   w_br_c[l].astype(BF16), w_o[l].astype(BF16))
        i = l // 2
        if l % 2 == 0:
            x = _ffn(x, mod_g, norm2_g[l], w_ff1[i].astype(BF16), w_ff3[i].astype(BF16), w_ff2[i].astype(BF16))
        else:
            x = _moe(x, mod_g, norm2_g[l], w_router[i], b_router[i], w_e1[i], w_e3[i], w_e2[i])

        new["pak"].append(pak.reshape(nbp, keep_a, H_A, DH_A))
        new["pav"].append(pav.reshape(nbp, keep_a, H_A, DH_A))
        new["pbc"].append(pbc)
        new["pbn"].append(pbn)
        new["pbm"].append(pbm)
        new["pbx"].append(pbx)
        new["pck"].append(pck.reshape(nbp, keep_c, HKV_C, DH_C))
        new["pcv"].append(pcv.reshape(nbp, keep_c, HKV_C, DH_C))
        new["sak"].append(sak.reshape(nbs, ss, H_A, DH_A))
        new["sav"].append(sav.reshape(nbs, ss, H_A, DH_A))
        new["sbc"].append(sbc)
        new["sbn"].append(sbn)
        new["sbm"].append(sbm)
        new["sbx"].append(sbx)
        new["sck"].append(sck.reshape(nbs, ss, HKV_C, DH_C))
        new["scv"].append(scv.reshape(nbs, ss, HKV_C, DH_C))

    y_p, y_s = _final_norm(x, norm_f_g, tp)
    st = {k: jnp.stack(v) for k, v in new.items()}
    return (y_p.reshape(nbp, sp, d), y_s.reshape(nbs, ss, d),
            st["pak"], st["pav"], st["pbc"], st["pbn"], st["pbm"], st["pbx"], st["pck"], st["pcv"],
            st["sak"], st["sav"], st["sbc"], st["sbn"], st["sbm"], st["sbx"], st["sck"], st["scv"])
```

```python
import functools

import numpy as np
import jax
import jax.numpy as jnp
from jax import lax
from jax.experimental import pallas as pl
from jax.experimental.pallas import tpu as pltpu

F32 = jnp.float32
BF16 = jnp.bfloat16

D_MODEL = 1024
DEPTH = 2
PAST_LEN = 2048
CHUNK = 64
H_A, DH_A, BACK_A, REL_CLIP = 8, 64, 8, 128
H_B, DK_B, DV_B, CONV_B = 4, 128, 256, 4
HQ_C, HKV_C, DH_C, BACK_C = 8, 2, 64, 2
ROT_DIM = DH_C // 4
ROPE_THETA = 500000.0
D_FF = 2816
N_EXPERTS = 8
D_FF_E = 3584
EPS = 1e-6
NEG = -1e30

WA = H_A * DH_A
WB_QK = H_B * DK_B
WB_V = H_B * DV_B
WC_Q = HQ_C * DH_C
WC_KV = HKV_C * DH_C
PRE_A = BACK_A * CHUNK
PRE_C = BACK_C * CHUNK

COL_QA, COL_KA, COL_VA, COL_QC = 0, 512, 1024, 1536
COL_QKB = 2048
COL_GATES = 3072
COL_VB = 6144
COL_OB = 7168
COL_KC = 8192
COL_VC = 8320
N_MAIN = 8448

LANE = 128
VMEM_LIMIT = 48 * 1024 * 1024


def _cparams(*sem):
    return pltpu.CompilerParams(dimension_semantics=sem, vmem_limit_bytes=VMEM_LIMIT)


def _row_tile(t):
    for tm in (512, 256, 128, 64):
        if t % tm == 0:
            return tm
    raise ValueError(f"token count {t} is not a multiple of 64")


def _ln_mod(x, g, sc, sh):
    tm, d = x.shape
    y = x * lax.rsqrt(jnp.mean(x * x, axis=-1, keepdims=True) + EPS) * g
    y = y.reshape(tm // CHUNK, CHUNK, d) * (1.0 + sc) + sh
    return y.reshape(tm, d)


def _mod_spec(tm, kind):
    return pl.BlockSpec((tm // CHUNK, 1, D_MODEL), lambda i, *_, k=kind: (i, 0, k))


def _ada_kernel(c_ref, w_ref, b_ref, o_ref):
    c = c_ref[...]
    a = (c * jax.nn.sigmoid(c)).astype(BF16)
    o_ref[...] = jnp.dot(a, w_ref[...].astype(BF16), preferred_element_type=F32) + b_ref[...]


def _ada(c, w, b):
    nb, d = c.shape
    n = w.shape[1]
    tn = 1536
    return pl.pallas_call(
        _ada_kernel,
        grid=(n // tn,),
        in_specs=[pl.BlockSpec((nb, d), lambda j: (0, 0)),
                  pl.BlockSpec((d, tn), lambda j: (0, j)),
                  pl.BlockSpec((1, tn), lambda j: (0, j))],
        out_specs=pl.BlockSpec((nb, tn), lambda j: (0, j)),
        out_shape=jax.ShapeDtypeStruct((nb, n), F32),
        compiler_params=_cparams("arbitrary"),
        name="adaln",
    )(c, w, b.reshape(1, n))


def _col_chunks(n, width=768):
    return [(c, min(width, n - c)) for c in range(0, n, width)]


def _resident(shape):
    return pl.BlockSpec(shape, lambda *_: (0,) * len(shape), pipeline_mode=pl.Buffered(1))


def _inproj_kernel(x_ref, sh_ref, sc_ref, g_ref, w_ref, wif_ref, u_ref, gt_ref, h_scr):
    h_scr[...] = _ln_mod(x_ref[...], g_ref[...], sc_ref[...], sh_ref[...]).astype(BF16)
    gt_ref[...] = lax.dot_general(wif_ref[...], h_scr[...], (((1,), (1,)), ((), ())), preferred_element_type=F32)
    for c, w in _col_chunks(N_MAIN):
        u_ref[:, c:c + w] = jnp.dot(h_scr[...], w_ref[:, c:c + w], preferred_element_type=F32)


def _inproj(x, mod_g, g, w_main, w_if_t):
    t, d = x.shape
    tm = min(256, _row_tile(t))
    return pl.pallas_call(
        _inproj_kernel,
        grid=(t // tm,),
        in_specs=[pl.BlockSpec((tm, d), lambda i: (i, 0)),
                  _mod_spec(tm, 0), _mod_spec(tm, 1),
                  _resident((1, d)), _resident((d, N_MAIN)), _resident((2 * H_B, d))],
        out_specs=[pl.BlockSpec((tm, N_MAIN), lambda i: (i, 0)),
                   pl.BlockSpec((2 * H_B, tm), lambda i: (0, i))],
        out_shape=[jax.ShapeDtypeStruct((t, N_MAIN), F32),
                   jax.ShapeDtypeStruct((2 * H_B, t), F32)],
        scratch_shapes=[pltpu.VMEM((tm, d), BF16)],
        compiler_params=_cparams("parallel"),
        name="inproj",
    )(x, mod_g, mod_g, g.reshape(1, d), w_main, w_if_t)


def _band_table(bq, pre, back):
    qi = np.arange(bq)[:, None] // CHUNK
    kj = np.arange(pre + bq)[None, :] // CHUNK
    return (kj >= qi) & (kj <= qi + back)


QBLK = 128
KV_CHUNK = 512


def _pad_rows(x, rows):
    return x if x.shape[0] == rows else jnp.concatenate(
        [x, jnp.zeros((rows - x.shape[0], x.shape[1]), x.dtype)], axis=0)


def _stage_keys_values(k_rows, v_ref, prefix_k, prefix_v, kpad, vt, pre, s):
    w = kpad.shape[1]
    s_pad = kpad.shape[0] - pre
    if prefix_k is None:
        kpad[0:pre, :] = jnp.zeros((pre, w), BF16)
        vt[:, 0:pre] = jnp.zeros((w, pre), BF16)
    else:
        kpad[0:pre, :] = prefix_k.astype(BF16)
        vt[:, 0:pre] = prefix_v.T.astype(BF16)
    for c in range(0, s_pad, KV_CHUNK):
        rows = min(KV_CHUNK, s_pad - c)
        real = max(0, min(rows, s - c))
        kpad[pre + c:pre + c + rows, :] = _pad_rows(k_rows(c, real), rows).astype(BF16)
        vt[:, pre + c:pre + c + rows] = _pad_rows(v_ref[c:c + real, :], rows).T.astype(BF16)


def _pair_store(o_ref, ots, lane0, rows):
    pair = jnp.concatenate(ots, axis=0).T
    o_ref[:, lane0:lane0 + pair.shape[1]] = pair[0:rows, :]


def _attn_a_kernel(*refs, s, has_prefix):
    if has_prefix:
        q_ref, k_ref, v_ref, pk_ref, pv_ref, bias_ref, o_ref, kt_ref, vt_ref, kpad, vt = refs
    else:
        q_ref, k_ref, v_ref, bias_ref, o_ref, kt_ref, vt_ref, kpad, vt = refs
    n = pl.program_id(1)
    kw = PRE_A + QBLK
    keep = kt_ref.shape[1]
    sq = q_ref.shape[0]

    @pl.when(n == 0)
    def _():
        _stage_keys_values(lambda c, rows: k_ref[c:c + rows, :], v_ref,
                           pk_ref[0] if has_prefix else None, pv_ref[0] if has_prefix else None,
                           kpad, vt, PRE_A, s)
        kt_ref[0] = k_ref[s - keep:s, :]
        vt_ref[0] = v_ref[s - keep:s, :]

    r0 = pl.multiple_of(n * QBLK, QBLK)
    q = _pad_rows((q_ref[...] * (DH_A ** -0.5)).astype(BF16), QBLK)
    heads = [slice(h * DH_A, (h + 1) * DH_A) for h in range(H_A)]
    nt = (((1,), (1,)), ((), ()))

    def attend(mask_keys):
        scores = [lax.dot_general(kpad[pl.ds(r0, kw), hs], q[:, hs], nt, preferred_element_type=F32)
                  for hs in heads]
        if mask_keys:
            key = lax.broadcasted_iota(jnp.int32, (kw, QBLK), 0)
            invalid = jnp.where(key >= PRE_A - r0, 0.0, NEG).astype(F32)
        probs, sums = [], []
        for h in range(H_A):
            sc = scores[h] + bias_ref[h]
            if mask_keys:
                sc = sc + invalid
            e = jnp.exp(sc - jnp.max(sc, axis=0, keepdims=True))
            sums.append(jnp.sum(e, axis=0, keepdims=True))
            probs.append(e.astype(BF16))
        for h in range(0, H_A, 2):
            ots = [jnp.dot(vt[heads[i], pl.ds(r0, kw)], probs[i], preferred_element_type=F32) / sums[i]
                   for i in (h, h + 1)]
            _pair_store(o_ref, ots, h * DH_A, sq)

    if has_prefix:
        attend(False)
    else:
        pl.when(r0 < PRE_A)(lambda: attend(True))
        pl.when(r0 >= PRE_A)(lambda: attend(False))


def _rel_bias_table(rel):
    kw = PRE_A + QBLK
    n = QBLK + kw
    m = np.arange(n)
    diag = rel.astype(F32)[np.clip(m + 1 - QBLK, -REL_CLIP, REL_CLIP) + REL_CLIP].T
    skew = jnp.tile(diag, (1, kw + 1))[:, :kw * (n - 1)].reshape(H_A, kw, n - 1)
    table = skew[:, :, kw - 1:kw - 1 + QBLK]
    return jnp.where(_band_table(QBLK, PRE_A, BACK_A).T[None], table, NEG)


def _attn_a(u, bias_t, *, nb, s, row0, prefix=None, out=None):
    t = u.shape[0]
    bq = min(QBLK, s)
    nq = s // bq
    s_pad = max(s, QBLK)
    kw = PRE_A + QBLK
    keep = min(PRE_A, s)
    rb, sb = row0 // bq, row0 // s
    in_specs = [pl.BlockSpec((bq, WA), lambda b, n: (rb + b * nq + n, COL_QA // WA)),
                pl.BlockSpec((s, WA), lambda b, n: (sb + b, COL_KA // WA)),
                pl.BlockSpec((s, WA), lambda b, n: (sb + b, COL_VA // WA))]
    args = [u, u, u]
    if prefix is not None:
        in_specs += [pl.BlockSpec((1, PRE_A, WA), lambda b, n: (b, 0, 0))] * 2
        args += [prefix[0], prefix[1]]
    in_specs.append(pl.BlockSpec((H_A, kw, QBLK), lambda b, n: (0, 0, 0)))
    args.append(bias_t)
    aliases = {}
    if out is not None:
        in_specs.append(pl.BlockSpec(memory_space=pl.ANY))
        args.append(out)
        aliases = {len(args) - 1: 0}
    body = functools.partial(_attn_a_kernel, s=s, has_prefix=prefix is not None)
    if out is not None:
        body = _drop_last_input(body, len(args))
    return pl.pallas_call(
        body,
        grid=(nb, nq),
        in_specs=in_specs,
        out_specs=[pl.BlockSpec((bq, WA), lambda b, n: (rb + b * nq + n, 0)),
                   pl.BlockSpec((1, keep, WA), lambda b, n: (b, 0, 0)),
                   pl.BlockSpec((1, keep, WA), lambda b, n: (b, 0, 0))],
        out_shape=[jax.ShapeDtypeStruct((t, WA), F32),
                   jax.ShapeDtypeStruct((nb, keep, WA), F32),
                   jax.ShapeDtypeStruct((nb, keep, WA), F32)],
        scratch_shapes=[pltpu.VMEM((PRE_A + s_pad, WA), BF16), pltpu.VMEM((WA, PRE_A + s_pad), BF16)],
        input_output_aliases=aliases,
        compiler_params=_cparams("parallel", "arbitrary"),
        name="mixer_a_sample" if prefix is not None else "mixer_a_prompt",
    )(*args)


def _drop_last_input(body, n_in):
    def wrapped(*refs):
        return body(*refs[:n_in - 1], *refs[n_in:])
    return wrapped


def _rope(x, c, s1, s2):
    w = x.shape[1]
    return x * c + pltpu.roll(x, 8, 1) * s1 + pltpu.roll(x, w - 8, 1) * s2


def _attn_c_kernel(*refs, s, has_prefix):
    if has_prefix:
        (q_ref, k_ref, v_ref, pk_ref, pv_ref, rc_ref, rs1_ref, rs2_ref, band_ref, sink_ref,
         o_ref, kt_ref, vt_ref, kpad, vt) = refs
    else:
        (q_ref, k_ref, v_ref, rc_ref, rs1_ref, rs2_ref, band_ref, sink_ref,
         o_ref, kt_ref, vt_ref, kpad, vt) = refs
    n = pl.program_id(1)
    kw = PRE_C + QBLK
    keep = kt_ref.shape[1]
    sq = q_ref.shape[0]

    def rope_at(x, start, rep=1):
        tables = [jnp.concatenate([t_ref[pl.ds(start, x.shape[0]), :]] * rep, axis=1)
                  for t_ref in (rc_ref, rs1_ref, rs2_ref)]
        return _rope(x, *tables)

    @pl.when(n == 0)
    def _():
        _stage_keys_values(lambda c, rows: rope_at(k_ref[c:c + rows, :], c), v_ref,
                           pk_ref[0] if has_prefix else None, pv_ref[0] if has_prefix else None,
                           kpad, vt, PRE_C, s)
        kt_ref[0] = rope_at(k_ref[s - keep:s, :], s - keep)
        vt_ref[0] = v_ref[s - keep:s, :]

    r0 = pl.multiple_of(n * QBLK, QBLK)
    q = rope_at(q_ref[...], pl.multiple_of(n * sq, sq), rep=WC_Q // WC_KV)
    q = _pad_rows((q * (DH_C ** -0.5)).astype(BF16), QBLK)
    per = HQ_C // HKV_C
    groups = [slice(g * DH_C, (g + 1) * DH_C) for g in range(HKV_C)]
    nt = (((1,), (1,)), ((), ()))

    def attend(mask_keys):
        scores = []
        for g, gs in enumerate(groups):
            qs = jnp.concatenate([q[:, (g * per + i) * DH_C:(g * per + i + 1) * DH_C] for i in range(per)], axis=0)
            scores.append(lax.dot_general(kpad[pl.ds(r0, kw), gs], qs, nt, preferred_element_type=F32))
        mask = band_ref[...]
        if mask_keys:
            key = lax.broadcasted_iota(jnp.int32, mask.shape, 0)
            mask = mask + jnp.where(key >= PRE_C - r0, 0.0, NEG).astype(F32)
        probs, sums = [], []
        for g in range(HKV_C):
            sk = jnp.concatenate([jnp.broadcast_to(sink_ref[g * per + i:g * per + i + 1, 0:1], (1, QBLK))
                                  for i in range(per)], axis=1)
            sc = scores[g] + mask
            m = jnp.maximum(jnp.max(sc, axis=0, keepdims=True), sk)
            e = jnp.exp(sc - m)
            sums.append(jnp.sum(e, axis=0, keepdims=True) + jnp.exp(sk - m))
            probs.append(e.astype(BF16))
        for g, gs in enumerate(groups):
            ot = jnp.dot(vt[gs, pl.ds(r0, kw)], probs[g], preferred_element_type=F32) / sums[g]
            for i in range(0, per, 2):
                _pair_store(o_ref, [ot[:, (i + j) * QBLK:(i + j + 1) * QBLK] for j in (0, 1)],
                            (g * per + i) * DH_C, sq)

    if has_prefix:
        attend(False)
    else:
        pl.when(r0 < PRE_C)(lambda: attend(True))
        pl.when(r0 >= PRE_C)(lambda: attend(False))


def _rope_tables(pos):
    half = ROT_DIM // 2
    inv_freq = 1.0 / (ROPE_THETA ** (jnp.arange(half, dtype=F32) * (2.0 / ROT_DIM)))
    ang = pos.astype(F32)[:, None] * inv_freq[None, :]
    cos, sin = jnp.cos(ang), jnp.sin(ang)
    n = pos.shape[0]
    one = jnp.ones((n, DH_C - ROT_DIM), F32)
    zero = jnp.zeros((n, DH_C - ROT_DIM), F32)
    zh = jnp.zeros((n, half), F32)
    c = jnp.concatenate([cos, cos, one], axis=1)
    s1 = jnp.concatenate([zh, sin, zero], axis=1)
    s2 = jnp.concatenate([-sin, zh, zero], axis=1)
    rep = WC_KV // DH_C
    return tuple(jnp.concatenate([a] * rep, axis=1) for a in (c, s1, s2))


def _attn_c(u, sink, *, nb, s, row0, pos0, prefix=None, out=None):
    t = u.shape[0]
    bq = min(QBLK, s)
    nq = s // bq
    s_pad = max(s, QBLK)
    kw = PRE_C + QBLK
    per = HQ_C // HKV_C
    tables = _rope_tables(pos0 + jnp.arange(s))
    band = np.where(_band_table(QBLK, PRE_C, BACK_C), 0.0, NEG).astype(np.float32)
    band = jnp.asarray(np.tile(band.T, (1, per)))
    sink_t = jnp.broadcast_to(sink.astype(F32)[:, None], (HQ_C, LANE))
    rb, sb = row0 // bq, row0 // s
    in_specs = [pl.BlockSpec((bq, WC_Q), lambda b, n: (rb + b * nq + n, COL_QC // WC_Q)),
                pl.BlockSpec((s, WC_KV), lambda b, n: (sb + b, COL_KC // WC_KV)),
                pl.BlockSpec((s, WC_KV), lambda b, n: (sb + b, COL_VC // WC_KV))]
    args = [u, u, u]
    if prefix is not None:
        in_specs += [pl.BlockSpec((1, PRE_C, WC_KV), lambda b, n: (b, 0, 0))] * 2
        args += [prefix[0], prefix[1]]
    in_specs += [pl.BlockSpec((s, WC_KV), lambda b, n: (0, 0))] * 3
    args += list(tables)
    in_specs += [pl.BlockSpec((kw, per * QBLK), lambda b, n: (0, 0)),
                 pl.BlockSpec((HQ_C, LANE), lambda b, n: (0, 0))]
    args += [band, sink_t]
    aliases = {}
    body = functools.partial(_attn_c_kernel, s=s, has_prefix=prefix is not None)
    if out is not None:
        in_specs.append(pl.BlockSpec(memory_space=pl.ANY))
        args.append(out)
        aliases = {len(args) - 1: 0}
        body = _drop_last_input(body, len(args))
    keep = min(PRE_C, s)
    return pl.pallas_call(
        body,
        grid=(nb, nq),
        in_specs=in_specs,
        out_specs=[pl.BlockSpec((bq, WC_Q), lambda b, n: (rb + b * nq + n, 0)),
                   pl.BlockSpec((1, keep, WC_KV), lambda b, n: (b, 0, 0)),
                   pl.BlockSpec((1, keep, WC_KV), lambda b, n: (b, 0, 0))],
        out_shape=[jax.ShapeDtypeStruct((t, WC_Q), F32),
                   jax.ShapeDtypeStruct((nb, keep, WC_KV), F32),
                   jax.ShapeDtypeStruct((nb, keep, WC_KV), F32)],
        scratch_shapes=[pltpu.VMEM((PRE_C + s_pad, WC_KV), BF16), pltpu.VMEM((WC_KV, PRE_C + s_pad), BF16)],
        input_output_aliases=aliases,
        compiler_params=_cparams("parallel", "arbitrary"),
        name="mixer_c_sample" if prefix is not None else "mixer_c_prompt",
    )(*args)


def _log_sigmoid(x):
    return jnp.minimum(x, 0.0) - jnp.log(1.0 + jnp.exp(-jnp.abs(x)))


def _lane_cumsum(x):
    n = x.shape[1]
    col = lax.broadcasted_iota(jnp.int32, x.shape, 1)
    k = 1
    while k < n:
        x = x + jnp.where(col >= k, pltpu.roll(x, k, 1), 0.0)
        k *= 2
    return x


def _mlstm_kernel(qk_ref, v_ref, og_ref, g_ref, hist_ref, cw_ref, cb_ref, bif_ref, ng_ref,
                  c0_ref, n0_ref, m0_ref, y_ref, cf_ref, nf_ref, mf_ref, xt_ref,
                  xpad, c_s, n_s, m_s, *, L):
    c = pl.program_id(1)

    @pl.when(c == 0)
    def _():
        c_s[...] = c0_ref[0]
        n_s[...] = n0_ref[0]
        m_s[...] = m0_ref[0]
        xpad[0:8, :] = hist_ref[0]

    x = qk_ref[...]
    xpad[8:8 + L, :] = x
    conv = (cb_ref[...] + x * cw_ref[3:4, :] + xpad[7:7 + L, :] * cw_ref[2:3, :]
            + xpad[6:6 + L, :] * cw_ref[1:2, :] + xpad[5:5 + L, :] * cw_ref[0:1, :])
    xpad[0:8, :] = xpad[L:L + 8, :]
    act = conv * jax.nn.sigmoid(conv)

    gb = g_ref[...] + bif_ref[:, 0:1]
    ig_rows = gb[0:H_B, :]
    b_rows = _lane_cumsum(_log_sigmoid(gb))[H_B:2 * H_B, :]

    row = lax.broadcasted_iota(jnp.int32, (L, L), 0)
    colm = lax.broadcasted_iota(jnp.int32, (L, L), 1)
    eye = row == colm
    causal = colm <= row

    def to_col(r):
        return jnp.sum(jnp.where(eye, r, 0.0), axis=1, keepdims=True)

    heads = range(H_B)
    nt = (((1,), (1,)), ((), ()))
    qf = [act[:, h * DK_B:(h + 1) * DK_B] for h in heads]
    kf = [act[:, WB_QK + h * DK_B:WB_QK + (h + 1) * DK_B] * (DK_B ** -0.5) for h in heads]
    qb = [q.astype(BF16) for q in qf]
    vb = [v_ref[:, h * DV_B:(h + 1) * DV_B].astype(BF16) for h in heads]
    cmat = [c_s[h] for h in heads]
    nvec = [n_s[h] for h in heads]
    qk = [lax.dot_general(qb[h], kf[h].astype(BF16), nt, preferred_element_type=F32) for h in heads]
    qc = [jnp.dot(qb[h], cmat[h].astype(BF16), preferred_element_type=F32) for h in heads]

    a, wi, mt, bc, igc = [], [], [], [], []
    for h in heads:
        br = b_rows[h:h + 1, 0:L]
        igr = ig_rows[h:h + 1, 0:L]
        bc.append(to_col(br))
        igc.append(to_col(igr))
        logd = jnp.where(causal, bc[h] + (igr - br), NEG)
        li = bc[h] + m_s[h][:, 0:1]
        mt.append(jnp.maximum(li, jnp.max(logd, axis=1, keepdims=True)))
        wi.append(jnp.exp(li - mt[h]))
        a.append(qk[h] * jnp.exp(logd - mt[h]))

    pv = [jnp.dot(a[h].astype(BF16), vb[h], preferred_element_type=F32) for h in heads]
    for h in heads:
        num = pv[h] + wi[h] * qc[h]
        den = jnp.sum(a[h], axis=1, keepdims=True) + wi[h] * jnp.sum(qf[h] * nvec[h], axis=1, keepdims=True)
        hh = num / jnp.maximum(jnp.abs(den), jnp.exp(-mt[h]))
        hn = hh * lax.rsqrt(jnp.mean(hh * hh, axis=-1, keepdims=True) + EPS)
        vs = slice(h * DV_B, (h + 1) * DV_B)
        y_ref[:, vs] = hn * ng_ref[:, vs] * jax.nn.sigmoid(og_ref[:, vs])

    for h in heads:
        bl, ml, wprev = bc[h][L - 1:L, :], mt[h][L - 1:L, :], wi[h][L - 1:L, :]
        kwt = kf[h] * jnp.exp(bl + igc[h] - bc[h] - ml)
        c_s[h] = wprev * cmat[h] + lax.dot_general(kwt.astype(BF16), vb[h], (((0,), (0,)), ((), ())),
                                                   preferred_element_type=F32)
        n_s[h] = wprev * nvec[h] + jnp.sum(kwt, axis=0, keepdims=True)
        m_s[h] = jnp.broadcast_to(ml, (1, LANE))

    @pl.when(c == pl.num_programs(1) - 1)
    def _():
        cf_ref[0] = c_s[...]
        nf_ref[0] = n_s[...]
        mf_ref[0] = m_s[...]
        xt_ref[0] = xpad[0:8, :]


def _mlstm(u, g_rows, hist, cw, cb, bif, ng, c0, n0, m0, *, nb, s, row0, L, out=None):
    t = u.shape[0]
    nc = s // L
    lp = g_rows.shape[2] // nc
    rb = row0 // L
    n0 = n0.reshape(nb, H_B, 1, DK_B)
    m0 = jnp.broadcast_to(m0.reshape(nb, H_B, 1, 1), (nb, H_B, 1, LANE))
    bif_t = jnp.broadcast_to(bif.astype(F32)[:, None], (2 * H_B, LANE))
    row_spec = lambda col: pl.BlockSpec((L, WB_V), lambda b, c: (rb + b * nc + c, col))
    full = lambda shape: pl.BlockSpec(shape, lambda b, c: (0,) * len(shape))
    state = lambda shape: pl.BlockSpec((1,) + shape, lambda b, c: (b,) + (0,) * len(shape))
    in_specs = [row_spec(COL_QKB // WB_V), row_spec(COL_VB // WB_V), row_spec(COL_OB // WB_V),
                pl.BlockSpec((None, 2 * H_B, lp), lambda b, c: (b, 0, c)),
                state((8, 2 * WB_QK)),
                full((CONV_B, 2 * WB_QK)), full((1, 2 * WB_QK)), full((2 * H_B, LANE)), full((1, WB_V)),
                state((H_B, DK_B, DV_B)), state((H_B, 1, DK_B)), state((H_B, 1, LANE))]
    args = [u, u, u, g_rows, hist, cw, cb.reshape(1, -1), bif_t, ng.reshape(1, -1), c0, n0, m0]
    n_real = len(args)
    aliases = {}
    body = functools.partial(_mlstm_kernel, L=L)
    if out is not None:
        in_specs.append(pl.BlockSpec(memory_space=pl.ANY))
        args.append(out)
        aliases = {n_real: 0}
        body = _drop_last_input(body, len(args))
    y, cf, nf, mf, xt = pl.pallas_call(
        body,
        grid=(nb, nc),
        in_specs=in_specs,
        out_specs=[pl.BlockSpec((L, WB_V), lambda b, c: (rb + b * nc + c, 0)),
                   state((H_B, DK_B, DV_B)), state((H_B, 1, DK_B)), state((H_B, 1, LANE)),
                   state((8, 2 * WB_QK))],
        out_shape=[jax.ShapeDtypeStruct((t, WB_V), F32),
                   jax.ShapeDtypeStruct((nb, H_B, DK_B, DV_B), F32),
                   jax.ShapeDtypeStruct((nb, H_B, 1, DK_B), F32),
                   jax.ShapeDtypeStruct((nb, H_B, 1, LANE), F32),
                   jax.ShapeDtypeStruct((nb, 8, 2 * WB_QK), F32)],
        scratch_shapes=[pltpu.VMEM((L + 8, 2 * WB_QK), F32),
                        pltpu.VMEM((H_B, DK_B, DV_B), F32),
                        pltpu.VMEM((H_B, 1, DK_B), F32),
                        pltpu.VMEM((H_B, 1, LANE), F32)],
        input_output_aliases=aliases,
        compiler_params=_cparams("parallel", "arbitrary"),
        name="mlstm_sample" if out is not None else "mlstm_prompt",
    )(*args)
    return y, cf, nf.reshape(nb, H_B, DK_B), mf[:, :, 0, 0], xt[:, 8 - (CONV_B - 1):, :]


def _merge_kernel(ya_ref, yb_ref, yc_ref, gt_ref, x_ref, g1_ref, wa_ref, wb_ref, wc_ref, wo_ref, o_ref):
    tm, d = x_ref.shape
    sg = jax.nn.sigmoid(gt_ref[...])
    m = (sg[:, 0:d] * jnp.dot(ya_ref[...].astype(BF16), wa_ref[...], preferred_element_type=F32)
         + sg[:, d:2 * d] * jnp.dot(yb_ref[...].astype(BF16), wb_ref[...], preferred_element_type=F32)
         + sg[:, 2 * d:3 * d] * jnp.dot(yc_ref[...].astype(BF16), wc_ref[...], preferred_element_type=F32))
    o = jnp.dot(m.astype(BF16), wo_ref[...], preferred_element_type=F32)
    o = (o.reshape(tm // CHUNK, CHUNK, d) * g1_ref[...]).reshape(tm, d)
    o_ref[...] = x_ref[...] + o


def _merge(ya, yb, yc, u, x, mod_g, wa, wb, wc, wo):
    t, d = x.shape
    tm = _row_tile(t)
    row = lambda w, col=0: pl.BlockSpec((tm, w), lambda i: (i, col))
    full = lambda a: _resident(a.shape)
    return pl.pallas_call(
        _merge_kernel,
        grid=(t // tm,),
        in_specs=[row(WA), row(WB_V), row(WC_Q), row(3 * d, COL_GATES // (3 * d)), row(d),
                  _mod_spec(tm, 2), full(wa), full(wb), full(wc), full(wo)],
        out_specs=row(d),
        out_shape=jax.ShapeDtypeStruct((t, d), F32),
        compiler_params=_cparams("parallel"),
        name="merge",
    )(ya, yb, yc, u, x, mod_g, wa, wb, wc, wo)


def _ffn_kernel(x_ref, sh_ref, sc_ref, g_ref, g2_ref, w1_ref, w3_ref, w2_ref, o_ref, h_scr):
    tm, d = x_ref.shape
    h_scr[...] = _ln_mod(x_ref[...], g_ref[...], sc_ref[...], sh_ref[...]).astype(BF16)
    acc = None
    for c, w in _col_chunks(w1_ref.shape[1]):
        a = jnp.dot(h_scr[...], w1_ref[:, c:c + w], preferred_element_type=F32)
        b = jnp.dot(h_scr[...], w3_ref[:, c:c + w], preferred_element_type=F32)
        tt = (a * jax.nn.sigmoid(a) * b).astype(BF16)
        p = jnp.dot(tt, w2_ref[c:c + w, :], preferred_element_type=F32)
        acc = p if acc is None else acc + p
    f = (acc.reshape(tm // CHUNK, CHUNK, d) * g2_ref[...]).reshape(tm, d)
    o_ref[...] = x_ref[...] + f


def _ffn(x, mod_g, g, w1, w3, w2):
    t, d = x.shape
    tm = _row_tile(t)
    ff = w1.shape[1]
    return pl.pallas_call(
        _ffn_kernel,
        grid=(t // tm,),
        in_specs=[pl.BlockSpec((tm, d), lambda i: (i, 0)),
                  _mod_spec(tm, 3), _mod_spec(tm, 4),
                  _resident((1, d)),
                  _mod_spec(tm, 5),
                  _resident((d, ff)), _resident((d, ff)), _resident((ff, d))],
        out_specs=pl.BlockSpec((tm, d), lambda i: (i, 0)),
        out_shape=jax.ShapeDtypeStruct((t, d), F32),
        scratch_shapes=[pltpu.VMEM((tm, d), BF16)],
        compiler_params=_cparams("parallel"),
        name="ffn_dense",
    )(x, mod_g, mod_g, g.reshape(1, d), mod_g, w1, w3, w2)


def _to_row_tiles(ref, x):
    for s in range(x.shape[1] // LANE):
        ref[:, s, :] = x[:, s * LANE:(s + 1) * LANE]


def _router_kernel(x_ref, sh_ref, sc_ref, g_ref, wr_ref, br_ref, h_ref, sel_ref, cnt_ref, count):
    i = pl.program_id(0)

    @pl.when(i == 0)
    def _():
        count[...] = jnp.zeros_like(count)

    h = _ln_mod(x_ref[...], g_ref[...], sc_ref[...], sh_ref[...])
    _to_row_tiles(h_ref, h)
    lg = jnp.dot(h.astype(BF16), wr_ref[...], preferred_element_type=F32) + br_ref[...]
    tm = lg.shape[0]
    lane = lax.broadcasted_iota(jnp.int32, lg.shape, 1).astype(F32)
    m1 = jnp.max(lg, axis=1, keepdims=True)
    i1 = jnp.min(jnp.where(lg == m1, lane, float(LANE)), axis=1, keepdims=True)
    lg2 = jnp.where(lane == i1, 2.0 * NEG, lg)
    m2 = jnp.max(lg2, axis=1, keepdims=True)
    i2 = jnp.min(jnp.where(lg2 == m2, lane, float(LANE)), axis=1, keepdims=True)
    e2 = jnp.exp(m2 - m1)
    w1 = 1.0 / (1.0 + e2)
    oh1 = (lane == i1).astype(F32)
    oh2 = (lane == i2).astype(F32)
    both = oh1 + oh2
    earlier = (lax.broadcasted_iota(jnp.int32, (tm, tm), 1) < lax.broadcasted_iota(jnp.int32, (tm, tm), 0))
    before = count[...] + jnp.dot(earlier.astype(BF16), both.astype(BF16), preferred_element_type=F32)
    r1 = jnp.sum(oh1 * before, axis=1, keepdims=True)
    r2 = jnp.sum(oh2 * (before + oh1), axis=1, keepdims=True)
    count[...] += jnp.sum(both, axis=0, keepdims=True)
    cnt_ref[...] = count[...]
    sel_ref[...] = (jnp.where(lane == 0.0, i1, 0.0) + jnp.where(lane == 1.0, i2, 0.0)
                    + jnp.where(lane == 2.0, w1, 0.0) + jnp.where(lane == 3.0, e2 * w1, 0.0)
                    + jnp.where(lane == 4.0, r1, 0.0) + jnp.where(lane == 5.0, r2, 0.0))


def _router(x, mod_g, g, w_r, b_r):
    t, d = x.shape
    tm = _row_tile(t)
    wr = jnp.zeros((d, LANE), BF16).at[:, :N_EXPERTS].set(w_r.astype(BF16))
    br = jnp.full((1, LANE), NEG, F32).at[0, :N_EXPERTS].set(b_r.astype(F32))
    return pl.pallas_call(
        _router_kernel,
        grid=(t // tm,),
        in_specs=[pl.BlockSpec((tm, d), lambda i: (i, 0)),
                  _mod_spec(tm, 3), _mod_spec(tm, 4),
                  pl.BlockSpec((1, d), lambda i: (0, 0)),
                  pl.BlockSpec((d, LANE), lambda i: (0, 0)),
                  pl.BlockSpec((1, LANE), lambda i: (0, 0))],
        out_specs=[pl.BlockSpec((tm, d // LANE, LANE), lambda i: (i, 0, 0)),
                   pl.BlockSpec((tm, LANE), lambda i: (i, 0)),
                   pl.BlockSpec((1, LANE), lambda i: (0, 0))],
        out_shape=[jax.ShapeDtypeStruct((t, d // LANE, LANE), F32), jax.ShapeDtypeStruct((t, LANE), F32),
                   jax.ShapeDtypeStruct((1, LANE), F32)],
        scratch_shapes=[pltpu.VMEM((1, LANE), F32)],
        compiler_params=_cparams("arbitrary"),
        name="router",
    )(x, mod_g, mod_g, g.reshape(1, d), wr, br)


def _route(sel, counts, tmr):
    t = sel.shape[0]
    i32 = jnp.int32
    experts = jnp.arange(N_EXPERTS, dtype=i32)[None, :]
    e = sel[:, 0:2].astype(i32).reshape(-1)
    oh = (e[:, None] == experts).astype(i32)
    cnt = counts[0, :N_EXPERTS].astype(i32)
    rank = sel[:, 4:6].astype(i32).reshape(-1)
    ntile_e = (cnt + tmr - 1) // tmr
    tile_end = jnp.cumsum(ntile_e)
    tile_start = tile_end - ntile_e
    pos = jnp.sum(oh * (tile_start * tmr)[None, :], axis=1) + rank
    n_tiles = -(-2 * t // tmr) + N_EXPERTS
    n_used = tile_end[-1]
    tiles = jnp.arange(n_tiles, dtype=i32)
    tc = jnp.minimum(tiles, n_used - 1)
    te = jnp.sum((tc[:, None] >= tile_end[None, :]).astype(i32), axis=1)
    ohe = (te[:, None] == experts).astype(i32)
    nvalid = jnp.sum(ohe * cnt[None, :], axis=1) - (tc - jnp.sum(ohe * tile_start[None, :], axis=1)) * tmr
    nvalid = jnp.where(tiles < n_used, jnp.clip(nvalid, 0, tmr), 0)
    return pos.astype(i32), te.astype(i32), nvalid.astype(i32), n_used.reshape(1).astype(i32)


EXPERT_TILE = 1024
EXPERT_SUB = 256
ROUTE_TILE = 256


def _row_copies(n_rows, make_copy):
    def issue(r, carry):
        for k in range(2):
            make_copy(r, k).start(priority=k)
        return carry

    def drain(r, carry):
        for k in range(2):
            make_copy(r, k).wait()
        return carry

    lax.fori_loop(0, n_rows, issue, 0, unroll=8)
    lax.fori_loop(0, n_rows, drain, 0, unroll=8)


def _dispatch_kernel(pos_ref, h_ref, xs_in_ref, xs_ref, sem):
    del xs_in_ref

    def make_copy(r, k):
        p = pos_ref[0, 0, 2 * r + k]
        return pltpu.make_async_copy(h_ref.at[pl.ds(r, 1)], xs_ref.at[pl.ds(p, 1)], sem)

    _row_copies(h_ref.shape[0], make_copy)


def _dispatch(h, pos, n_rows):
    t = h.shape[0]
    tm = min(ROUTE_TILE, _row_tile(t))
    xs0 = jnp.zeros((n_rows,) + h.shape[1:], F32)
    return pl.pallas_call(
        _dispatch_kernel,
        grid=(t // tm,),
        in_specs=[pl.BlockSpec((1, 1, 2 * tm), lambda i: (i, 0, 0), memory_space=pltpu.SMEM),
                  pl.BlockSpec((tm,) + h.shape[1:], lambda i: (i, 0, 0)),
                  pl.BlockSpec(memory_space=pl.ANY)],
        out_specs=pl.BlockSpec(memory_space=pl.ANY),
        out_shape=jax.ShapeDtypeStruct(xs0.shape, F32),
        scratch_shapes=[pltpu.SemaphoreType.DMA(())],
        input_output_aliases={2: 0},
        compiler_params=_cparams("arbitrary"),
        name="moe_dispatch",
    )(pos.reshape(t // tm, 1, 2 * tm), h, xs0)


def _experts_kernel(te_ref, nv_ref, nu_ref, x_ref, w1_ref, w3_ref, w2_ref, y_ref, xb, w1b, w3b, w2b, acc):
    del te_ref, nu_ref
    i, j = pl.program_id(0), pl.program_id(1)
    nv = nv_ref[i]

    @pl.when(nv > 0)
    def _():
        @pl.when(j == 0)
        def _():
            for s in range(x_ref.shape[1]):
                xb[:, s * LANE:(s + 1) * LANE] = x_ref[:, s, :].astype(BF16)
            acc[...] = jnp.zeros_like(acc)

        w1b[...] = w1_ref[0].astype(BF16)
        w3b[...] = w3_ref[0].astype(BF16)
        w2b[...] = w2_ref[0].astype(BF16)

        for m in range(EXPERT_SUB, EXPERT_TILE + 1, EXPERT_SUB):
            @pl.when((nv > m - EXPERT_SUB) & (nv <= m))
            def _(m=m):
                xs = xb[0:m, :]
                a = jnp.dot(xs, w1b[...], preferred_element_type=F32)
                b = jnp.dot(xs, w3b[...], preferred_element_type=F32)
                tt = (a * jax.nn.sigmoid(a) * b).astype(BF16)
                acc[0:m, :] += jnp.dot(tt, w2b[...], preferred_element_type=F32)

        @pl.when(j == pl.num_programs(1) - 1)
        def _():
            _to_row_tiles(y_ref, acc[...])


def _experts(xs, te, nvalid, n_used, w1, w3, w2):
    n_rows = xs.shape[0]
    ne, d, ff = w1.shape
    tf = 512
    nj = ff // tf
    n_tiles = n_rows // EXPERT_TILE
    tile_rows = (EXPERT_TILE,) + xs.shape[1:]

    def row_map(i, j, te_ref, nv_ref, nu_ref):
        return (jnp.minimum(i, nu_ref[0] - 1), 0, 0)

    def col(i, j, nu_ref):
        return jnp.where(i < nu_ref[0], j, nj - 1)

    grid_spec = pltpu.PrefetchScalarGridSpec(
        num_scalar_prefetch=3,
        grid=(n_tiles, nj),
        in_specs=[pl.BlockSpec(tile_rows, row_map),
                  pl.BlockSpec((1, d, tf), lambda i, j, te_ref, nv_ref, nu_ref: (te_ref[i], 0, col(i, j, nu_ref))),
                  pl.BlockSpec((1, d, tf), lambda i, j, te_ref, nv_ref, nu_ref: (te_ref[i], 0, col(i, j, nu_ref))),
                  pl.BlockSpec((1, tf, d), lambda i, j, te_ref, nv_ref, nu_ref: (te_ref[i], col(i, j, nu_ref), 0))],
        out_specs=pl.BlockSpec(tile_rows, row_map),
        scratch_shapes=[pltpu.VMEM((EXPERT_TILE, d), BF16), pltpu.VMEM((d, tf), BF16), pltpu.VMEM((d, tf), BF16),
                        pltpu.VMEM((tf, d), BF16), pltpu.VMEM((EXPERT_TILE, d), F32)])
    return pl.pallas_call(
        _experts_kernel,
        grid_spec=grid_spec,
        out_shape=jax.ShapeDtypeStruct(xs.shape, F32),
        compiler_params=_cparams("arbitrary", "arbitrary"),
        name="moe_experts",
    )(te, nvalid, n_used, xs, w1, w3, w2)


def _combine_kernel(pos_ref, sel_ref, x_ref, g2_ref, y_ref, o_ref, rows, sem):
    tm, d = x_ref.shape

    def make_copy(r, k):
        p = pos_ref[0, 0, 2 * r + k]
        return pltpu.make_async_copy(y_ref.at[pl.ds(p, 1)], rows.at[k, pl.ds(r, 1)], sem)

    _row_copies(tm, make_copy)
    sel = sel_ref[...]
    w1, w2 = sel[:, 2:3], sel[:, 3:4]
    f = jnp.concatenate([w1 * rows[0, :, s, :] + w2 * rows[1, :, s, :] for s in range(d // LANE)], axis=1)
    f = (f.reshape(tm // CHUNK, CHUNK, d) * g2_ref[...]).reshape(tm, d)
    o_ref[...] = x_ref[...] + f


def _combine(y, pos, sel, x, mod_g):
    t, d = x.shape
    tm = min(ROUTE_TILE, _row_tile(t))
    return pl.pallas_call(
        _combine_kernel,
        grid=(t // tm,),
        in_specs=[pl.BlockSpec((1, 1, 2 * tm), lambda i: (i, 0, 0), memory_space=pltpu.SMEM),
                  pl.BlockSpec((tm, LANE), lambda i: (i, 0)),
                  pl.BlockSpec((tm, d), lambda i: (i, 0)),
                  _mod_spec(tm, 5),
                  pl.BlockSpec(memory_space=pl.ANY)],
        out_specs=pl.BlockSpec((tm, d), lambda i: (i, 0)),
        out_shape=jax.ShapeDtypeStruct((t, d), F32),
        scratch_shapes=[pltpu.VMEM((2, tm) + y.shape[1:], F32), pltpu.SemaphoreType.DMA(())],
        compiler_params=_cparams("arbitrary"),
        name="moe_combine",
    )(pos.reshape(t // tm, 1, 2 * tm), sel, x, mod_g, y)


def _moe(x, mod_g, g, w_r, b_r, w1, w3, w2):
    t = x.shape[0]
    h2, sel, counts = _router(x, mod_g, g, w_r, b_r)
    pos, te, nvalid, n_used = _route(sel, counts, EXPERT_TILE)
    n_rows = te.shape[0] * EXPERT_TILE
    xs = _dispatch(h2, pos, n_rows)
    y = _experts(xs, te, nvalid, n_used, w1, w3, w2)
    return _combine(y, pos, sel, x, mod_g)


def _final_kernel(x_ref, g_ref, op_ref, os_ref, *, n_prompt_tiles):
    x = x_ref[...]
    y = x * lax.rsqrt(jnp.mean(x * x, axis=-1, keepdims=True) + EPS) * g_ref[...]
    i = pl.program_id(0)

    @pl.when(i < n_prompt_tiles)
    def _():
        op_ref[...] = y

    @pl.when(i >= n_prompt_tiles)
    def _():
        os_ref[...] = y


def _final_norm(x, g, tp):
    t, d = x.shape
    tm = _row_tile(np.gcd(tp, t - tp))
    n_p = tp // tm
    return pl.pallas_call(
        functools.partial(_final_kernel, n_prompt_tiles=n_p),
        grid=(t // tm,),
        in_specs=[pl.BlockSpec((tm, d), lambda i: (i, 0)), pl.BlockSpec((1, d), lambda i: (0, 0))],
        out_specs=[pl.BlockSpec((tm, d), lambda i: (jnp.minimum(i, n_p - 1), 0)),
                   pl.BlockSpec((tm, d), lambda i: (jnp.maximum(i - n_p, 0), 0))],
        out_shape=[jax.ShapeDtypeStruct((tp, d), F32), jax.ShapeDtypeStruct((t - tp, d), F32)],
        compiler_params=_cparams("arbitrary"),
        name="final_norm",
    )(x, g.reshape(1, d))


def _split_w_in(w):
    o = np.cumsum([0, WA, WA, WA, 2 * WB_QK, WB_V, 2 * H_B, WB_V, WC_Q, WC_KV, WC_KV, 3 * D_MODEL])
    seg = lambda k: w[:, o[k]:o[k + 1]]
    qa, ka, va, qkb, vb, ifb, ob, qc, kc, vc, gates = (seg(k) for k in range(11))
    main = jnp.concatenate([qa, ka, va, qc, qkb, gates, vb, ob, kc, vc], axis=1).astype(BF16)
    return main, ifb.T.astype(BF16)


def kernel(x_prompt, x_sample, c_prompt, c_sample, cache_a_k, cache_a_v, state_b_C, state_b_n, state_b_m, state_b_conv, cache_c_k, cache_c_v, norm1_g, norm2_g, w_ada, b_ada, w_in, b_if_b, conv_w_b, conv_b_b, norm_b_g, rel_a, sink_c, w_br_a, w_br_b, w_br_c, w_o, w_ff1, w_ff3, w_ff2, w_router, b_router, w_e1, w_e3, w_e2, norm_f_g):
    nbp, sp, d = x_prompt.shape
    nbs, ss, _ = x_sample.shape
    tp, ts = nbp * sp, nbs * ss
    x = jnp.concatenate([x_prompt.reshape(tp, d), x_sample.reshape(ts, d)], axis=0)
    cond = jnp.concatenate([c_prompt, c_sample], axis=0)
    group_batch = np.concatenate([np.repeat(np.arange(nbp), sp // CHUNK), nbp + np.repeat(np.arange(nbs), ss // CHUNK)])

    lp_s = max(ss, LANE)
    l_b = 256 if sp % 256 == 0 else CHUNK
    keep_a, keep_c = min(PRE_A, sp), min(PRE_C, sp)
    zeros_state = (jnp.zeros((nbp, H_B, DK_B, DV_B), F32), jnp.zeros((nbp, H_B, DK_B), F32), jnp.zeros((nbp, H_B), F32))
    new = {k: [] for k in ("pak", "pav", "pbc", "pbn", "pbm", "pbx", "pck", "pcv",
                           "sak", "sav", "sbc", "sbn", "sbm", "sbx", "sck", "scv")}

    for l in range(DEPTH):
        mod = _ada(cond, w_ada[l], b_ada[l])
        mod_g = mod[group_batch].reshape(-1, 1, 6 * d)
        w_main, w_if_t = _split_w_in(w_in[l])
        u, g_t = _inproj(x, mod_g, norm1_g[l], w_main, w_if_t)

        bias_t = _rel_bias_table(rel_a[l])
        ya, pak, pav = _attn_a(u, bias_t, nb=nbp, s=sp, row0=0)
        ya, sak, sav = _attn_a(u, bias_t, nb=nbs, s=ss, row0=tp,
                               prefix=(cache_a_k[l].reshape(nbs, PRE_A, WA), cache_a_v[l].reshape(nbs, PRE_A, WA)),
                               out=ya)
        yc, pck, pcv = _attn_c(u, sink_c[l], nb=nbp, s=sp, row0=0, pos0=0)
        yc, sck, scv = _attn_c(u, sink_c[l], nb=nbs, s=ss, row0=tp, pos0=PAST_LEN,
                               prefix=(cache_c_k[l].reshape(nbs, PRE_C, WC_KV),
                                       cache_c_v[l].reshape(nbs, PRE_C, WC_KV)), out=yc)
        g_p = g_t[:, :tp].reshape(2 * H_B, nbp, sp).transpose(1, 0, 2)
        g_s = g_t[:, tp:].reshape(2 * H_B, nbs, ss).transpose(1, 0, 2)
        g_s = jnp.pad(g_s, ((0, 0), (0, 0), (0, lp_s - ss)))
        hist_p = jnp.zeros((nbp, 8, 2 * WB_QK), F32)
        hist_s = jnp.pad(state_b_conv[l], ((0, 0), (8 - (CONV_B - 1), 0), (0, 0)))
        bargs = (conv_w_b[l], conv_b_b[l], b_if_b[l], norm_b_g[l])
        yb, pbc, pbn, pbm, pbx = _mlstm(u, g_p, hist_p, *bargs, *zeros_state, nb=nbp, s=sp, row0=0, L=l_b)
        yb, sbc, sbn, sbm, sbx = _mlstm(u, g_s, hist_s, *bargs, state_b_C[l], state_b_n[l], state_b_m[l],
                                        nb=nbs, s=ss, row0=tp, L=ss, out=yb)

        x = _merge(ya, yb, yc, u, x, mod_g, w_br_a[l].astype(BF16), w_br_b[l].astype(BF16),
                   w_br_c[l].astype(BF16), w_o[l].astype(BF16))
        i = l // 2
        if l % 2 == 0:
            x = _ffn(x, mod_g, norm2_g[l], w_ff1[i].astype(BF16), w_ff3[i].astype(BF16), w_ff2[i].astype(BF16))
        else:
            x = _moe(x, mod_g, norm2_g[l], w_router[i], b_router[i], w_e1[i], w_e3[i], w_e2[i])

        new["pak"].append(pak.reshape(nbp, keep_a, H_A, DH_A))
        new["pav"].append(pav.reshape(nbp, keep_a, H_A, DH_A))
        new["pbc"].append(pbc)
        new["pbn"].append(pbn)
        new["pbm"].append(pbm)
        new["pbx"].append(pbx)
        new["pck"].append(pck.reshape(nbp, keep_c, HKV_C, DH_C))
        new["pcv"].append(pcv.reshape(nbp, keep_c, HKV_C, DH_C))
        new["sak"].append(sak.reshape(nbs, ss, H_A, DH_A))
        new["sav"].append(sav.reshape(nbs, ss, H_A, DH_A))
        new["sbc"].append(sbc)
        new["sbn"].append(sbn)
        new["sbm"].append(sbm)
        new["sbx"].append(sbx)
        new["sck"].append(sck.reshape(nbs, ss, HKV_C, DH_C))
        new["scv"].append(scv.reshape(nbs, ss, HKV_C, DH_C))

    y_p, y_s = _final_norm(x, norm_f_g, tp)
    st = {k: jnp.stack(v) for k, v in new.items()}
    return (y_p.reshape(nbp, sp, d), y_s.reshape(nbs, ss, d),
            st["pak"], st["pav"], st["pbc"], st["pbn"], st["pbm"], st["pbx"], st["pck"], st["pcv"],
            st["sak"], st["sav"], st["sbc"], st["sbn"], st["sbm"], st["sbx"], st["sck"], st["scv"])
```

```python
import functools

import numpy as np
import jax
import jax.numpy as jnp
from jax import lax
from jax.experimental import pallas as pl
from jax.experimental.pallas import tpu as pltpu

F32 = jnp.float32
BF16 = jnp.bfloat16

D_MODEL = 1024
DEPTH = 2
PAST_LEN = 2048
CHUNK = 64
H_A, DH_A, BACK_A, REL_CLIP = 8, 64, 8, 128
H_B, DK_B, DV_B, CONV_B = 4, 128, 256, 4
HQ_C, HKV_C, DH_C, BACK_C = 8, 2, 64, 2
ROT_DIM = DH_C // 4
ROPE_THETA = 500000.0
D_FF = 2816
N_EXPERTS = 8
D_FF_E = 3584
EPS = 1e-6
NEG = -1e30

WA = H_A * DH_A
WB_QK = H_B * DK_B
WB_V = H_B * DV_B
WC_Q = HQ_C * DH_C
WC_KV = HKV_C * DH_C
PRE_A = BACK_A * CHUNK
PRE_C = BACK_C * CHUNK

COL_QA, COL_KA, COL_VA, COL_QC = 0, 512, 1024, 1536
COL_QKB = 2048
COL_KC = 3072
COL_VC = 3200
N_F32 = 3328
COLB_GATES = 0
COLB_VB = 3072
COLB_OB = 4096
N_B16 = 5120
N_MAIN = N_F32 + N_B16

LANE = 128
SUBLANES = 8
VMEM_LIMIT = 48 * 1024 * 1024


def _cparams(*sem):
    return pltpu.CompilerParams(dimension_semantics=sem, vmem_limit_bytes=VMEM_LIMIT)


def _row_tile(t):
    for tm in (512, 256, 128, 64):
        if t % tm == 0:
            return tm
    raise ValueError(f"token count {t} is not a multiple of 64")


def _ln_mod(x, g, sc, sh):
    tm, d = x.shape
    y = x * lax.rsqrt(jnp.mean(x * x, axis=-1, keepdims=True) + EPS) * g
    y = y.reshape(tm // CHUNK, CHUNK, d) * (1.0 + sc) + sh
    return y.reshape(tm, d)


def _mod_spec(tm, kind):
    return pl.BlockSpec((tm // CHUNK, 1, D_MODEL), lambda i, *_, k=kind: (i, 0, k))


def _ada_kernel(c_ref, w_ref, b_ref, o_ref):
    c = c_ref[...]
    a = (c * jax.nn.sigmoid(c)).astype(BF16)
    o_ref[...] = jnp.dot(a, w_ref[...].astype(BF16), preferred_element_type=F32) + b_ref[...]


def _ada(c, w, b, layer):
    nb, d = c.shape
    n = w.shape[2]
    tn = 1536
    return pl.pallas_call(
        _ada_kernel,
        grid=(n // tn,),
        in_specs=[pl.BlockSpec((nb, d), lambda j: (0, 0)),
                  pl.BlockSpec((None, d, tn), lambda j: (layer, 0, j)),
                  pl.BlockSpec((None, 1, tn), lambda j: (layer, 0, j))],
        out_specs=pl.BlockSpec((nb, tn), lambda j: (0, j)),
        out_shape=jax.ShapeDtypeStruct((nb, n), F32),
        compiler_params=_cparams("arbitrary"),
        name="adaln",
    )(c, w, b.reshape(b.shape[0], 1, n))


def _col_chunks(n, width=768):
    return [(c, min(width, n - c)) for c in range(0, n, width)]


def _resident(shape):
    return pl.BlockSpec(shape, lambda *_: (0,) * len(shape), pipeline_mode=pl.Buffered(1))


def _inproj_kernel(x_ref, sh_ref, sc_ref, g_ref, w_ref, wif_ref, u_ref, ub_ref, gt_ref, h_scr):
    h_scr[...] = _ln_mod(x_ref[...], g_ref[...], sc_ref[...], sh_ref[...]).astype(BF16)
    gt_ref[...] = lax.dot_general(wif_ref[...], h_scr[...], (((1,), (1,)), ((), ())), preferred_element_type=F32)
    for c, w in _col_chunks(N_F32):
        u_ref[:, c:c + w] = jnp.dot(h_scr[...], w_ref[:, c:c + w], preferred_element_type=F32)
    for c, w in _col_chunks(N_B16):
        ub_ref[:, c:c + w] = jnp.dot(h_scr[...], w_ref[:, N_F32 + c:N_F32 + c + w],
                                     preferred_element_type=F32).astype(BF16)


def _inproj(x, mod_g, g, w_main, w_if_t):
    t, d = x.shape
    tm = min(256, _row_tile(t))
    return pl.pallas_call(
        _inproj_kernel,
        grid=(t // tm,),
        in_specs=[pl.BlockSpec((tm, d), lambda i: (i, 0)),
                  _mod_spec(tm, 0), _mod_spec(tm, 1),
                  _resident((1, d)), _resident((d, N_MAIN)), _resident((2 * H_B, d))],
        out_specs=[pl.BlockSpec((tm, N_F32), lambda i: (i, 0)),
                   pl.BlockSpec((tm, N_B16), lambda i: (i, 0)),
                   pl.BlockSpec((2 * H_B, tm), lambda i: (0, i))],
        out_shape=[jax.ShapeDtypeStruct((t, N_F32), F32),
                   jax.ShapeDtypeStruct((t, N_B16), BF16),
                   jax.ShapeDtypeStruct((2 * H_B, t), F32)],
        scratch_shapes=[pltpu.VMEM((tm, d), BF16)],
        compiler_params=_cparams("parallel"),
        name="inproj",
    )(x, mod_g, mod_g, g.reshape(1, d), w_main, w_if_t)


def _band_table(bq, pre, back):
    qi = np.arange(bq)[:, None] // CHUNK
    kj = np.arange(pre + bq)[None, :] // CHUNK
    return (kj >= qi) & (kj <= qi + back)


QBLK = 128
KV_CHUNK = 512


def _pad_rows(x, rows):
    return x if x.shape[0] == rows else jnp.concatenate(
        [x, jnp.zeros((rows - x.shape[0], x.shape[1]), x.dtype)], axis=0)


def _stage_keys_values(k_rows, v_ref, prefix_k, prefix_v, kpad, vt, pre, s):
    w = kpad.shape[1]
    s_pad = kpad.shape[0] - pre
    if prefix_k is None:
        kpad[0:pre, :] = jnp.zeros((pre, w), BF16)
        vt[:, 0:pre] = jnp.zeros((w, pre), BF16)
    else:
        kpad[0:pre, :] = prefix_k.astype(BF16)
        vt[:, 0:pre] = prefix_v.T.astype(BF16)
    for c in range(0, s_pad, KV_CHUNK):
        rows = min(KV_CHUNK, s_pad - c)
        real = max(0, min(rows, s - c))
        kpad[pre + c:pre + c + rows, :] = _pad_rows(k_rows(c, real), rows).astype(BF16)
        vt[:, pre + c:pre + c + rows] = _pad_rows(v_ref[c:c + real, :], rows).T.astype(BF16)


def _pair_store(o_ref, ots, lane0, rows):
    pair = jnp.concatenate(ots, axis=0).T
    o_ref[:, lane0:lane0 + pair.shape[1]] = pair[0:rows, :]


def _attn_a_kernel(*refs, s, has_prefix):
    if has_prefix:
        q_ref, k_ref, v_ref, pk_ref, pv_ref, bias_ref, o_ref, kt_ref, vt_ref, kpad, vt = refs
    else:
        q_ref, k_ref, v_ref, bias_ref, o_ref, kt_ref, vt_ref, kpad, vt = refs
    n = pl.program_id(1)
    kw = PRE_A + QBLK
    keep = kt_ref.shape[1]
    sq = q_ref.shape[0]

    @pl.when(n == 0)
    def _():
        _stage_keys_values(lambda c, rows: k_ref[c:c + rows, :], v_ref,
                           pk_ref[0] if has_prefix else None, pv_ref[0] if has_prefix else None,
                           kpad, vt, PRE_A, s)
        kt_ref[0] = k_ref[s - keep:s, :]
        vt_ref[0] = v_ref[s - keep:s, :]

    r0 = pl.multiple_of(n * QBLK, QBLK)
    q = _pad_rows((q_ref[...] * (DH_A ** -0.5)).astype(BF16), QBLK)
    heads = [slice(h * DH_A, (h + 1) * DH_A) for h in range(H_A)]
    nt = (((1,), (1,)), ((), ()))

    def attend(mask_keys):
        scores = [lax.dot_general(kpad[pl.ds(r0, kw), hs], q[:, hs], nt, preferred_element_type=F32)
                  for hs in heads]
        if mask_keys:
            key = lax.broadcasted_iota(jnp.int32, (kw, QBLK), 0)
            invalid = jnp.where(key >= PRE_A - r0, 0.0, NEG).astype(F32)
        probs, sums = [], []
        for h in range(H_A):
            sc = scores[h] + bias_ref[h]
            if mask_keys:
                sc = sc + invalid
            e = jnp.exp(sc - jnp.max(sc, axis=0, keepdims=True))
            sums.append(jnp.sum(e, axis=0, keepdims=True))
            probs.append(e.astype(BF16))
        for h in range(0, H_A, 2):
            ots = [jnp.dot(vt[heads[i], pl.ds(r0, kw)], probs[i], preferred_element_type=F32) / sums[i]
                   for i in (h, h + 1)]
            _pair_store(o_ref, ots, h * DH_A, sq)

    if has_prefix:
        attend(False)
    else:
        pl.when(r0 < PRE_A)(lambda: attend(True))
        pl.when(r0 >= PRE_A)(lambda: attend(False))


def _rel_bias_table(rel):
    kw = PRE_A + QBLK
    n = QBLK + kw
    m = np.arange(n)
    diag = rel.astype(F32)[np.clip(m + 1 - QBLK, -REL_CLIP, REL_CLIP) + REL_CLIP].T
    skew = jnp.tile(diag, (1, kw + 1))[:, :kw * (n - 1)].reshape(H_A, kw, n - 1)
    table = skew[:, :, kw - 1:kw - 1 + QBLK]
    return jnp.where(_band_table(QBLK, PRE_A, BACK_A).T[None], table, NEG)


def _attn_a(u, bias_t, *, nb, s, row0, prefix=None, out=None):
    t = u.shape[0]
    bq = min(QBLK, s)
    nq = s // bq
    s_pad = max(s, QBLK)
    kw = PRE_A + QBLK
    keep = min(PRE_A, s)
    rb, sb = row0 // bq, row0 // s
    in_specs = [pl.BlockSpec((bq, WA), lambda b, n: (rb + b * nq + n, COL_QA // WA)),
                pl.BlockSpec((s, WA), lambda b, n: (sb + b, COL_KA // WA)),
                pl.BlockSpec((s, WA), lambda b, n: (sb + b, COL_VA // WA))]
    args = [u, u, u]
    if prefix is not None:
        in_specs += [pl.BlockSpec((1, PRE_A, WA), lambda b, n: (b, 0, 0))] * 2
        args += [prefix[0], prefix[1]]
    in_specs.append(pl.BlockSpec((H_A, kw, QBLK), lambda b, n: (0, 0, 0)))
    args.append(bias_t)
    aliases = {}
    if out is not None:
        in_specs.append(pl.BlockSpec(memory_space=pl.ANY))
        args.append(out)
        aliases = {len(args) - 1: 0}
    body = functools.partial(_attn_a_kernel, s=s, has_prefix=prefix is not None)
    if out is not None:
        body = _drop_last_input(body, len(args))
    return pl.pallas_call(
        body,
        grid=(nb, nq),
        in_specs=in_specs,
        out_specs=[pl.BlockSpec((bq, WA), lambda b, n: (rb + b * nq + n, 0)),
                   pl.BlockSpec((1, keep, WA), lambda b, n: (b, 0, 0)),
                   pl.BlockSpec((1, keep, WA), lambda b, n: (b, 0, 0))],
        out_shape=[jax.ShapeDtypeStruct((t, WA), F32),
                   jax.ShapeDtypeStruct((nb, keep, WA), F32),
                   jax.ShapeDtypeStruct((nb, keep, WA), F32)],
        scratch_shapes=[pltpu.VMEM((PRE_A + s_pad, WA), BF16), pltpu.VMEM((WA, PRE_A + s_pad), BF16)],
        input_output_aliases=aliases,
        compiler_params=_cparams("parallel", "arbitrary"),
        name="mixer_a_sample" if prefix is not None else "mixer_a_prompt",
    )(*args)


def _drop_last_input(body, n_in):
    def wrapped(*refs):
        return body(*refs[:n_in - 1], *refs[n_in:])
    return wrapped


def _rope(x, c, s1, s2):
    w = x.shape[1]
    return x * c + pltpu.roll(x, 8, 1) * s1 + pltpu.roll(x, w - 8, 1) * s2


def _attn_c_kernel(*refs, s, has_prefix):
    if has_prefix:
        (q_ref, k_ref, v_ref, pk_ref, pv_ref, rc_ref, rs1_ref, rs2_ref, band_ref, sink_ref,
         o_ref, kt_ref, vt_ref, kpad, vt) = refs
    else:
        (q_ref, k_ref, v_ref, rc_ref, rs1_ref, rs2_ref, band_ref, sink_ref,
         o_ref, kt_ref, vt_ref, kpad, vt) = refs
    n = pl.program_id(1)
    kw = PRE_C + QBLK
    keep = kt_ref.shape[1]
    sq = q_ref.shape[0]

    def rope_at(x, start, rep=1):
        tables = [jnp.concatenate([t_ref[pl.ds(start, x.shape[0]), :]] * rep, axis=1)
                  for t_ref in (rc_ref, rs1_ref, rs2_ref)]
        return _rope(x, *tables)

    @pl.when(n == 0)
    def _():
        _stage_keys_values(lambda c, rows: rope_at(k_ref[c:c + rows, :], c), v_ref,
                           pk_ref[0] if has_prefix else None, pv_ref[0] if has_prefix else None,
                           kpad, vt, PRE_C, s)
        kt_ref[0] = rope_at(k_ref[s - keep:s, :], s - keep)
        vt_ref[0] = v_ref[s - keep:s, :]

    r0 = pl.multiple_of(n * QBLK, QBLK)
    q = rope_at(q_ref[...], pl.multiple_of(n * sq, sq), rep=WC_Q // WC_KV)
    q = _pad_rows((q * (DH_C ** -0.5)).astype(BF16), QBLK)
    per = HQ_C // HKV_C
    groups = [slice(g * DH_C, (g + 1) * DH_C) for g in range(HKV_C)]
    nt = (((1,), (1,)), ((), ()))

    def attend(mask_keys):
        scores = []
        for g, gs in enumerate(groups):
            qs = jnp.concatenate([q[:, (g * per + i) * DH_C:(g * per + i + 1) * DH_C] for i in range(per)], axis=0)
            scores.append(lax.dot_general(kpad[pl.ds(r0, kw), gs], qs, nt, preferred_element_type=F32))
        mask = band_ref[...]
        if mask_keys:
            key = lax.broadcasted_iota(jnp.int32, mask.shape, 0)
            mask = mask + jnp.where(key >= PRE_C - r0, 0.0, NEG).astype(F32)
        probs, sums = [], []
        for g in range(HKV_C):
            sk = jnp.concatenate([jnp.broadcast_to(sink_ref[g * per + i:g * per + i + 1, 0:1], (1, QBLK))
                                  for i in range(per)], axis=1)
            sc = scores[g] + mask
            m = jnp.maximum(jnp.max(sc, axis=0, keepdims=True), sk)
            e = jnp.exp(sc - m)
            sums.append(jnp.sum(e, axis=0, keepdims=True) + jnp.exp(sk - m))
            probs.append(e.astype(BF16))
        for g, gs in enumerate(groups):
            ot = jnp.dot(vt[gs, pl.ds(r0, kw)], probs[g], preferred_element_type=F32) / sums[g]
            for i in range(0, per, 2):
                _pair_store(o_ref, [ot[:, (i + j) * QBLK:(i + j + 1) * QBLK] for j in (0, 1)],
                            (g * per + i) * DH_C, sq)

    if has_prefix:
        attend(False)
    else:
        pl.when(r0 < PRE_C)(lambda: attend(True))
        pl.when(r0 >= PRE_C)(lambda: attend(False))


def _rope_tables(pos):
    half = ROT_DIM // 2
    inv_freq = 1.0 / (ROPE_THETA ** (jnp.arange(half, dtype=F32) * (2.0 / ROT_DIM)))
    ang = pos.astype(F32)[:, None] * inv_freq[None, :]
    cos, sin = jnp.cos(ang), jnp.sin(ang)
    n = pos.shape[0]
    one = jnp.ones((n, DH_C - ROT_DIM), F32)
    zero = jnp.zeros((n, DH_C - ROT_DIM), F32)
    zh = jnp.zeros((n, half), F32)
    c = jnp.concatenate([cos, cos, one], axis=1)
    s1 = jnp.concatenate([zh, sin, zero], axis=1)
    s2 = jnp.concatenate([-sin, zh, zero], axis=1)
    rep = WC_KV // DH_C
    return tuple(jnp.concatenate([a] * rep, axis=1) for a in (c, s1, s2))


def _attn_c(u, sink, *, nb, s, row0, pos0, prefix=None, out=None):
    t = u.shape[0]
    bq = min(QBLK, s)
    nq = s // bq
    s_pad = max(s, QBLK)
    kw = PRE_C + QBLK
    per = HQ_C // HKV_C
    tables = _rope_tables(pos0 + jnp.arange(s))
    band = np.where(_band_table(QBLK, PRE_C, BACK_C), 0.0, NEG).astype(np.float32)
    band = jnp.asarray(np.tile(band.T, (1, per)))
    sink_t = jnp.broadcast_to(sink.astype(F32)[:, None], (HQ_C, LANE))
    rb, sb = row0 // bq, row0 // s
    in_specs = [pl.BlockSpec((bq, WC_Q), lambda b, n: (rb + b * nq + n, COL_QC // WC_Q)),
                pl.BlockSpec((s, WC_KV), lambda b, n: (sb + b, COL_KC // WC_KV)),
                pl.BlockSpec((s, WC_KV), lambda b, n: (sb + b, COL_VC // WC_KV))]
    args = [u, u, u]
    if prefix is not None:
        in_specs += [pl.BlockSpec((1, PRE_C, WC_KV), lambda b, n: (b, 0, 0))] * 2
        args += [prefix[0], prefix[1]]
    in_specs += [pl.BlockSpec((s, WC_KV), lambda b, n: (0, 0))] * 3
    args += list(tables)
    in_specs += [pl.BlockSpec((kw, per * QBLK), lambda b, n: (0, 0)),
                 pl.BlockSpec((HQ_C, LANE), lambda b, n: (0, 0))]
    args += [band, sink_t]
    aliases = {}
    body = functools.partial(_attn_c_kernel, s=s, has_prefix=prefix is not None)
    if out is not None:
        in_specs.append(pl.BlockSpec(memory_space=pl.ANY))
        args.append(out)
        aliases = {len(args) - 1: 0}
        body = _drop_last_input(body, len(args))
    keep = min(PRE_C, s)
    return pl.pallas_call(
        body,
        grid=(nb, nq),
        in_specs=in_specs,
        out_specs=[pl.BlockSpec((bq, WC_Q), lambda b, n: (rb + b * nq + n, 0)),
                   pl.BlockSpec((1, keep, WC_KV), lambda b, n: (b, 0, 0)),
                   pl.BlockSpec((1, keep, WC_KV), lambda b, n: (b, 0, 0))],
        out_shape=[jax.ShapeDtypeStruct((t, WC_Q), F32),
                   jax.ShapeDtypeStruct((nb, keep, WC_KV), F32),
                   jax.ShapeDtypeStruct((nb, keep, WC_KV), F32)],
        scratch_shapes=[pltpu.VMEM((PRE_C + s_pad, WC_KV), BF16), pltpu.VMEM((WC_KV, PRE_C + s_pad), BF16)],
        input_output_aliases=aliases,
        compiler_params=_cparams("parallel", "arbitrary"),
        name="mixer_c_sample" if prefix is not None else "mixer_c_prompt",
    )(*args)


def _log_sigmoid(x):
    return jnp.minimum(x, 0.0) - jnp.log(1.0 + jnp.exp(-jnp.abs(x)))


def _lane_cumsum(x):
    n = x.shape[1]
    col = lax.broadcasted_iota(jnp.int32, x.shape, 1)
    k = 1
    while k < n:
        x = x + jnp.where(col >= k, pltpu.roll(x, k, 1), 0.0)
        k *= 2
    return x


def _mlstm_kernel(qk_ref, v_ref, og_ref, g_ref, hist_ref, cw_ref, cb_ref, bif_ref, ng_ref,
                  c0_ref, n0_ref, m0_ref, y_ref, cf_ref, nf_ref, mf_ref, xt_ref,
                  xpad, c_s, n_s, m_s, *, L):
    c = pl.program_id(1)

    @pl.when(c == 0)
    def _():
        c_s[...] = c0_ref[0]
        n_s[...] = n0_ref[0]
        m_s[...] = m0_ref[0]
        xpad[0:8, :] = hist_ref[0]

    x = qk_ref[...]
    xpad[8:8 + L, :] = x
    conv = (cb_ref[...] + x * cw_ref[3:4, :] + xpad[7:7 + L, :] * cw_ref[2:3, :]
            + xpad[6:6 + L, :] * cw_ref[1:2, :] + xpad[5:5 + L, :] * cw_ref[0:1, :])
    xpad[0:8, :] = xpad[L:L + 8, :]
    act = conv * jax.nn.sigmoid(conv)

    gb = g_ref[...] + bif_ref[:, 0:1]
    b_all = _lane_cumsum(_log_sigmoid(gb))
    ig_rows = gb[0:H_B, :]
    b_rows = b_all[H_B:2 * H_B, :]
    both = jnp.where(lax.broadcasted_iota(jnp.int32, gb.shape, 0) < H_B, gb, b_all)
    cols = jnp.concatenate([both, jnp.zeros((LANE - 2 * H_B, gb.shape[1]), F32)], axis=0).T

    row = lax.broadcasted_iota(jnp.int32, (L, L), 0)
    colm = lax.broadcasted_iota(jnp.int32, (L, L), 1)
    causal = colm <= row

    heads = range(H_B)
    nt = (((1,), (1,)), ((), ()))
    qf = [act[:, h * DK_B:(h + 1) * DK_B] for h in heads]
    kf = [act[:, WB_QK + h * DK_B:WB_QK + (h + 1) * DK_B] * (DK_B ** -0.5) for h in heads]
    qb = [q.astype(BF16) for q in qf]
    vb = [v_ref[:, h * DV_B:(h + 1) * DV_B].astype(BF16) for h in heads]
    cmat = [c_s[h] for h in heads]
    nvec = [n_s[h] for h in heads]
    qk = [lax.dot_general(qb[h], kf[h].astype(BF16), nt, preferred_element_type=F32) for h in heads]
    qc = [jnp.dot(qb[h], cmat[h].astype(BF16), preferred_element_type=F32) for h in heads]

    a, wi, mt, bc, igc = [], [], [], [], []
    for h in heads:
        br = b_rows[h:h + 1, 0:L]
        igr = ig_rows[h:h + 1, 0:L]
        bc.append(cols[0:L, H_B + h:H_B + h + 1])
        igc.append(cols[0:L, h:h + 1])
        logd = jnp.where(causal, bc[h] + (igr - br), NEG)
        li = bc[h] + m_s[h][:, 0:1]
        mt.append(jnp.maximum(li, jnp.max(logd, axis=1, keepdims=True)))
        wi.append(jnp.exp(li - mt[h]))
        a.append(qk[h] * jnp.exp(logd - mt[h]))

    pv = [jnp.dot(a[h].astype(BF16), vb[h], preferred_element_type=F32) for h in heads]
    for h in heads:
        num = pv[h] + wi[h] * qc[h]
        den = jnp.sum(a[h], axis=1, keepdims=True) + wi[h] * jnp.sum(qf[h] * nvec[h], axis=1, keepdims=True)
        hh = num / jnp.maximum(jnp.abs(den), jnp.exp(-mt[h]))
        hn = hh * lax.rsqrt(jnp.mean(hh * hh, axis=-1, keepdims=True) + EPS)
        vs = slice(h * DV_B, (h + 1) * DV_B)
        y_ref[:, vs] = hn * ng_ref[:, vs] * jax.nn.sigmoid(og_ref[:, vs].astype(F32))

    for h in heads:
        bl, ml, wprev = bc[h][L - 1:L, :], mt[h][L - 1:L, :], wi[h][L - 1:L, :]
        kwt = kf[h] * jnp.exp(bl + igc[h] - bc[h] - ml)
        c_s[h] = wprev * cmat[h] + lax.dot_general(kwt.astype(BF16), vb[h], (((0,), (0,)), ((), ())),
                                                   preferred_element_type=F32)
        n_s[h] = wprev * nvec[h] + jnp.sum(kwt, axis=0, keepdims=True)
        m_s[h] = jnp.broadcast_to(ml, (1, LANE))

    @pl.when(c == pl.num_programs(1) - 1)
    def _():
        cf_ref[0] = c_s[...]
        nf_ref[0] = n_s[...]
        mf_ref[0] = m_s[...]
        xt_ref[0] = xpad[0:8, :]


def _mlstm(u, ub, g_rows, hist, cw, cb, bif, ng, c0, n0, m0, *, nb, s, row0, L, out=None):
    t = u.shape[0]
    nc = s // L
    lp = g_rows.shape[2] // nc
    rb = row0 // L
    n0 = n0.reshape(nb, H_B, 1, DK_B)
    m0 = jnp.broadcast_to(m0.reshape(nb, H_B, 1, 1), (nb, H_B, 1, LANE))
    bif_t = jnp.broadcast_to(bif.astype(F32)[:, None], (2 * H_B, LANE))
    row_spec = lambda col: pl.BlockSpec((L, WB_V), lambda b, c: (rb + b * nc + c, col))
    full = lambda shape: pl.BlockSpec(shape, lambda b, c: (0,) * len(shape))
    state = lambda shape: pl.BlockSpec((1,) + shape, lambda b, c: (b,) + (0,) * len(shape))
    in_specs = [row_spec(COL_QKB // WB_V), row_spec(COLB_VB // WB_V), row_spec(COLB_OB // WB_V),
                pl.BlockSpec((None, 2 * H_B, lp), lambda b, c: (b, 0, c)),
                state((8, 2 * WB_QK)),
                full((CONV_B, 2 * WB_QK)), full((1, 2 * WB_QK)), full((2 * H_B, LANE)), full((1, WB_V)),
                state((H_B, DK_B, DV_B)), state((H_B, 1, DK_B)), state((H_B, 1, LANE))]
    args = [u, ub, ub, g_rows, hist, cw, cb.reshape(1, -1), bif_t, ng.reshape(1, -1), c0, n0, m0]
    n_real = len(args)
    aliases = {}
    body = functools.partial(_mlstm_kernel, L=L)
    if out is not None:
        in_specs.append(pl.BlockSpec(memory_space=pl.ANY))
        args.append(out)
        aliases = {n_real: 0}
        body = _drop_last_input(body, len(args))
    y, cf, nf, mf, xt = pl.pallas_call(
        body,
        grid=(nb, nc),
        in_specs=in_specs,
        out_specs=[pl.BlockSpec((L, WB_V), lambda b, c: (rb + b * nc + c, 0)),
                   state((H_B, DK_B, DV_B)), state((H_B, 1, DK_B)), state((H_B, 1, LANE)),
                   state((8, 2 * WB_QK))],
        out_shape=[jax.ShapeDtypeStruct((t, WB_V), F32),
                   jax.ShapeDtypeStruct((nb, H_B, DK_B, DV_B), F32),
                   jax.ShapeDtypeStruct((nb, H_B, 1, DK_B), F32),
                   jax.ShapeDtypeStruct((nb, H_B, 1, LANE), F32),
                   jax.ShapeDtypeStruct((nb, 8, 2 * WB_QK), F32)],
        scratch_shapes=[pltpu.VMEM((L + 8, 2 * WB_QK), F32),
                        pltpu.VMEM((H_B, DK_B, DV_B), F32),
                        pltpu.VMEM((H_B, 1, DK_B), F32),
                        pltpu.VMEM((H_B, 1, LANE), F32)],
        input_output_aliases=aliases,
        compiler_params=_cparams("parallel", "arbitrary"),
        name="mlstm_sample" if out is not None else "mlstm_prompt",
    )(*args)
    return y, cf, nf.reshape(nb, H_B, DK_B), mf[:, :, 0, 0], xt[:, 8 - (CONV_B - 1):, :]


def _merge_kernel(ya_ref, yb_ref, yc_ref, gt_ref, x_ref, g1_ref, wa_ref, wb_ref, wc_ref, wo_ref, o_ref):
    tm, d = x_ref.shape
    sg = jax.nn.sigmoid(gt_ref[...].astype(F32))
    m = (sg[:, 0:d] * jnp.dot(ya_ref[...].astype(BF16), wa_ref[...], preferred_element_type=F32)
         + sg[:, d:2 * d] * jnp.dot(yb_ref[...].astype(BF16), wb_ref[...], preferred_element_type=F32)
         + sg[:, 2 * d:3 * d] * jnp.dot(yc_ref[...].astype(BF16), wc_ref[...], preferred_element_type=F32))
    o = jnp.dot(m.astype(BF16), wo_ref[...], preferred_element_type=F32)
    o = (o.reshape(tm // CHUNK, CHUNK, d) * g1_ref[...]).reshape(tm, d)
    o_ref[...] = x_ref[...] + o


def _merge(ya, yb, yc, ub, x, mod_g, wa, wb, wc, wo):
    t, d = x.shape
    tm = _row_tile(t)
    row = lambda w, col=0: pl.BlockSpec((tm, w), lambda i: (i, col))
    full = lambda a: _resident(a.shape)
    return pl.pallas_call(
        _merge_kernel,
        grid=(t // tm,),
        in_specs=[row(WA), row(WB_V), row(WC_Q), row(3 * d, COLB_GATES // (3 * d)), row(d),
                  _mod_spec(tm, 2), full(wa), full(wb), full(wc), full(wo)],
        out_specs=row(d),
        out_shape=jax.ShapeDtypeStruct((t, d), F32),
        compiler_params=_cparams("parallel"),
        name="merge",
    )(ya, yb, yc, ub, x, mod_g, wa, wb, wc, wo)


def _ffn_kernel(x_ref, sh_ref, sc_ref, g_ref, g2_ref, w1_ref, w3_ref, w2_ref, o_ref, h_scr):
    tm, d = x_ref.shape
    h_scr[...] = _ln_mod(x_ref[...], g_ref[...], sc_ref[...], sh_ref[...]).astype(BF16)
    acc = None
    for c, w in _col_chunks(w1_ref.shape[1]):
        a = jnp.dot(h_scr[...], w1_ref[:, c:c + w], preferred_element_type=F32)
        b = jnp.dot(h_scr[...], w3_ref[:, c:c + w], preferred_element_type=F32)
        tt = (a * jax.nn.sigmoid(a) * b).astype(BF16)
        p = jnp.dot(tt, w2_ref[c:c + w, :], preferred_element_type=F32)
        acc = p if acc is None else acc + p
    f = (acc.reshape(tm // CHUNK, CHUNK, d) * g2_ref[...]).reshape(tm, d)
    o_ref[...] = x_ref[...] + f


def _ffn(x, mod_g, g, w1, w3, w2):
    t, d = x.shape
    tm = _row_tile(t)
    ff = w1.shape[1]
    return pl.pallas_call(
        _ffn_kernel,
        grid=(t // tm,),
        in_specs=[pl.BlockSpec((tm, d), lambda i: (i, 0)),
                  _mod_spec(tm, 3), _mod_spec(tm, 4),
                  _resident((1, d)),
                  _mod_spec(tm, 5),
                  _resident((d, ff)), _resident((d, ff)), _resident((ff, d))],
        out_specs=pl.BlockSpec((tm, d), lambda i: (i, 0)),
        out_shape=jax.ShapeDtypeStruct((t, d), F32),
        scratch_shapes=[pltpu.VMEM((tm, d), BF16)],
        compiler_params=_cparams("parallel"),
        name="ffn_dense",
    )(x, mod_g, mod_g, g.reshape(1, d), mod_g, w1, w3, w2)


def _router_kernel(x_ref, sh_ref, sc_ref, g_ref, wr_ref, br_ref, h_ref, sel_ref):
    h = _ln_mod(x_ref[...], g_ref[...], sc_ref[...], sh_ref[...])
    h_ref[...] = h.reshape(h_ref.shape)
    lg = jnp.dot(h.astype(BF16), wr_ref[...], preferred_element_type=F32) + br_ref[...]
    lane = lax.broadcasted_iota(jnp.int32, lg.shape, 1).astype(F32)
    m1 = jnp.max(lg, axis=1, keepdims=True)
    i1 = jnp.min(jnp.where(lg == m1, lane, float(LANE)), axis=1, keepdims=True)
    lg2 = jnp.where(lane == i1, 2.0 * NEG, lg)
    m2 = jnp.max(lg2, axis=1, keepdims=True)
    i2 = jnp.min(jnp.where(lg2 == m2, lane, float(LANE)), axis=1, keepdims=True)
    e2 = jnp.exp(m2 - m1)
    w1 = 1.0 / (1.0 + e2)
    sel_ref[...] = (jnp.where(lane == 0.0, i1, 0.0) + jnp.where(lane == 1.0, i2, 0.0)
                    + jnp.where(lane == 2.0, w1, 0.0) + jnp.where(lane == 3.0, e2 * w1, 0.0))


def _router(x, mod_g, g, w_r, b_r):
    t, d = x.shape
    tm = _row_tile(t)
    wr = jnp.zeros((d, LANE), BF16).at[:, :N_EXPERTS].set(w_r.astype(BF16))
    br = jnp.full((1, LANE), NEG, F32).at[0, :N_EXPERTS].set(b_r.astype(F32))
    return pl.pallas_call(
        _router_kernel,
        grid=(t // tm,),
        in_specs=[pl.BlockSpec((tm, d), lambda i: (i, 0)),
                  _mod_spec(tm, 3), _mod_spec(tm, 4),
                  pl.BlockSpec((1, d), lambda i: (0, 0)),
                  pl.BlockSpec((d, LANE), lambda i: (0, 0)),
                  pl.BlockSpec((1, LANE), lambda i: (0, 0))],
        out_specs=[pl.BlockSpec((tm // SUBLANES, SUBLANES, d), lambda i: (i, 0, 0)),
                   pl.BlockSpec((tm, LANE), lambda i: (i, 0))],
        out_shape=[jax.ShapeDtypeStruct((t // SUBLANES, SUBLANES, d), F32), jax.ShapeDtypeStruct((t, LANE), F32)],
        compiler_params=_cparams("parallel"),
        name="router",
    )(x, mod_g, mod_g, g.reshape(1, d), wr, br)


def _route(sel, tmr):
    t = sel.shape[0]
    i32 = jnp.int32
    experts = jnp.arange(N_EXPERTS, dtype=i32)[None, :]
    e = sel[:, 0:2].astype(i32).reshape(-1)
    oh = (e[:, None] == experts).astype(i32)
    csum = jnp.cumsum(oh, axis=0)
    cnt = csum[-1]
    rank = jnp.sum(csum * oh, axis=1) - 1
    ntile_e = (cnt + tmr - 1) // tmr
    tile_end = jnp.cumsum(ntile_e)
    tile_start = tile_end - ntile_e
    pos = jnp.sum(oh * (tile_start * tmr)[None, :], axis=1) + rank
    n_tiles = -(-2 * t // tmr) + N_EXPERTS
    n_used = tile_end[-1]
    tiles = jnp.arange(n_tiles, dtype=i32)
    tc = jnp.minimum(tiles, n_used - 1)
    te = jnp.sum((tc[:, None] >= tile_end[None, :]).astype(i32), axis=1)
    ohe = (te[:, None] == experts).astype(i32)
    nvalid = jnp.sum(ohe * cnt[None, :], axis=1) - (tc - jnp.sum(ohe * tile_start[None, :], axis=1)) * tmr
    nvalid = jnp.where(tiles < n_used, jnp.clip(nvalid, 0, tmr), 0)
    return pos.astype(i32), te.astype(i32), nvalid.astype(i32), n_used.reshape(1).astype(i32)


EXPERT_TILE = 1024
EXPERT_SUB = 256
ROUTE_TILE = 256


def _row_copies(n_groups, make_copy):
    def issue(g, carry):
        for sub in range(SUBLANES):
            for k in range(2):
                make_copy(g, sub, k).start(priority=k)
        return carry

    def drain(g, carry):
        for sub in range(SUBLANES):
            for k in range(2):
                make_copy(g, sub, k).wait()
        return carry

    lax.fori_loop(0, n_groups, issue, 0)
    lax.fori_loop(0, n_groups, drain, 0)


def _slot_row(hi_ref, lo_ref, g, sub, k):
    idx = 2 * SUBLANES * g + 2 * sub + k
    return hi_ref[0, 0, idx], lo_ref[0, 0, idx]


def _split_pos(pos, t, tm):
    hi = lax.shift_right_logical(pos, 3).reshape(t // tm, 1, 2 * tm)
    lo = jnp.bitwise_and(pos, SUBLANES - 1).reshape(t // tm, 1, 2 * tm)
    return hi, lo


def _dispatch_kernel(hi_ref, lo_ref, h_ref, xs_in_ref, xs_ref, sem):
    del xs_in_ref

    def make_copy(g, sub, k):
        hi, lo = _slot_row(hi_ref, lo_ref, g, sub, k)
        return pltpu.make_async_copy(h_ref.at[g, pl.ds(sub, 1), :], xs_ref.at[hi, pl.ds(lo, 1), :], sem)

    _row_copies(h_ref.shape[0], make_copy)


def _dispatch(h, pos, n_rows):
    t, d = h.shape[0] * SUBLANES, h.shape[2]
    tm = min(ROUTE_TILE, _row_tile(t))
    xs0 = jnp.zeros((n_rows // SUBLANES, SUBLANES, d), F32)
    smem = pl.BlockSpec((1, 1, 2 * tm), lambda i: (i, 0, 0), memory_space=pltpu.SMEM)
    return pl.pallas_call(
        _dispatch_kernel,
        grid=(t // tm,),
        in_specs=[smem, smem,
                  pl.BlockSpec((tm // SUBLANES, SUBLANES, d), lambda i: (i, 0, 0)),
                  pl.BlockSpec(memory_space=pl.ANY)],
        out_specs=pl.BlockSpec(memory_space=pl.ANY),
        out_shape=jax.ShapeDtypeStruct(xs0.shape, F32),
        scratch_shapes=[pltpu.SemaphoreType.DMA(())],
        input_output_aliases={3: 0},
        compiler_params=_cparams("arbitrary"),
        name="moe_dispatch",
    )(*_split_pos(pos, t, tm), h, xs0)


def _experts_kernel(te_ref, nv_ref, nu_ref, x_ref, w1_ref, w3_ref, w2_ref, y_ref, xb, w1b, w3b, w2b, acc):
    del te_ref, nu_ref
    i, j = pl.program_id(0), pl.program_id(1)
    nv = nv_ref[i]

    @pl.when(nv > 0)
    def _():
        @pl.when(j == 0)
        def _():
            xb[...] = x_ref[...].astype(BF16)
            acc[...] = jnp.zeros_like(acc)

        w1b[...] = w1_ref[0].astype(BF16)
        w3b[...] = w3_ref[0].astype(BF16)
        w2b[...] = w2_ref[0].astype(BF16)

        for m in range(EXPERT_SUB, EXPERT_TILE + 1, EXPERT_SUB):
            @pl.when((nv > m - EXPERT_SUB) & (nv <= m))
            def _(m=m):
                xs = xb[0:m, :]
                a = jnp.dot(xs, w1b[...], preferred_element_type=F32)
                b = jnp.dot(xs, w3b[...], preferred_element_type=F32)
                tt = (a * jax.nn.sigmoid(a) * b).astype(BF16)
                acc[0:m, :] += jnp.dot(tt, w2b[...], preferred_element_type=F32)

        @pl.when(j == pl.num_programs(1) - 1)
        def _():
            y_ref[...] = acc[...]


def _experts(xs, te, nvalid, n_used, w1, w3, w2):
    n_rows, d = xs.shape
    ne, _, ff = w1.shape
    tf = 512
    nj = ff // tf
    n_tiles = n_rows // EXPERT_TILE
    tile_rows = (EXPERT_TILE, d)

    def row_map(i, j, te_ref, nv_ref, nu_ref):
        return (jnp.minimum(i, nu_ref[0] - 1), 0)

    def col(i, j, nu_ref):
        return jnp.where(i < nu_ref[0], j, nj - 1)

    grid_spec = pltpu.PrefetchScalarGridSpec(
        num_scalar_prefetch=3,
        grid=(n_tiles, nj),
        in_specs=[pl.BlockSpec(tile_rows, row_map),
                  pl.BlockSpec((1, d, tf), lambda i, j, te_ref, nv_ref, nu_ref: (te_ref[i], 0, col(i, j, nu_ref))),
                  pl.BlockSpec((1, d, tf), lambda i, j, te_ref, nv_ref, nu_ref: (te_ref[i], 0, col(i, j, nu_ref))),
                  pl.BlockSpec((1, tf, d), lambda i, j, te_ref, nv_ref, nu_ref: (te_ref[i], col(i, j, nu_ref), 0))],
        out_specs=pl.BlockSpec(tile_rows, row_map),
        scratch_shapes=[pltpu.VMEM((EXPERT_TILE, d), BF16), pltpu.VMEM((d, tf), BF16), pltpu.VMEM((d, tf), BF16),
                        pltpu.VMEM((tf, d), BF16), pltpu.VMEM((EXPERT_TILE, d), F32)])
    return pl.pallas_call(
        _experts_kernel,
        grid_spec=grid_spec,
        out_shape=jax.ShapeDtypeStruct(xs.shape, F32),
        compiler_params=_cparams("arbitrary", "arbitrary"),
        name="moe_experts",
    )(te, nvalid, n_used, xs, w1, w3, w2)


def _combine_kernel(hi_ref, lo_ref, sel_ref, x_ref, g2_ref, y_ref, o_ref, rows, sem):
    tm, d = x_ref.shape

    def make_copy(g, sub, k):
        hi, lo = _slot_row(hi_ref, lo_ref, g, sub, k)
        return pltpu.make_async_copy(y_ref.at[hi, pl.ds(lo, 1), :], rows.at[k, g, pl.ds(sub, 1), :], sem)

    _row_copies(tm // SUBLANES, make_copy)
    sel = sel_ref[...]
    f = sel[:, 2:3] * rows[0].reshape(tm, d) + sel[:, 3:4] * rows[1].reshape(tm, d)
    f = (f.reshape(tm // CHUNK, CHUNK, d) * g2_ref[...]).reshape(tm, d)
    o_ref[...] = x_ref[...] + f


def _combine(y, pos, sel, x, mod_g):
    t, d = x.shape
    tm = min(ROUTE_TILE, _row_tile(t))
    smem = pl.BlockSpec((1, 1, 2 * tm), lambda i: (i, 0, 0), memory_space=pltpu.SMEM)
    return pl.pallas_call(
        _combine_kernel,
        grid=(t // tm,),
        in_specs=[smem, smem,
                  pl.BlockSpec((tm, LANE), lambda i: (i, 0)),
                  pl.BlockSpec((tm, d), lambda i: (i, 0)),
                  _mod_spec(tm, 5),
                  pl.BlockSpec(memory_space=pl.ANY)],
        out_specs=pl.BlockSpec((tm, d), lambda i: (i, 0)),
        out_shape=jax.ShapeDtypeStruct((t, d), F32),
        scratch_shapes=[pltpu.VMEM((2, tm // SUBLANES, SUBLANES, d), F32), pltpu.SemaphoreType.DMA(())],
        compiler_params=_cparams("arbitrary"),
        name="moe_combine",
    )(*_split_pos(pos, t, tm), sel, x, mod_g, y)


def _moe(x, mod_g, g, w_r, b_r, w1, w3, w2):
    d = x.shape[1]
    h2, sel = _router(x, mod_g, g, w_r, b_r)
    pos, te, nvalid, n_used = _route(sel, EXPERT_TILE)
    n_rows = te.shape[0] * EXPERT_TILE
    xs = _dispatch(h2, pos, n_rows)
    y = _experts(xs.reshape(n_rows, d), te, nvalid, n_used, w1, w3, w2)
    return _combine(y.reshape(n_rows // SUBLANES, SUBLANES, d), pos, sel, x, mod_g)


def _final_kernel(x_ref, g_ref, op_ref, os_ref, *, n_prompt_tiles):
    x = x_ref[...]
    y = x * lax.rsqrt(jnp.mean(x * x, axis=-1, keepdims=True) + EPS) * g_ref[...]
    i = pl.program_id(0)

    @pl.when(i < n_prompt_tiles)
    def _():
        op_ref[...] = y

    @pl.when(i >= n_prompt_tiles)
    def _():
        os_ref[...] = y


def _final_norm(x, g, tp):
    t, d = x.shape
    tm = _row_tile(np.gcd(tp, t - tp))
    n_p = tp // tm
    return pl.pallas_call(
        functools.partial(_final_kernel, n_prompt_tiles=n_p),
        grid=(t // tm,),
        in_specs=[pl.BlockSpec((tm, d), lambda i: (i, 0)), pl.BlockSpec((1, d), lambda i: (0, 0))],
        out_specs=[pl.BlockSpec((tm, d), lambda i: (jnp.minimum(i, n_p - 1), 0)),
                   pl.BlockSpec((tm, d), lambda i: (jnp.maximum(i - n_p, 0), 0))],
        out_shape=[jax.ShapeDtypeStruct((tp, d), F32), jax.ShapeDtypeStruct((t - tp, d), F32)],
        compiler_params=_cparams("arbitrary"),
        name="final_norm",
    )(x, g.reshape(1, d))


def _split_w_in(w):
    o = np.cumsum([0, WA, WA, WA, 2 * WB_QK, WB_V, 2 * H_B, WB_V, WC_Q, WC_KV, WC_KV, 3 * D_MODEL])
    seg = lambda k: w[:, o[k]:o[k + 1]]
    qa, ka, va, qkb, vb, ifb, ob, qc, kc, vc, gates = (seg(k) for k in range(11))
    main = jnp.concatenate([qa, ka, va, qc, qkb, kc, vc, gates, vb, ob], axis=1).astype(BF16)
    return main, ifb.T.astype(BF16)


def kernel(x_prompt, x_sample, c_prompt, c_sample, cache_a_k, cache_a_v, state_b_C, state_b_n, state_b_m, state_b_conv, cache_c_k, cache_c_v, norm1_g, norm2_g, w_ada, b_ada, w_in, b_if_b, conv_w_b, conv_b_b, norm_b_g, rel_a, sink_c, w_br_a, w_br_b, w_br_c, w_o, w_ff1, w_ff3, w_ff2, w_router, b_router, w_e1, w_e3, w_e2, norm_f_g):
    nbp, sp, d = x_prompt.shape
    nbs, ss, _ = x_sample.shape
    tp, ts = nbp * sp, nbs * ss
    x = jnp.concatenate([x_prompt.reshape(tp, d), x_sample.reshape(ts, d)], axis=0)
    cond = jnp.concatenate([c_prompt, c_sample], axis=0)
    group_batch = np.concatenate([np.repeat(np.arange(nbp), sp // CHUNK), nbp + np.repeat(np.arange(nbs), ss // CHUNK)])

    lp_s = max(ss, LANE)
    l_b = 256 if sp % 256 == 0 else CHUNK
    keep_a, keep_c = min(PRE_A, sp), min(PRE_C, sp)
    zeros_state = (jnp.zeros((nbp, H_B, DK_B, DV_B), F32), jnp.zeros((nbp, H_B, DK_B), F32), jnp.zeros((nbp, H_B), F32))
    new = {k: [] for k in ("pak", "pav", "pbc", "pbn", "pbm", "pbx", "pck", "pcv",
                           "sak", "sav", "sbc", "sbn", "sbm", "sbx", "sck", "scv")}

    for l in range(DEPTH):
        mod = _ada(cond, w_ada, b_ada, l)
        mod_g = mod[group_batch].reshape(-1, 1, 6 * d)
        w_main, w_if_t = _split_w_in(w_in[l])
        u, ub, g_t = _inproj(x, mod_g, norm1_g[l], w_main, w_if_t)

        bias_t = _rel_bias_table(rel_a[l])
        ya, pak, pav = _attn_a(u, bias_t, nb=nbp, s=sp, row0=0)
        ya, sak, sav = _attn_a(u, bias_t, nb=nbs, s=ss, row0=tp,
                               prefix=(cache_a_k[l].reshape(nbs, PRE_A, WA), cache_a_v[l].reshape(nbs, PRE_A, WA)),
                               out=ya)
        yc, pck, pcv = _attn_c(u, sink_c[l], nb=nbp, s=sp, row0=0, pos0=0)
        yc, sck, scv = _attn_c(u, sink_c[l], nb=nbs, s=ss, row0=tp, pos0=PAST_LEN,
                               prefix=(cache_c_k[l].reshape(nbs, PRE_C, WC_KV),
                                       cache_c_v[l].reshape(nbs, PRE_C, WC_KV)), out=yc)
        g_p = g_t[:, :tp].reshape(2 * H_B, nbp, sp).transpose(1, 0, 2)
        g_s = g_t[:, tp:].reshape(2 * H_B, nbs, ss).transpose(1, 0, 2)
        g_s = jnp.pad(g_s, ((0, 0), (0, 0), (0, lp_s - ss)))
        hist_p = jnp.zeros((nbp, 8, 2 * WB_QK), F32)
        hist_s = jnp.pad(state_b_conv[l], ((0, 0), (8 - (CONV_B - 1), 0), (0, 0)))
        bargs = (conv_w_b[l], conv_b_b[l], b_if_b[l], norm_b_g[l])
        yb, pbc, pbn, pbm, pbx = _mlstm(u, ub, g_p, hist_p, *bargs, *zeros_state, nb=nbp, s=sp, row0=0, L=l_b)
        yb, sbc, sbn, sbm, sbx = _mlstm(u, ub, g_s, hist_s, *bargs, state_b_C[l], state_b_n[l], state_b_m[l],
                                        nb=nbs, s=ss, row0=tp, L=ss, out=yb)

        x = _merge(ya, yb, yc, ub, x, mod_g, w_br_a[l].astype(BF16), w_br_b[l].astype(BF16),
                   w_br_c[l].astype(BF16), w_o[l].astype(BF16))
        i = l // 2
        if l % 2 == 0:
            x = _ffn(x, mod_g, norm2_g[l], w_ff1[i].astype(BF16), w_ff3[i].astype(BF16), w_ff2[i].astype(BF16))
        else:
            x = _moe(x, mod_g, norm2_g[l], w_router[i], b_router[i], w_e1[i], w_e3[i], w_e2[i])

        new["pak"].append(pak.reshape(nbp, keep_a, H_A, DH_A))
        new["pav"].append(pav.reshape(nbp, keep_a, H_A, DH_A))
        new["pbc"].append(pbc)
        new["pbn"].append(pbn)
        new["pbm"].append(pbm)
        new["pbx"].append(pbx)
        new["pck"].append(pck.reshape(nbp, keep_c, HKV_C, DH_C))
        new["pcv"].append(pcv.reshape(nbp, keep_c, HKV_C, DH_C))
        new["sak"].append(sak.reshape(nbs, ss, H_A, DH_A))
        new["sav"].append(sav.reshape(nbs, ss, H_A, DH_A))
        new["sbc"].append(sbc)
        new["sbn"].append(sbn)
        new["sbm"].append(sbm)
        new["sbx"].append(sbx)
        new["sck"].append(sck.reshape(nbs, ss, HKV_C, DH_C))
        new["scv"].append(scv.reshape(nbs, ss, HKV_C, DH_C))

    y_p, y_s = _final_norm(x, norm_f_g, tp)
    st = {k: jnp.stack(v) for k, v in new.items()}
    return (y_p.reshape(nbp, sp, d), y_s.reshape(nbs, ss, d),
            st["pak"], st["pav"], st["pbc"], st["pbn"], st["pbm"], st["pbx"], st["pck"], st["pcv"],
            st["sak"], st["sav"], st["sbc"], st["sbn"], st["sbm"], st["sbx"], st["sck"], st["scv"])
```

```python
import functools

import numpy as np
import jax
import jax.numpy as jnp
from jax import lax
from jax.experimental import pallas as pl
from jax.experimental.pallas import tpu as pltpu

F32 = jnp.float32
BF16 = jnp.bfloat16

D_MODEL = 1024
DEPTH = 2
PAST_LEN = 2048
CHUNK = 64
H_A, DH_A, BACK_A, REL_CLIP = 8, 64, 8, 128
H_B, DK_B, DV_B, CONV_B = 4, 128, 256, 4
HQ_C, HKV_C, DH_C, BACK_C = 8, 2, 64, 2
ROT_DIM = DH_C // 4
ROPE_THETA = 500000.0
D_FF = 2816
N_EXPERTS = 8
D_FF_E = 3584
EPS = 1e-6
NEG = -1e30

WA = H_A * DH_A
WB_QK = H_B * DK_B
WB_V = H_B * DV_B
WC_Q = HQ_C * DH_C
WC_KV = HKV_C * DH_C
PRE_A = BACK_A * CHUNK
PRE_C = BACK_C * CHUNK

COL_QA, COL_KA, COL_VA, COL_QC = 0, 512, 1024, 1536
COL_QKB = 2048
COL_KC = 3072
COL_VC = 3200
N_F32 = 3328
COLB_GATES = 0
COLB_VB = 3072
COLB_OB = 4096
N_B16 = 5120
N_MAIN = N_F32 + N_B16

LANE = 128
SUBLANES = 8
VMEM_LIMIT = 48 * 1024 * 1024
INPROJ_VMEM_SLACK = 8 * 1024 * 1024


def _cparams(*sem, vmem=VMEM_LIMIT):
    return pltpu.CompilerParams(dimension_semantics=sem, vmem_limit_bytes=vmem)


def _row_tile(t):
    for tm in (512, 256, 128, 64):
        if t % tm == 0:
            return tm
    raise ValueError(f"token count {t} is not a multiple of 64")


def _ln_mod(x, g, sc, sh):
    tm, d = x.shape
    y = x * lax.rsqrt(jnp.mean(x * x, axis=-1, keepdims=True) + EPS) * g
    y = y.reshape(tm // CHUNK, CHUNK, d) * (1.0 + sc) + sh
    return y.reshape(tm, d)


def _mod_spec(tm, kind):
    return pl.BlockSpec((tm // CHUNK, 1, D_MODEL), lambda i, *_, k=kind: (i, 0, k))


def _ada_kernel(c_ref, w_ref, b_ref, o_ref):
    c = c_ref[...]
    a = (c * jax.nn.sigmoid(c)).astype(BF16)
    o_ref[...] = jnp.dot(a, w_ref[...].astype(BF16), preferred_element_type=F32) + b_ref[...]


def _ada(c, w, b, layer):
    nb, d = c.shape
    n = w.shape[2]
    tn = 1536
    return pl.pallas_call(
        _ada_kernel,
        grid=(n // tn,),
        in_specs=[pl.BlockSpec((nb, d), lambda j: (0, 0)),
                  pl.BlockSpec((None, d, tn), lambda j: (layer, 0, j)),
                  pl.BlockSpec((None, 1, tn), lambda j: (layer, 0, j))],
        out_specs=pl.BlockSpec((nb, tn), lambda j: (0, j)),
        out_shape=jax.ShapeDtypeStruct((nb, n), F32),
        compiler_params=_cparams("arbitrary"),
        name="adaln",
    )(c, w, b.reshape(b.shape[0], 1, n))


def _col_chunks(n, width=768):
    return [(c, min(width, n - c)) for c in range(0, n, width)]


def _resident(shape):
    return pl.BlockSpec(shape, lambda *_: (0,) * len(shape), pipeline_mode=pl.Buffered(1))


def _inproj_kernel(x_ref, sh_ref, sc_ref, g_ref, w_ref, wif_ref, u_ref, ub_ref, gt_ref, h_scr):
    h_scr[...] = _ln_mod(x_ref[...], g_ref[...], sc_ref[...], sh_ref[...]).astype(BF16)
    gt_ref[...] = lax.dot_general(wif_ref[...], h_scr[...], (((1,), (1,)), ((), ())), preferred_element_type=F32)
    for c, w in _col_chunks(N_F32):
        u_ref[:, c:c + w] = jnp.dot(h_scr[...], w_ref[:, c:c + w], preferred_element_type=F32)
    for c, w in _col_chunks(N_B16):
        ub_ref[:, c:c + w] = jnp.dot(h_scr[...], w_ref[:, N_F32 + c:N_F32 + c + w],
                                     preferred_element_type=F32).astype(BF16)


def _inproj(x, mod_g, g, w_main, w_if_t):
    t, d = x.shape
    tm = _row_tile(t)
    vmem = (d * N_MAIN * 2 + 2 * tm * (d * 4 + N_F32 * 4 + N_B16 * 2) + tm * d * 2) + INPROJ_VMEM_SLACK
    return pl.pallas_call(
        _inproj_kernel,
        grid=(t // tm,),
        in_specs=[pl.BlockSpec((tm, d), lambda i: (i, 0)),
                  _mod_spec(tm, 0), _mod_spec(tm, 1),
                  _resident((1, d)), _resident((d, N_MAIN)), _resident((2 * H_B, d))],
        out_specs=[pl.BlockSpec((tm, N_F32), lambda i: (i, 0)),
                   pl.BlockSpec((tm, N_B16), lambda i: (i, 0)),
                   pl.BlockSpec((2 * H_B, tm), lambda i: (0, i))],
        out_shape=[jax.ShapeDtypeStruct((t, N_F32), F32),
                   jax.ShapeDtypeStruct((t, N_B16), BF16),
                   jax.ShapeDtypeStruct((2 * H_B, t), F32)],
        scratch_shapes=[pltpu.VMEM((tm, d), BF16)],
        compiler_params=_cparams("parallel", vmem=vmem),
        name="inproj",
    )(x, mod_g, mod_g, g.reshape(1, d), w_main, w_if_t)


def _band_table(bq, pre, back):
    qi = np.arange(bq)[:, None] // CHUNK
    kj = np.arange(pre + bq)[None, :] // CHUNK
    return (kj >= qi) & (kj <= qi + back)


QBLK = 128
KV_CHUNK = 512


def _pad_rows(x, rows):
    return x if x.shape[0] == rows else jnp.concatenate(
        [x, jnp.zeros((rows - x.shape[0], x.shape[1]), x.dtype)], axis=0)


def _stage_keys_values(k_rows, v_ref, prefix_k, prefix_v, kpad, vt, pre, s):
    w = kpad.shape[1]
    s_pad = kpad.shape[0] - pre
    if prefix_k is None:
        kpad[0:pre, :] = jnp.zeros((pre, w), BF16)
        vt[:, 0:pre] = jnp.zeros((w, pre), BF16)
    else:
        kpad[0:pre, :] = prefix_k.astype(BF16)
        vt[:, 0:pre] = prefix_v.T.astype(BF16)
    for c in range(0, s_pad, KV_CHUNK):
        rows = min(KV_CHUNK, s_pad - c)
        real = max(0, min(rows, s - c))
        kpad[pre + c:pre + c + rows, :] = _pad_rows(k_rows(c, real), rows).astype(BF16)
        vt[:, pre + c:pre + c + rows] = _pad_rows(v_ref[c:c + real, :], rows).T.astype(BF16)


def _pair_store(o_ref, ots, lane0, rows):
    pair = jnp.concatenate(ots, axis=0).T
    o_ref[:, lane0:lane0 + pair.shape[1]] = pair[0:rows, :]


def _attn_a_kernel(*refs, s, has_prefix):
    if has_prefix:
        q_ref, k_ref, v_ref, pk_ref, pv_ref, bias_ref, o_ref, kt_ref, vt_ref, kpad, vt = refs
    else:
        q_ref, k_ref, v_ref, bias_ref, o_ref, kt_ref, vt_ref, kpad, vt = refs
    n = pl.program_id(1)
    kw = PRE_A + QBLK
    keep = kt_ref.shape[1]
    sq = q_ref.shape[0]

    @pl.when(n == 0)
    def _():
        _stage_keys_values(lambda c, rows: k_ref[c:c + rows, :], v_ref,
                           pk_ref[0] if has_prefix else None, pv_ref[0] if has_prefix else None,
                           kpad, vt, PRE_A, s)
        kt_ref[0] = k_ref[s - keep:s, :]
        vt_ref[0] = v_ref[s - keep:s, :]

    r0 = pl.multiple_of(n * QBLK, QBLK)
    q = _pad_rows((q_ref[...] * (DH_A ** -0.5)).astype(BF16), QBLK)
    heads = [slice(h * DH_A, (h + 1) * DH_A) for h in range(H_A)]
    nt = (((1,), (1,)), ((), ()))

    def attend(mask_keys):
        scores = [lax.dot_general(kpad[pl.ds(r0, kw), hs], q[:, hs], nt, preferred_element_type=F32)
                  for hs in heads]
        if mask_keys:
            key = lax.broadcasted_iota(jnp.int32, (kw, QBLK), 0)
            invalid = jnp.where(key >= PRE_A - r0, 0.0, NEG).astype(F32)
        probs, sums = [], []
        for h in range(H_A):
            sc = scores[h] + bias_ref[h]
            if mask_keys:
                sc = sc + invalid
            e = jnp.exp(sc - jnp.max(sc, axis=0, keepdims=True))
            sums.append(jnp.sum(e, axis=0, keepdims=True))
            probs.append(e.astype(BF16))
        for h in range(0, H_A, 2):
            ots = [jnp.dot(vt[heads[i], pl.ds(r0, kw)], probs[i], preferred_element_type=F32) / sums[i]
                   for i in (h, h + 1)]
            _pair_store(o_ref, ots, h * DH_A, sq)

    if has_prefix:
        attend(False)
    else:
        pl.when(r0 < PRE_A)(lambda: attend(True))
        pl.when(r0 >= PRE_A)(lambda: attend(False))


def _rel_bias_table(rel):
    kw = PRE_A + QBLK
    j0 = PRE_A - REL_CLIP
    rows = kw - j0
    n = QBLK + rows
    m = np.arange(n)
    relf = rel.astype(F32)
    diag = relf[np.clip(m + 1 - QBLK, -REL_CLIP, REL_CLIP) + REL_CLIP].T
    skew = jnp.tile(diag, (1, rows + 1))[:, :rows * (n - 1)].reshape(H_A, rows, n - 1)
    far = jnp.broadcast_to(relf[2 * REL_CLIP][:, None, None], (H_A, j0, QBLK))
    table = jnp.concatenate([far, skew[:, :, rows - 1:rows - 1 + QBLK]], axis=1)
    return jnp.where(_band_table(QBLK, PRE_A, BACK_A).T[None], table, NEG)


def _attn_a(u, bias_t, *, nb, s, row0, prefix=None, out=None):
    t = u.shape[0]
    bq = min(QBLK, s)
    nq = s // bq
    s_pad = max(s, QBLK)
    kw = PRE_A + QBLK
    keep = min(PRE_A, s)
    rb, sb = row0 // bq, row0 // s
    in_specs = [pl.BlockSpec((bq, WA), lambda b, n: (rb + b * nq + n, COL_QA // WA)),
                pl.BlockSpec((s, WA), lambda b, n: (sb + b, COL_KA // WA)),
                pl.BlockSpec((s, WA), lambda b, n: (sb + b, COL_VA // WA))]
    args = [u, u, u]
    if prefix is not None:
        in_specs += [pl.BlockSpec((1, PRE_A, WA), lambda b, n: (b, 0, 0))] * 2
        args += [prefix[0], prefix[1]]
    in_specs.append(pl.BlockSpec((H_A, kw, QBLK), lambda b, n: (0, 0, 0)))
    args.append(bias_t)
    aliases = {}
    if out is not None:
        in_specs.append(pl.BlockSpec(memory_space=pl.ANY))
        args.append(out)
        aliases = {len(args) - 1: 0}
    body = functools.partial(_attn_a_kernel, s=s, has_prefix=prefix is not None)
    if out is not None:
        body = _drop_last_input(body, len(args))
    return pl.pallas_call(
        body,
        grid=(nb, nq),
        in_specs=in_specs,
        out_specs=[pl.BlockSpec((bq, WA), lambda b, n: (rb + b * nq + n, 0)),
                   pl.BlockSpec((1, keep, WA), lambda b, n: (b, 0, 0)),
                   pl.BlockSpec((1, keep, WA), lambda b, n: (b, 0, 0))],
        out_shape=[jax.ShapeDtypeStruct((t, WA), F32),
                   jax.ShapeDtypeStruct((nb, keep, WA), F32),
                   jax.ShapeDtypeStruct((nb, keep, WA), F32)],
        scratch_shapes=[pltpu.VMEM((PRE_A + s_pad, WA), BF16), pltpu.VMEM((WA, PRE_A + s_pad), BF16)],
        input_output_aliases=aliases,
        compiler_params=_cparams("parallel", "arbitrary"),
        name="mixer_a_sample" if prefix is not None else "mixer_a_prompt",
    )(*args)


def _drop_last_input(body, n_in):
    def wrapped(*refs):
        return body(*refs[:n_in - 1], *refs[n_in:])
    return wrapped


def _rope(x, c, s1, s2):
    w = x.shape[1]
    return x * c + pltpu.roll(x, 8, 1) * s1 + pltpu.roll(x, w - 8, 1) * s2


def _attn_c_kernel(*refs, s, has_prefix):
    if has_prefix:
        (q_ref, k_ref, v_ref, pk_ref, pv_ref, rc_ref, rs1_ref, rs2_ref, band_ref, sink_ref,
         o_ref, kt_ref, vt_ref, kpad, vt) = refs
    else:
        (q_ref, k_ref, v_ref, rc_ref, rs1_ref, rs2_ref, band_ref, sink_ref,
         o_ref, kt_ref, vt_ref, kpad, vt) = refs
    n = pl.program_id(1)
    kw = PRE_C + QBLK
    keep = kt_ref.shape[1]
    sq = q_ref.shape[0]

    def rope_at(x, start, rep=1):
        tables = [jnp.concatenate([t_ref[pl.ds(start, x.shape[0]), :]] * rep, axis=1)
                  for t_ref in (rc_ref, rs1_ref, rs2_ref)]
        return _rope(x, *tables)

    @pl.when(n == 0)
    def _():
        _stage_keys_values(lambda c, rows: rope_at(k_ref[c:c + rows, :], c), v_ref,
                           pk_ref[0] if has_prefix else None, pv_ref[0] if has_prefix else None,
                           kpad, vt, PRE_C, s)
        kt_ref[0] = rope_at(k_ref[s - keep:s, :], s - keep)
        vt_ref[0] = v_ref[s - keep:s, :]

    r0 = pl.multiple_of(n * QBLK, QBLK)
    q = rope_at(q_ref[...], pl.multiple_of(n * sq, sq), rep=WC_Q // WC_KV)
    q = _pad_rows((q * (DH_C ** -0.5)).astype(BF16), QBLK)
    per = HQ_C // HKV_C
    groups = [slice(g * DH_C, (g + 1) * DH_C) for g in range(HKV_C)]
    nt = (((1,), (1,)), ((), ()))

    def attend(mask_keys):
        scores = []
        for g, gs in enumerate(groups):
            qs = jnp.concatenate([q[:, (g * per + i) * DH_C:(g * per + i + 1) * DH_C] for i in range(per)], axis=0)
            scores.append(lax.dot_general(kpad[pl.ds(r0, kw), gs], qs, nt, preferred_element_type=F32))
        mask = band_ref[...]
        if mask_keys:
            key = lax.broadcasted_iota(jnp.int32, mask.shape, 0)
            mask = mask + jnp.where(key >= PRE_C - r0, 0.0, NEG).astype(F32)
        probs, sums = [], []
        for g in range(HKV_C):
            sk = jnp.concatenate([jnp.broadcast_to(sink_ref[g * per + i:g * per + i + 1, 0:1], (1, QBLK))
                                  for i in range(per)], axis=1)
            sc = scores[g] + mask
            m = jnp.maximum(jnp.max(sc, axis=0, keepdims=True), sk)
            e = jnp.exp(sc - m)
            sums.append(jnp.sum(e, axis=0, keepdims=True) + jnp.exp(sk - m))
            probs.append(e.astype(BF16))
        for g, gs in enumerate(groups):
            ot = jnp.dot(vt[gs, pl.ds(r0, kw)], probs[g], preferred_element_type=F32) / sums[g]
            for i in range(0, per, 2):
                _pair_store(o_ref, [ot[:, (i + j) * QBLK:(i + j + 1) * QBLK] for j in (0, 1)],
                            (g * per + i) * DH_C, sq)

    if has_prefix:
        attend(False)
    else:
        pl.when(r0 < PRE_C)(lambda: attend(True))
        pl.when(r0 >= PRE_C)(lambda: attend(False))


def _rope_tables(pos):
    half = ROT_DIM // 2
    inv_freq = 1.0 / (ROPE_THETA ** (jnp.arange(half, dtype=F32) * (2.0 / ROT_DIM)))
    ang = pos.astype(F32)[:, None] * inv_freq[None, :]
    cos, sin = jnp.cos(ang), jnp.sin(ang)
    n = pos.shape[0]
    one = jnp.ones((n, DH_C - ROT_DIM), F32)
    zero = jnp.zeros((n, DH_C - ROT_DIM), F32)
    zh = jnp.zeros((n, half), F32)
    c = jnp.concatenate([cos, cos, one], axis=1)
    s1 = jnp.concatenate([zh, sin, zero], axis=1)
    s2 = jnp.concatenate([-sin, zh, zero], axis=1)
    rep = WC_KV // DH_C
    return tuple(jnp.concatenate([a] * rep, axis=1) for a in (c, s1, s2))


def _attn_c(u, sink, *, nb, s, row0, pos0, prefix=None, out=None):
    t = u.shape[0]
    bq = min(QBLK, s)
    nq = s // bq
    s_pad = max(s, QBLK)
    kw = PRE_C + QBLK
    per = HQ_C // HKV_C
    tables = _rope_tables(pos0 + jnp.arange(s))
    band = np.where(_band_table(QBLK, PRE_C, BACK_C), 0.0, NEG).astype(np.float32)
    band = jnp.asarray(np.tile(band.T, (1, per)))
    sink_t = jnp.broadcast_to(sink.astype(F32)[:, None], (HQ_C, LANE))
    rb, sb = row0 // bq, row0 // s
    in_specs = [pl.BlockSpec((bq, WC_Q), lambda b, n: (rb + b * nq + n, COL_QC // WC_Q)),
                pl.BlockSpec((s, WC_KV), lambda b, n: (sb + b, COL_KC // WC_KV)),
                pl.BlockSpec((s, WC_KV), lambda b, n: (sb + b, COL_VC // WC_KV))]
    args = [u, u, u]
    if prefix is not None:
        in_specs += [pl.BlockSpec((1, PRE_C, WC_KV), lambda b, n: (b, 0, 0))] * 2
        args += [prefix[0], prefix[1]]
    in_specs += [pl.BlockSpec((s, WC_KV), lambda b, n: (0, 0))] * 3
    args += list(tables)
    in_specs += [pl.BlockSpec((kw, per * QBLK), lambda b, n: (0, 0)),
                 pl.BlockSpec((HQ_C, LANE), lambda b, n: (0, 0))]
    args += [band, sink_t]
    aliases = {}
    body = functools.partial(_attn_c_kernel, s=s, has_prefix=prefix is not None)
    if out is not None:
        in_specs.append(pl.BlockSpec(memory_space=pl.ANY))
        args.append(out)
        aliases = {len(args) - 1: 0}
        body = _drop_last_input(body, len(args))
    keep = min(PRE_C, s)
    return pl.pallas_call(
        body,
        grid=(nb, nq),
        in_specs=in_specs,
        out_specs=[pl.BlockSpec((bq, WC_Q), lambda b, n: (rb + b * nq + n, 0)),
                   pl.BlockSpec((1, keep, WC_KV), lambda b, n: (b, 0, 0)),
                   pl.BlockSpec((1, keep, WC_KV), lambda b, n: (b, 0, 0))],
        out_shape=[jax.ShapeDtypeStruct((t, WC_Q), F32),
                   jax.ShapeDtypeStruct((nb, keep, WC_KV), F32),
                   jax.ShapeDtypeStruct((nb, keep, WC_KV), F32)],
        scratch_shapes=[pltpu.VMEM((PRE_C + s_pad, WC_KV), BF16), pltpu.VMEM((WC_KV, PRE_C + s_pad), BF16)],
        input_output_aliases=aliases,
        compiler_params=_cparams("parallel", "arbitrary"),
        name="mixer_c_sample" if prefix is not None else "mixer_c_prompt",
    )(*args)


def _log_sigmoid(x):
    return jnp.minimum(x, 0.0) - jnp.log(1.0 + jnp.exp(-jnp.abs(x)))


def _lane_cumsum(x):
    n = x.shape[1]
    col = lax.broadcasted_iota(jnp.int32, x.shape, 1)
    k = 1
    while k < n:
        x = x + jnp.where(col >= k, pltpu.roll(x, k, 1), 0.0)
        k *= 2
    return x


def _mlstm_kernel(qk_ref, v_ref, og_ref, g_ref, hist_ref, cw_ref, cb_ref, bif_ref, ng_ref,
                  c0_ref, n0_ref, m0_ref, y_ref, cf_ref, nf_ref, mf_ref, xt_ref,
                  xpad, c_s, n_s, m_s, *, L):
    c = pl.program_id(1)

    @pl.when(c == 0)
    def _():
        c_s[...] = c0_ref[0]
        n_s[...] = n0_ref[0]
        m_s[...] = m0_ref[0]
        xpad[0:8, :] = hist_ref[0]

    x = qk_ref[...]
    xpad[8:8 + L, :] = x
    conv = (cb_ref[...] + x * cw_ref[3:4, :] + xpad[7:7 + L, :] * cw_ref[2:3, :]
            + xpad[6:6 + L, :] * cw_ref[1:2, :] + xpad[5:5 + L, :] * cw_ref[0:1, :])
    xpad[0:8, :] = xpad[L:L + 8, :]
    act = conv * jax.nn.sigmoid(conv)

    gb = g_ref[...] + bif_ref[:, 0:1]
    b_all = _lane_cumsum(_log_sigmoid(gb))
    ig_rows = gb[0:H_B, :]
    b_rows = b_all[H_B:2 * H_B, :]
    both = jnp.where(lax.broadcasted_iota(jnp.int32, gb.shape, 0) < H_B, gb, b_all)
    cols = jnp.concatenate([both, jnp.zeros((LANE - 2 * H_B, gb.shape[1]), F32)], axis=0).T

    row = lax.broadcasted_iota(jnp.int32, (L, L), 0)
    colm = lax.broadcasted_iota(jnp.int32, (L, L), 1)
    causal = colm <= row

    heads = range(H_B)
    nt = (((1,), (1,)), ((), ()))
    qf = [act[:, h * DK_B:(h + 1) * DK_B] for h in heads]
    kf = [act[:, WB_QK + h * DK_B:WB_QK + (h + 1) * DK_B] * (DK_B ** -0.5) for h in heads]
    qb = [q.astype(BF16) for q in qf]
    vb = [v_ref[:, h * DV_B:(h + 1) * DV_B].astype(BF16) for h in heads]
    cmat = [c_s[h] for h in heads]
    nvec = [n_s[h] for h in heads]
    qk = [lax.dot_general(qb[h], kf[h].astype(BF16), nt, preferred_element_type=F32) for h in heads]
    qc = [jnp.dot(qb[h], cmat[h].astype(BF16), preferred_element_type=F32) for h in heads]

    a, wi, mt, bc, igc = [], [], [], [], []
    for h in heads:
        br = b_rows[h:h + 1, 0:L]
        igr = ig_rows[h:h + 1, 0:L]
        bc.append(cols[0:L, H_B + h:H_B + h + 1])
        igc.append(cols[0:L, h:h + 1])
        logd = jnp.where(causal, bc[h] + (igr - br), NEG)
        li = bc[h] + m_s[h][:, 0:1]
        mt.append(jnp.maximum(li, jnp.max(logd, axis=1, keepdims=True)))
        wi.append(jnp.exp(li - mt[h]))
        a.append(qk[h] * jnp.exp(logd - mt[h]))

    pv = [jnp.dot(a[h].astype(BF16), vb[h], preferred_element_type=F32) for h in heads]
    for h in heads:
        num = pv[h] + wi[h] * qc[h]
        den = jnp.sum(a[h], axis=1, keepdims=True) + wi[h] * jnp.sum(qf[h] * nvec[h], axis=1, keepdims=True)
        hh = num / jnp.maximum(jnp.abs(den), jnp.exp(-mt[h]))
        hn = hh * lax.rsqrt(jnp.mean(hh * hh, axis=-1, keepdims=True) + EPS)
        vs = slice(h * DV_B, (h + 1) * DV_B)
        y_ref[:, vs] = hn * ng_ref[:, vs] * jax.nn.sigmoid(og_ref[:, vs].astype(F32))

    for h in heads:
        bl, ml, wprev = bc[h][L - 1:L, :], mt[h][L - 1:L, :], wi[h][L - 1:L, :]
        kwt = kf[h] * jnp.exp(bl + igc[h] - bc[h] - ml)
        c_s[h] = wprev * cmat[h] + lax.dot_general(kwt.astype(BF16), vb[h], (((0,), (0,)), ((), ())),
                                                   preferred_element_type=F32)
        n_s[h] = wprev * nvec[h] + jnp.sum(kwt, axis=0, keepdims=True)
        m_s[h] = jnp.broadcast_to(ml, (1, LANE))

    @pl.when(c == pl.num_programs(1) - 1)
    def _():
        cf_ref[0] = c_s[...]
        nf_ref[0] = n_s[...]
        mf_ref[0] = m_s[...]
        xt_ref[0] = xpad[0:8, :]


def _mlstm(u, ub, g_rows, hist, cw, cb, bif, ng, c0, n0, m0, *, nb, s, row0, L, out=None):
    t = u.shape[0]
    nc = s // L
    lp = g_rows.shape[2] // nc
    rb = row0 // L
    n0 = n0.reshape(nb, H_B, 1, DK_B)
    m0 = jnp.broadcast_to(m0.reshape(nb, H_B, 1, 1), (nb, H_B, 1, LANE))
    bif_t = jnp.broadcast_to(bif.astype(F32)[:, None], (2 * H_B, LANE))
    row_spec = lambda col: pl.BlockSpec((L, WB_V), lambda b, c: (rb + b * nc + c, col))
    full = lambda shape: pl.BlockSpec(shape, lambda b, c: (0,) * len(shape))
    state = lambda shape: pl.BlockSpec((1,) + shape, lambda b, c: (b,) + (0,) * len(shape))
    in_specs = [row_spec(COL_QKB // WB_V), row_spec(COLB_VB // WB_V), row_spec(COLB_OB // WB_V),
                pl.BlockSpec((None, 2 * H_B, lp), lambda b, c: (b, 0, c)),
                state((8, 2 * WB_QK)),
                full((CONV_B, 2 * WB_QK)), full((1, 2 * WB_QK)), full((2 * H_B, LANE)), full((1, WB_V)),
                state((H_B, DK_B, DV_B)), state((H_B, 1, DK_B)), state((H_B, 1, LANE))]
    args = [u, ub, ub, g_rows, hist, cw, cb.reshape(1, -1), bif_t, ng.reshape(1, -1), c0, n0, m0]
    n_real = len(args)
    aliases = {}
    body = functools.partial(_mlstm_kernel, L=L)
    if out is not None:
        in_specs.append(pl.BlockSpec(memory_space=pl.ANY))
        args.append(out)
        aliases = {n_real: 0}
        body = _drop_last_input(body, len(args))
    y, cf, nf, mf, xt = pl.pallas_call(
        body,
        grid=(nb, nc),
        in_specs=in_specs,
        out_specs=[pl.BlockSpec((L, WB_V), lambda b, c: (rb + b * nc + c, 0)),
                   state((H_B, DK_B, DV_B)), state((H_B, 1, DK_B)), state((H_B, 1, LANE)),
                   state((8, 2 * WB_QK))],
        out_shape=[jax.ShapeDtypeStruct((t, WB_V), F32),
                   jax.ShapeDtypeStruct((nb, H_B, DK_B, DV_B), F32),
                   jax.ShapeDtypeStruct((nb, H_B, 1, DK_B), F32),
                   jax.ShapeDtypeStruct((nb, H_B, 1, LANE), F32),
                   jax.ShapeDtypeStruct((nb, 8, 2 * WB_QK), F32)],
        scratch_shapes=[pltpu.VMEM((L + 8, 2 * WB_QK), F32),
                        pltpu.VMEM((H_B, DK_B, DV_B), F32),
                        pltpu.VMEM((H_B, 1, DK_B), F32),
                        pltpu.VMEM((H_B, 1, LANE), F32)],
        input_output_aliases=aliases,
        compiler_params=_cparams("parallel", "arbitrary"),
        name="mlstm_sample" if out is not None else "mlstm_prompt",
    )(*args)
    return y, cf, nf.reshape(nb, H_B, DK_B), mf[:, :, 0, 0], xt[:, 8 - (CONV_B - 1):, :]


def _merge_kernel(ya_ref, yb_ref, yc_ref, gt_ref, x_ref, g1_ref, wa_ref, wb_ref, wc_ref, wo_ref, o_ref):
    tm, d = x_ref.shape
    sg = jax.nn.sigmoid(gt_ref[...].astype(F32))
    m = (sg[:, 0:d] * jnp.dot(ya_ref[...].astype(BF16), wa_ref[...], preferred_element_type=F32)
         + sg[:, d:2 * d] * jnp.dot(yb_ref[...].astype(BF16), wb_ref[...], preferred_element_type=F32)
         + sg[:, 2 * d:3 * d] * jnp.dot(yc_ref[...].astype(BF16), wc_ref[...], preferred_element_type=F32))
    o = jnp.dot(m.astype(BF16), wo_ref[...], preferred_element_type=F32)
    o = (o.reshape(tm // CHUNK, CHUNK, d) * g1_ref[...]).reshape(tm, d)
    o_ref[...] = x_ref[...] + o


def _merge(ya, yb, yc, ub, x, mod_g, wa, wb, wc, wo):
    t, d = x.shape
    tm = _row_tile(t)
    row = lambda w, col=0: pl.BlockSpec((tm, w), lambda i: (i, col))
    full = lambda a: _resident(a.shape)
    return pl.pallas_call(
        _merge_kernel,
        grid=(t // tm,),
        in_specs=[row(WA), row(WB_V), row(WC_Q), row(3 * d, COLB_GATES // (3 * d)), row(d),
                  _mod_spec(tm, 2), full(wa), full(wb), full(wc), full(wo)],
        out_specs=row(d),
        out_shape=jax.ShapeDtypeStruct((t, d), F32),
        compiler_params=_cparams("parallel"),
        name="merge",
    )(ya, yb, yc, ub, x, mod_g, wa, wb, wc, wo)


def _ffn_kernel(x_ref, sh_ref, sc_ref, g_ref, g2_ref, w1_ref, w3_ref, w2_ref, o_ref, h_scr):
    tm, d = x_ref.shape
    h_scr[...] = _ln_mod(x_ref[...], g_ref[...], sc_ref[...], sh_ref[...]).astype(BF16)
    acc = None
    for c, w in _col_chunks(w1_ref.shape[1]):
        a = jnp.dot(h_scr[...], w1_ref[:, c:c + w], preferred_element_type=F32)
        b = jnp.dot(h_scr[...], w3_ref[:, c:c + w], preferred_element_type=F32)
        tt = (a * jax.nn.sigmoid(a) * b).astype(BF16)
        p = jnp.dot(tt, w2_ref[c:c + w, :], preferred_element_type=F32)
        acc = p if acc is None else acc + p
    f = (acc.reshape(tm // CHUNK, CHUNK, d) * g2_ref[...]).reshape(tm, d)
    o_ref[...] = x_ref[...] + f


def _ffn(x, mod_g, g, w1, w3, w2):
    t, d = x.shape
    tm = _row_tile(t)
    ff = w1.shape[1]
    return pl.pallas_call(
        _ffn_kernel,
        grid=(t // tm,),
        in_specs=[pl.BlockSpec((tm, d), lambda i: (i, 0)),
                  _mod_spec(tm, 3), _mod_spec(tm, 4),
                  _resident((1, d)),
                  _mod_spec(tm, 5),
                  _resident((d, ff)), _resident((d, ff)), _resident((ff, d))],
        out_specs=pl.BlockSpec((tm, d), lambda i: (i, 0)),
        out_shape=jax.ShapeDtypeStruct((t, d), F32),
        scratch_shapes=[pltpu.VMEM((tm, d), BF16)],
        compiler_params=_cparams("parallel"),
        name="ffn_dense",
    )(x, mod_g, mod_g, g.reshape(1, d), mod_g, w1, w3, w2)


def _router_kernel(x_ref, sh_ref, sc_ref, g_ref, wr_ref, br_ref, h_ref, sel_ref):
    h = _ln_mod(x_ref[...], g_ref[...], sc_ref[...], sh_ref[...])
    h_ref[...] = h.reshape(h_ref.shape)
    lg = jnp.dot(h.astype(BF16), wr_ref[...], preferred_element_type=F32) + br_ref[...]
    lane = lax.broadcasted_iota(jnp.int32, lg.shape, 1).astype(F32)
    m1 = jnp.max(lg, axis=1, keepdims=True)
    i1 = jnp.min(jnp.where(lg == m1, lane, float(LANE)), axis=1, keepdims=True)
    lg2 = jnp.where(lane == i1, 2.0 * NEG, lg)
    m2 = jnp.max(lg2, axis=1, keepdims=True)
    i2 = jnp.min(jnp.where(lg2 == m2, lane, float(LANE)), axis=1, keepdims=True)
    e2 = jnp.exp(m2 - m1)
    w1 = 1.0 / (1.0 + e2)
    sel_ref[...] = (jnp.where(lane == 0.0, i1, 0.0) + jnp.where(lane == 1.0, i2, 0.0)
                    + jnp.where(lane == 2.0, w1, 0.0) + jnp.where(lane == 3.0, e2 * w1, 0.0))


def _router(x, mod_g, g, w_r, b_r):
    t, d = x.shape
    tm = _row_tile(t)
    wr = jnp.zeros((d, LANE), BF16).at[:, :N_EXPERTS].set(w_r.astype(BF16))
    br = jnp.full((1, LANE), NEG, F32).at[0, :N_EXPERTS].set(b_r.astype(F32))
    return pl.pallas_call(
        _router_kernel,
        grid=(t // tm,),
        in_specs=[pl.BlockSpec((tm, d), lambda i: (i, 0)),
                  _mod_spec(tm, 3), _mod_spec(tm, 4),
                  pl.BlockSpec((1, d), lambda i: (0, 0)),
                  pl.BlockSpec((d, LANE), lambda i: (0, 0)),
                  pl.BlockSpec((1, LANE), lambda i: (0, 0))],
        out_specs=[pl.BlockSpec((tm // SUBLANES, SUBLANES, d), lambda i: (i, 0, 0)),
                   pl.BlockSpec((tm, LANE), lambda i: (i, 0))],
        out_shape=[jax.ShapeDtypeStruct((t // SUBLANES, SUBLANES, d), F32), jax.ShapeDtypeStruct((t, LANE), F32)],
        compiler_params=_cparams("parallel"),
        name="router",
    )(x, mod_g, mod_g, g.reshape(1, d), wr, br)


def _route(sel, tmr):
    t = sel.shape[0]
    i32 = jnp.int32
    experts = jnp.arange(N_EXPERTS, dtype=i32)[None, :]
    e = sel[:, 0:2].astype(i32).reshape(-1)
    oh = (e[:, None] == experts).astype(i32)
    csum = jnp.cumsum(oh, axis=0)
    cnt = csum[-1]
    rank = jnp.sum(csum * oh, axis=1) - 1
    ntile_e = (cnt + tmr - 1) // tmr
    tile_end = jnp.cumsum(ntile_e)
    tile_start = tile_end - ntile_e
    pos = jnp.sum(oh * (tile_start * tmr)[None, :], axis=1) + rank
    n_tiles = -(-2 * t // tmr) + N_EXPERTS
    n_used = tile_end[-1]
    tiles = jnp.arange(n_tiles, dtype=i32)
    tc = jnp.minimum(tiles, n_used - 1)
    te = jnp.sum((tc[:, None] >= tile_end[None, :]).astype(i32), axis=1)
    ohe = (te[:, None] == experts).astype(i32)
    nvalid = jnp.sum(ohe * cnt[None, :], axis=1) - (tc - jnp.sum(ohe * tile_start[None, :], axis=1)) * tmr
    nvalid = jnp.where(tiles < n_used, jnp.clip(nvalid, 0, tmr), 0)
    return pos.astype(i32), te.astype(i32), nvalid.astype(i32), n_used.reshape(1).astype(i32)


EXPERT_TILE = 1024
EXPERT_SUB = 256
ROUTE_TILE = 256


def _row_copies(n_groups, make_copy):
    def issue(g, carry):
        for sub in range(SUBLANES):
            for k in range(2):
                make_copy(g, sub, k).start(priority=k)
        return carry

    def drain(g, carry):
        for sub in range(SUBLANES):
            for k in range(2):
                make_copy(g, sub, k).wait()
        return carry

    lax.fori_loop(0, n_groups, issue, 0)
    lax.fori_loop(0, n_groups, drain, 0)


def _slot_row(hi_ref, lo_ref, g, sub, k):
    idx = 2 * SUBLANES * g + 2 * sub + k
    return hi_ref[0, 0, idx], lo_ref[0, 0, idx]


def _split_pos(pos, t, tm):
    hi = lax.shift_right_logical(pos, 3).reshape(t // tm, 1, 2 * tm)
    lo = jnp.bitwise_and(pos, SUBLANES - 1).reshape(t // tm, 1, 2 * tm)
    return hi, lo


def _dispatch_kernel(hi_ref, lo_ref, h_ref, xs_in_ref, xs_ref, sem):
    del xs_in_ref

    def make_copy(g, sub, k):
        hi, lo = _slot_row(hi_ref, lo_ref, g, sub, k)
        return pltpu.make_async_copy(h_ref.at[g, pl.ds(sub, 1), :], xs_ref.at[hi, pl.ds(lo, 1), :], sem)

    _row_copies(h_ref.shape[0], make_copy)


def _dispatch(h, pos, n_rows):
    t, d = h.shape[0] * SUBLANES, h.shape[2]
    tm = min(ROUTE_TILE, _row_tile(t))
    xs0 = jnp.zeros((n_rows // SUBLANES, SUBLANES, d), F32)
    smem = pl.BlockSpec((1, 1, 2 * tm), lambda i: (i, 0, 0), memory_space=pltpu.SMEM)
    return pl.pallas_call(
        _dispatch_kernel,
        grid=(t // tm,),
        in_specs=[smem, smem,
                  pl.BlockSpec((tm // SUBLANES, SUBLANES, d), lambda i: (i, 0, 0)),
                  pl.BlockSpec(memory_space=pl.ANY)],
        out_specs=pl.BlockSpec(memory_space=pl.ANY),
        out_shape=jax.ShapeDtypeStruct(xs0.shape, F32),
        scratch_shapes=[pltpu.SemaphoreType.DMA(())],
        input_output_aliases={3: 0},
        compiler_params=_cparams("arbitrary"),
        name="moe_dispatch",
    )(*_split_pos(pos, t, tm), h, xs0)


def _experts_kernel(te_ref, nv_ref, nu_ref, x_ref, w1_ref, w3_ref, w2_ref, y_ref, xb, w1b, w3b, w2b, acc):
    del te_ref, nu_ref
    i, j = pl.program_id(0), pl.program_id(1)
    nv = nv_ref[i]

    @pl.when(nv > 0)
    def _():
        @pl.when(j == 0)
        def _():
            xb[...] = x_ref[...].astype(BF16)
            acc[...] = jnp.zeros_like(acc)

        w1b[...] = w1_ref[0].astype(BF16)
        w3b[...] = w3_ref[0].astype(BF16)
        w2b[...] = w2_ref[0].astype(BF16)

        for m in range(EXPERT_SUB, EXPERT_TILE + 1, EXPERT_SUB):
            @pl.when((nv > m - EXPERT_SUB) & (nv <= m))
            def _(m=m):
                xs = xb[0:m, :]
                a = jnp.dot(xs, w1b[...], preferred_element_type=F32)
                b = jnp.dot(xs, w3b[...], preferred_element_type=F32)
                tt = (a * jax.nn.sigmoid(a) * b).astype(BF16)
                acc[0:m, :] += jnp.dot(tt, w2b[...], preferred_element_type=F32)

        @pl.when(j == pl.num_programs(1) - 1)
        def _():
            y_ref[...] = acc[...]


def _experts(xs, te, nvalid, n_used, w1, w3, w2):
    n_rows, d = xs.shape
    ne, _, ff = w1.shape
    tf = 512
    nj = ff // tf
    n_tiles = n_rows // EXPERT_TILE
    tile_rows = (EXPERT_TILE, d)

    def row_map(i, j, te_ref, nv_ref, nu_ref):
        return (jnp.minimum(i, nu_ref[0] - 1), 0)

    def col(i, j, nu_ref):
        return jnp.where(i < nu_ref[0], j, nj - 1)

    grid_spec = pltpu.PrefetchScalarGridSpec(
        num_scalar_prefetch=3,
        grid=(n_tiles, nj),
        in_specs=[pl.BlockSpec(tile_rows, row_map),
                  pl.BlockSpec((1, d, tf), lambda i, j, te_ref, nv_ref, nu_ref: (te_ref[i], 0, col(i, j, nu_ref))),
                  pl.BlockSpec((1, d, tf), lambda i, j, te_ref, nv_ref, nu_ref: (te_ref[i], 0, col(i, j, nu_ref))),
                  pl.BlockSpec((1, tf, d), lambda i, j, te_ref, nv_ref, nu_ref: (te_ref[i], col(i, j, nu_ref), 0))],
        out_specs=pl.BlockSpec(tile_rows, row_map),
        scratch_shapes=[pltpu.VMEM((EXPERT_TILE, d), BF16), pltpu.VMEM((d, tf), BF16), pltpu.VMEM((d, tf), BF16),
                        pltpu.VMEM((tf, d), BF16), pltpu.VMEM((EXPERT_TILE, d), F32)])
    return pl.pallas_call(
        _experts_kernel,
        grid_spec=grid_spec,
        out_shape=jax.ShapeDtypeStruct(xs.shape, F32),
        compiler_params=_cparams("arbitrary", "arbitrary"),
        name="moe_experts",
    )(te, nvalid, n_used, xs, w1, w3, w2)


def _combine_kernel(hi_ref, lo_ref, sel_ref, x_ref, g2_ref, y_ref, *rest, n_prompt_tiles):
    tm, d = x_ref.shape
    rows, sem = rest[-2:]

    def make_copy(g, sub, k):
        hi, lo = _slot_row(hi_ref, lo_ref, g, sub, k)
        return pltpu.make_async_copy(y_ref.at[hi, pl.ds(lo, 1), :], rows.at[k, g, pl.ds(sub, 1), :], sem)

    _row_copies(tm // SUBLANES, make_copy)
    sel = sel_ref[...]
    f = sel[:, 2:3] * rows[0].reshape(tm, d) + sel[:, 3:4] * rows[1].reshape(tm, d)
    f = (f.reshape(tm // CHUNK, CHUNK, d) * g2_ref[...]).reshape(tm, d)
    xo = x_ref[...] + f
    if n_prompt_tiles is None:
        rest[0][...] = xo
        return
    gf_ref, op_ref, os_ref = rest[:3]
    yo = xo * lax.rsqrt(jnp.mean(xo * xo, axis=-1, keepdims=True) + EPS) * gf_ref[...]
    i = pl.program_id(0)

    @pl.when(i < n_prompt_tiles)
    def _():
        op_ref[...] = yo

    @pl.when(i >= n_prompt_tiles)
    def _():
        os_ref[...] = yo


def _combine(y, pos, sel, x, mod_g, final=None):
    t, d = x.shape
    tm = min(ROUTE_TILE, _row_tile(t if final is None else np.gcd(final[1], t - final[1])))
    smem = pl.BlockSpec((1, 1, 2 * tm), lambda i: (i, 0, 0), memory_space=pltpu.SMEM)
    in_specs = [smem, smem,
                pl.BlockSpec((tm, LANE), lambda i: (i, 0)),
                pl.BlockSpec((tm, d), lambda i: (i, 0)),
                _mod_spec(tm, 5),
                pl.BlockSpec(memory_space=pl.ANY)]
    args = [*_split_pos(pos, t, tm), sel, x, mod_g, y]
    if final is None:
        n_p = None
        out_specs = pl.BlockSpec((tm, d), lambda i: (i, 0))
        out_shape = jax.ShapeDtypeStruct((t, d), F32)
    else:
        gain, tp = final
        n_p = tp // tm
        in_specs.append(pl.BlockSpec((1, d), lambda i: (0, 0)))
        args.append(gain.reshape(1, d))
        out_specs = [pl.BlockSpec((tm, d), lambda i: (jnp.minimum(i, n_p - 1), 0)),
                     pl.BlockSpec((tm, d), lambda i: (jnp.maximum(i - n_p, 0), 0))]
        out_shape = [jax.ShapeDtypeStruct((tp, d), F32), jax.ShapeDtypeStruct((t - tp, d), F32)]
    return pl.pallas_call(
        functools.partial(_combine_kernel, n_prompt_tiles=n_p),
        grid=(t // tm,),
        in_specs=in_specs,
        out_specs=out_specs,
        out_shape=out_shape,
        scratch_shapes=[pltpu.VMEM((2, tm // SUBLANES, SUBLANES, d), F32), pltpu.SemaphoreType.DMA(())],
        compiler_params=_cparams("arbitrary"),
        name="moe_combine",
    )(*args)


def _moe(x, mod_g, g, w_r, b_r, w1, w3, w2, final=None):
    d = x.shape[1]
    h2, sel = _router(x, mod_g, g, w_r, b_r)
    pos, te, nvalid, n_used = _route(sel, EXPERT_TILE)
    n_rows = te.shape[0] * EXPERT_TILE
    xs = _dispatch(h2, pos, n_rows)
    y = _experts(xs.reshape(n_rows, d), te, nvalid, n_used, w1, w3, w2)
    return _combine(y.reshape(n_rows // SUBLANES, SUBLANES, d), pos, sel, x, mod_g, final)


def _final_kernel(x_ref, g_ref, op_ref, os_ref, *, n_prompt_tiles):
    x = x_ref[...]
    y = x * lax.rsqrt(jnp.mean(x * x, axis=-1, keepdims=True) + EPS) * g_ref[...]
    i = pl.program_id(0)

    @pl.when(i < n_prompt_tiles)
    def _():
        op_ref[...] = y

    @pl.when(i >= n_prompt_tiles)
    def _():
        os_ref[...] = y


def _final_norm(x, g, tp):
    t, d = x.shape
    tm = _row_tile(np.gcd(tp, t - tp))
    n_p = tp // tm
    return pl.pallas_call(
        functools.partial(_final_kernel, n_prompt_tiles=n_p),
        grid=(t // tm,),
        in_specs=[pl.BlockSpec((tm, d), lambda i: (i, 0)), pl.BlockSpec((1, d), lambda i: (0, 0))],
        out_specs=[pl.BlockSpec((tm, d), lambda i: (jnp.minimum(i, n_p - 1), 0)),
                   pl.BlockSpec((tm, d), lambda i: (jnp.maximum(i - n_p, 0), 0))],
        out_shape=[jax.ShapeDtypeStruct((tp, d), F32), jax.ShapeDtypeStruct((t - tp, d), F32)],
        compiler_params=_cparams("arbitrary"),
        name="final_norm",
    )(x, g.reshape(1, d))


def _split_w_in(w):
    o = np.cumsum([0, WA, WA, WA, 2 * WB_QK, WB_V, 2 * H_B, WB_V, WC_Q, WC_KV, WC_KV, 3 * D_MODEL])
    seg = lambda k: w[:, o[k]:o[k + 1]]
    qa, ka, va, qkb, vb, ifb, ob, qc, kc, vc, gates = (seg(k) for k in range(11))
    main = jnp.concatenate([qa, ka, va, qc, qkb, kc, vc, gates, vb, ob], axis=1).astype(BF16)
    return main, ifb.T.astype(BF16)


def kernel(x_prompt, x_sample, c_prompt, c_sample, cache_a_k, cache_a_v, state_b_C, state_b_n, state_b_m, state_b_conv, cache_c_k, cache_c_v, norm1_g, norm2_g, w_ada, b_ada, w_in, b_if_b, conv_w_b, conv_b_b, norm_b_g, rel_a, sink_c, w_br_a, w_br_b, w_br_c, w_o, w_ff1, w_ff3, w_ff2, w_router, b_router, w_e1, w_e3, w_e2, norm_f_g):
    nbp, sp, d = x_prompt.shape
    nbs, ss, _ = x_sample.shape
    tp, ts = nbp * sp, nbs * ss
    x = jnp.concatenate([x_prompt.reshape(tp, d), x_sample.reshape(ts, d)], axis=0)
    cond = jnp.concatenate([c_prompt, c_sample], axis=0)
    group_batch = np.concatenate([np.repeat(np.arange(nbp), sp // CHUNK), nbp + np.repeat(np.arange(nbs), ss // CHUNK)])

    lp_s = max(ss, LANE)
    l_b = 256 if sp % 256 == 0 else CHUNK
    keep_a, keep_c = min(PRE_A, sp), min(PRE_C, sp)
    zeros_state = (jnp.zeros((nbp, H_B, DK_B, DV_B), F32), jnp.zeros((nbp, H_B, DK_B), F32), jnp.zeros((nbp, H_B), F32))
    new = {k: [] for k in ("pak", "pav", "pbc", "pbn", "pbm", "pbx", "pck", "pcv",
                           "sak", "sav", "sbc", "sbn", "sbm", "sbx", "sck", "scv")}

    for l in range(DEPTH):
        mod = _ada(cond, w_ada, b_ada, l)
        mod_g = mod[group_batch].reshape(-1, 1, 6 * d)
        w_main, w_if_t = _split_w_in(w_in[l])
        u, ub, g_t = _inproj(x, mod_g, norm1_g[l], w_main, w_if_t)

        bias_t = _rel_bias_table(rel_a[l])
        ya, pak, pav = _attn_a(u, bias_t, nb=nbp, s=sp, row0=0)
        ya, sak, sav = _attn_a(u, bias_t, nb=nbs, s=ss, row0=tp,
                               prefix=(cache_a_k[l].reshape(nbs, PRE_A, WA), cache_a_v[l].reshape(nbs, PRE_A, WA)),
                               out=ya)
        yc, pck, pcv = _attn_c(u, sink_c[l], nb=nbp, s=sp, row0=0, pos0=0)
        yc, sck, scv = _attn_c(u, sink_c[l], nb=nbs, s=ss, row0=tp, pos0=PAST_LEN,
                               prefix=(cache_c_k[l].reshape(nbs, PRE_C, WC_KV),
                                       cache_c_v[l].reshape(nbs, PRE_C, WC_KV)), out=yc)
        g_p = g_t[:, :tp].reshape(2 * H_B, nbp, sp).transpose(1, 0, 2)
        g_s = g_t[:, tp:].reshape(2 * H_B, nbs, ss).transpose(1, 0, 2)
        g_s = jnp.pad(g_s, ((0, 0), (0, 0), (0, lp_s - ss)))
        hist_p = jnp.zeros((nbp, 8, 2 * WB_QK), F32)
        hist_s = jnp.pad(state_b_conv[l], ((0, 0), (8 - (CONV_B - 1), 0), (0, 0)))
        bargs = (conv_w_b[l], conv_b_b[l], b_if_b[l], norm_b_g[l])
        yb, pbc, pbn, pbm, pbx = _mlstm(u, ub, g_p, hist_p, *bargs, *zeros_state, nb=nbp, s=sp, row0=0, L=l_b)
        yb, sbc, sbn, sbm, sbx = _mlstm(u, ub, g_s, hist_s, *bargs, state_b_C[l], state_b_n[l], state_b_m[l],
                                        nb=nbs, s=ss, row0=tp, L=ss, out=yb)

        x = _merge(ya, yb, yc, ub, x, mod_g, w_br_a[l].astype(BF16), w_br_b[l].astype(BF16),
                   w_br_c[l].astype(BF16), w_o[l].astype(BF16))
        i = l // 2
        if l % 2 == 0:
            x = _ffn(x, mod_g, norm2_g[l], w_ff1[i].astype(BF16), w_ff3[i].astype(BF16), w_ff2[i].astype(BF16))
        else:
            x = _moe(x, mod_g, norm2_g[l], w_router[i], b_router[i], w_e1[i], w_e3[i], w_e2[i],
                     final=(norm_f_g, tp) if l == DEPTH - 1 else None)

        for name, val in (("pak", pak), ("pav", pav), ("pbc", pbc), ("pbn", pbn), ("pbm", pbm), ("pbx", pbx),
                          ("pck", pck), ("pcv", pcv), ("sak", sak), ("sav", sav), ("sbc", sbc), ("sbn", sbn),
                          ("sbm", sbm), ("sbx", sbx), ("sck", sck), ("scv", scv)):
            new[name].append(val)

    y_p, y_s = x if DEPTH % 2 == 0 else _final_norm(x, norm_f_g, tp)
    heads = {"pak": (H_A, DH_A), "pav": (H_A, DH_A), "sak": (H_A, DH_A), "sav": (H_A, DH_A),
             "pck": (HKV_C, DH_C), "pcv": (HKV_C, DH_C), "sck": (HKV_C, DH_C), "scv": (HKV_C, DH_C)}
    st = {k: jnp.stack(v) for k, v in new.items()}
    st = {k: v.reshape(v.shape[:-1] + heads[k]) if k in heads else v for k, v in st.items()}
    return (y_p.reshape(nbp, sp, d), y_s.reshape(nbs, ss, d),
            st["pak"], st["pav"], st["pbc"], st["pbn"], st["pbm"], st["pbx"], st["pck"], st["pcv"],
            st["sak"], st["sav"], st["sbc"], st["sbn"], st["sbm"], st["sbx"], st["sck"], st["scv"])
```

```python
import functools

import numpy as np
import jax
import jax.numpy as jnp
from jax import lax
from jax.experimental import pallas as pl
from jax.experimental.pallas import tpu as pltpu

F32 = jnp.float32
BF16 = jnp.bfloat16

D_MODEL = 1024
DEPTH = 2
PAST_LEN = 2048
CHUNK = 64
H_A, DH_A, BACK_A, REL_CLIP = 8, 64, 8, 128
H_B, DK_B, DV_B, CONV_B = 4, 128, 256, 4
HQ_C, HKV_C, DH_C, BACK_C = 8, 2, 64, 2
ROT_DIM = DH_C // 4
ROPE_THETA = 500000.0
D_FF = 2816
N_EXPERTS = 8
D_FF_E = 3584
EPS = 1e-6
NEG = -1e30

WA = H_A * DH_A
WB_QK = H_B * DK_B
WB_V = H_B * DV_B
WC_Q = HQ_C * DH_C
WC_KV = HKV_C * DH_C
PRE_A = BACK_A * CHUNK
PRE_C = BACK_C * CHUNK

COL_QA, COL_KA, COL_VA, COL_QC = 0, 512, 1024, 1536
COL_QKB = 2048
COL_KC = 3072
COL_VC = 3200
N_F32 = 3328
COLB_GATES = 0
COLB_VB = 3072
COLB_OB = 4096
N_B16 = 5120
N_MAIN = N_F32 + N_B16

LANE = 128
SUBLANES = 8
VMEM_LIMIT = 48 * 1024 * 1024
INPROJ_VMEM_SLACK = 8 * 1024 * 1024


def _cparams(*sem, vmem=VMEM_LIMIT):
    return pltpu.CompilerParams(dimension_semantics=sem, vmem_limit_bytes=vmem)


def _row_tile(t):
    for tm in (512, 256, 128, 64):
        if t % tm == 0:
            return tm
    raise ValueError(f"token count {t} is not a multiple of 64")


def _ln_mod(x, g, sc, sh):
    tm, d = x.shape
    y = x * lax.rsqrt(jnp.mean(x * x, axis=-1, keepdims=True) + EPS) * g
    y = y.reshape(tm // CHUNK, CHUNK, d) * (1.0 + sc) + sh
    return y.reshape(tm, d)


def _mod_spec(tm, kind):
    return pl.BlockSpec((tm // CHUNK, 1, D_MODEL), lambda i, *_, k=kind: (i, 0, k))


def _ada_kernel(c_ref, w_ref, b_ref, o_ref):
    c = c_ref[...]
    a = (c * jax.nn.sigmoid(c)).astype(BF16)
    o_ref[...] = jnp.dot(a, w_ref[...].astype(BF16), preferred_element_type=F32) + b_ref[...]


def _ada(c, w, b, layer):
    nb, d = c.shape
    n = w.shape[2]
    tn = 1536
    return pl.pallas_call(
        _ada_kernel,
        grid=(n // tn,),
        in_specs=[pl.BlockSpec((nb, d), lambda j: (0, 0)),
                  pl.BlockSpec((None, d, tn), lambda j: (layer, 0, j)),
                  pl.BlockSpec((None, 1, tn), lambda j: (layer, 0, j))],
        out_specs=pl.BlockSpec((nb, tn), lambda j: (0, j)),
        out_shape=jax.ShapeDtypeStruct((nb, n), F32),
        compiler_params=_cparams("arbitrary"),
        name="adaln",
    )(c, w, b.reshape(b.shape[0], 1, n))


def _col_chunks(n, width=768):
    return [(c, min(width, n - c)) for c in range(0, n, width)]


def _resident(shape):
    return pl.BlockSpec(shape, lambda *_: (0,) * len(shape), pipeline_mode=pl.Buffered(1))


def _inproj_kernel(x_ref, sh_ref, sc_ref, g_ref, w_ref, wif_ref, u_ref, ub_ref, gt_ref, h_scr):
    h_scr[...] = _ln_mod(x_ref[...], g_ref[...], sc_ref[...], sh_ref[...]).astype(BF16)
    gt_ref[...] = lax.dot_general(wif_ref[...], h_scr[...], (((1,), (1,)), ((), ())), preferred_element_type=F32)
    for c, w in _col_chunks(N_F32):
        u_ref[:, c:c + w] = jnp.dot(h_scr[...], w_ref[:, c:c + w], preferred_element_type=F32)
    for c, w in _col_chunks(N_B16):
        ub_ref[:, c:c + w] = jnp.dot(h_scr[...], w_ref[:, N_F32 + c:N_F32 + c + w],
                                     preferred_element_type=F32).astype(BF16)


def _inproj(x, mod_g, g, w_main, w_if_t):
    t, d = x.shape
    tm = _row_tile(t)
    vmem = (d * N_MAIN * 2 + 2 * tm * (d * 4 + N_F32 * 4 + N_B16 * 2) + tm * d * 2) + INPROJ_VMEM_SLACK
    return pl.pallas_call(
        _inproj_kernel,
        grid=(t // tm,),
        in_specs=[pl.BlockSpec((tm, d), lambda i: (i, 0)),
                  _mod_spec(tm, 0), _mod_spec(tm, 1),
                  _resident((1, d)), _resident((d, N_MAIN)), _resident((2 * H_B, d))],
        out_specs=[pl.BlockSpec((tm, N_F32), lambda i: (i, 0)),
                   pl.BlockSpec((tm, N_B16), lambda i: (i, 0)),
                   pl.BlockSpec((2 * H_B, tm), lambda i: (0, i))],
        out_shape=[jax.ShapeDtypeStruct((t, N_F32), F32),
                   jax.ShapeDtypeStruct((t, N_B16), BF16),
                   jax.ShapeDtypeStruct((2 * H_B, t), F32)],
        scratch_shapes=[pltpu.VMEM((tm, d), BF16)],
        compiler_params=_cparams("parallel", vmem=vmem),
        name="inproj",
    )(x, mod_g, mod_g, g.reshape(1, d), w_main, w_if_t)


def _band_table(bq, pre, back):
    qi = np.arange(bq)[:, None] // CHUNK
    kj = np.arange(pre + bq)[None, :] // CHUNK
    return (kj >= qi) & (kj <= qi + back)


QBLK = 128
KV_CHUNK = 512


def _pad_rows(x, rows):
    return x if x.shape[0] == rows else jnp.concatenate(
        [x, jnp.zeros((rows - x.shape[0], x.shape[1]), x.dtype)], axis=0)


def _stage_keys_values(k_rows, v_ref, prefix_k, prefix_v, kpad, vt, pre, s):
    w = kpad.shape[1]
    s_pad = kpad.shape[0] - pre
    if prefix_k is None:
        kpad[0:pre, :] = jnp.zeros((pre, w), BF16)
        vt[:, 0:pre] = jnp.zeros((w, pre), BF16)
    else:
        kpad[0:pre, :] = prefix_k.astype(BF16)
        vt[:, 0:pre] = prefix_v.T.astype(BF16)
    for c in range(0, s_pad, KV_CHUNK):
        rows = min(KV_CHUNK, s_pad - c)
        real = max(0, min(rows, s - c))
        kpad[pre + c:pre + c + rows, :] = _pad_rows(k_rows(c, real), rows).astype(BF16)
        vt[:, pre + c:pre + c + rows] = _pad_rows(v_ref[c:c + real, :], rows).T.astype(BF16)


def _pair_store(o_ref, ots, lane0, rows):
    pair = jnp.concatenate(ots, axis=0).T
    o_ref[:, lane0:lane0 + pair.shape[1]] = pair[0:rows, :]


def _attn_a_kernel(*refs, s, has_prefix):
    if has_prefix:
        q_ref, k_ref, v_ref, pk_ref, pv_ref, bias_ref, o_ref, kt_ref, vt_ref, kpad, vt = refs
    else:
        q_ref, k_ref, v_ref, bias_ref, o_ref, kt_ref, vt_ref, kpad, vt = refs
    n = pl.program_id(1)
    kw = PRE_A + QBLK
    keep = kt_ref.shape[1]
    sq = q_ref.shape[0]

    @pl.when(n == 0)
    def _():
        _stage_keys_values(lambda c, rows: k_ref[c:c + rows, :], v_ref,
                           pk_ref[0] if has_prefix else None, pv_ref[0] if has_prefix else None,
                           kpad, vt, PRE_A, s)
        kt_ref[0] = k_ref[s - keep:s, :]
        vt_ref[0] = v_ref[s - keep:s, :]

    r0 = pl.multiple_of(n * QBLK, QBLK)
    q = _pad_rows((q_ref[...] * (DH_A ** -0.5)).astype(BF16), QBLK)
    heads = [slice(h * DH_A, (h + 1) * DH_A) for h in range(H_A)]
    nt = (((1,), (1,)), ((), ()))

    def attend(mask_keys):
        scores = [lax.dot_general(kpad[pl.ds(r0, kw), hs], q[:, hs], nt, preferred_element_type=F32)
                  for hs in heads]
        if mask_keys:
            key = lax.broadcasted_iota(jnp.int32, (kw, QBLK), 0)
            invalid = jnp.where(key >= PRE_A - r0, 0.0, NEG).astype(F32)
        probs, sums = [], []
        for h in range(H_A):
            sc = scores[h] + bias_ref[h]
            if mask_keys:
                sc = sc + invalid
            e = jnp.exp(sc - jnp.max(sc, axis=0, keepdims=True))
            sums.append(jnp.sum(e, axis=0, keepdims=True))
            probs.append(e.astype(BF16))
        for h in range(0, H_A, 2):
            ots = [jnp.dot(vt[heads[i], pl.ds(r0, kw)], probs[i], preferred_element_type=F32) / sums[i]
                   for i in (h, h + 1)]
            _pair_store(o_ref, ots, h * DH_A, sq)

    if has_prefix:
        attend(False)
    else:
        pl.when(r0 < PRE_A)(lambda: attend(True))
        pl.when(r0 >= PRE_A)(lambda: attend(False))


def _rel_bias_table(rel):
    kw = PRE_A + QBLK
    j0 = PRE_A - REL_CLIP
    rows = kw - j0
    n = QBLK + rows
    m = np.arange(n)
    relf = rel.astype(F32)
    diag = relf[np.clip(m + 1 - QBLK, -REL_CLIP, REL_CLIP) + REL_CLIP].T
    skew = jnp.tile(diag, (1, rows + 1))[:, :rows * (n - 1)].reshape(H_A, rows, n - 1)
    far = jnp.broadcast_to(relf[2 * REL_CLIP][:, None, None], (H_A, j0, QBLK))
    table = jnp.concatenate([far, skew[:, :, rows - 1:rows - 1 + QBLK]], axis=1)
    return jnp.where(_band_table(QBLK, PRE_A, BACK_A).T[None], table, NEG)


def _attn_a(u, bias_t, *, nb, s, row0, prefix=None, out=None):
    t = u.shape[0]
    bq = min(QBLK, s)
    nq = s // bq
    s_pad = max(s, QBLK)
    kw = PRE_A + QBLK
    keep = min(PRE_A, s)
    rb, sb = row0 // bq, row0 // s
    in_specs = [pl.BlockSpec((bq, WA), lambda b, n: (rb + b * nq + n, COL_QA // WA)),
                pl.BlockSpec((s, WA), lambda b, n: (sb + b, COL_KA // WA)),
                pl.BlockSpec((s, WA), lambda b, n: (sb + b, COL_VA // WA))]
    args = [u, u, u]
    if prefix is not None:
        in_specs += [pl.BlockSpec((1, PRE_A, WA), lambda b, n: (b, 0, 0))] * 2
        args += [prefix[0], prefix[1]]
    in_specs.append(pl.BlockSpec((H_A, kw, QBLK), lambda b, n: (0, 0, 0)))
    args.append(bias_t)
    aliases = {}
    if out is not None:
        in_specs.append(pl.BlockSpec(memory_space=pl.ANY))
        args.append(out)
        aliases = {len(args) - 1: 0}
    body = functools.partial(_attn_a_kernel, s=s, has_prefix=prefix is not None)
    if out is not None:
        body = _drop_last_input(body, len(args))
    return pl.pallas_call(
        body,
        grid=(nb, nq),
        in_specs=in_specs,
        out_specs=[pl.BlockSpec((bq, WA), lambda b, n: (rb + b * nq + n, 0)),
                   pl.BlockSpec((1, keep, WA), lambda b, n: (b, 0, 0)),
                   pl.BlockSpec((1, keep, WA), lambda b, n: (b, 0, 0))],
        out_shape=[jax.ShapeDtypeStruct((t, WA), F32),
                   jax.ShapeDtypeStruct((nb, keep, WA), F32),
                   jax.ShapeDtypeStruct((nb, keep, WA), F32)],
        scratch_shapes=[pltpu.VMEM((PRE_A + s_pad, WA), BF16), pltpu.VMEM((WA, PRE_A + s_pad), BF16)],
        input_output_aliases=aliases,
        compiler_params=_cparams("parallel", "arbitrary"),
        name="mixer_a_sample" if prefix is not None else "mixer_a_prompt",
    )(*args)


def _drop_last_input(body, n_in):
    def wrapped(*refs):
        return body(*refs[:n_in - 1], *refs[n_in:])
    return wrapped


def _rope(x, c, s1, s2):
    w = x.shape[1]
    return x * c + pltpu.roll(x, 8, 1) * s1 + pltpu.roll(x, w - 8, 1) * s2


def _attn_c_kernel(*refs, s, has_prefix):
    if has_prefix:
        (q_ref, k_ref, v_ref, pk_ref, pv_ref, rc_ref, rs1_ref, rs2_ref, band_ref, sink_ref,
         o_ref, kt_ref, vt_ref, kpad, vt) = refs
    else:
        (q_ref, k_ref, v_ref, rc_ref, rs1_ref, rs2_ref, band_ref, sink_ref,
         o_ref, kt_ref, vt_ref, kpad, vt) = refs
    n = pl.program_id(1)
    kw = PRE_C + QBLK
    keep = kt_ref.shape[1]
    sq = q_ref.shape[0]

    def rope_at(x, start, rep=1):
        tables = [jnp.concatenate([t_ref[pl.ds(start, x.shape[0]), :]] * rep, axis=1)
                  for t_ref in (rc_ref, rs1_ref, rs2_ref)]
        return _rope(x, *tables)

    @pl.when(n == 0)
    def _():
        _stage_keys_values(lambda c, rows: rope_at(k_ref[c:c + rows, :], c), v_ref,
                           pk_ref[0] if has_prefix else None, pv_ref[0] if has_prefix else None,
                           kpad, vt, PRE_C, s)
        kt_ref[0] = rope_at(k_ref[s - keep:s, :], s - keep)
        vt_ref[0] = v_ref[s - keep:s, :]

    r0 = pl.multiple_of(n * QBLK, QBLK)
    q = rope_at(q_ref[...], pl.multiple_of(n * sq, sq), rep=WC_Q // WC_KV)
    q = _pad_rows((q * (DH_C ** -0.5)).astype(BF16), QBLK)
    per = HQ_C // HKV_C
    groups = [slice(g * DH_C, (g + 1) * DH_C) for g in range(HKV_C)]
    nt = (((1,), (1,)), ((), ()))

    def attend(mask_keys):
        scores = []
        for g, gs in enumerate(groups):
            qs = jnp.concatenate([q[:, (g * per + i) * DH_C:(g * per + i + 1) * DH_C] for i in range(per)], axis=0)
            scores.append(lax.dot_general(kpad[pl.ds(r0, kw), gs], qs, nt, preferred_element_type=F32))
        mask = band_ref[...]
        if mask_keys:
            key = lax.broadcasted_iota(jnp.int32, mask.shape, 0)
            mask = mask + jnp.where(key >= PRE_C - r0, 0.0, NEG).astype(F32)
        probs, sums = [], []
        for g in range(HKV_C):
            sk = jnp.concatenate([jnp.broadcast_to(sink_ref[g * per + i:g * per + i + 1, 0:1], (1, QBLK))
                                  for i in range(per)], axis=1)
            sc = scores[g] + mask
            m = jnp.maximum(jnp.max(sc, axis=0, keepdims=True), sk)
            e = jnp.exp(sc - m)
            sums.append(jnp.sum(e, axis=0, keepdims=True) + jnp.exp(sk - m))
            probs.append(e.astype(BF16))
        for g, gs in enumerate(groups):
            ot = jnp.dot(vt[gs, pl.ds(r0, kw)], probs[g], preferred_element_type=F32) / sums[g]
            for i in range(0, per, 2):
                _pair_store(o_ref, [ot[:, (i + j) * QBLK:(i + j + 1) * QBLK] for j in (0, 1)],
                            (g * per + i) * DH_C, sq)

    if has_prefix:
        attend(False)
    else:
        pl.when(r0 < PRE_C)(lambda: attend(True))
        pl.when(r0 >= PRE_C)(lambda: attend(False))


def _rope_tables(pos):
    half = ROT_DIM // 2
    inv_freq = 1.0 / (ROPE_THETA ** (jnp.arange(half, dtype=F32) * (2.0 / ROT_DIM)))
    ang = pos.astype(F32)[:, None] * inv_freq[None, :]
    cos, sin = jnp.cos(ang), jnp.sin(ang)
    n = pos.shape[0]
    one = jnp.ones((n, DH_C - ROT_DIM), F32)
    zero = jnp.zeros((n, DH_C - ROT_DIM), F32)
    zh = jnp.zeros((n, half), F32)
    c = jnp.concatenate([cos, cos, one], axis=1)
    s1 = jnp.concatenate([zh, sin, zero], axis=1)
    s2 = jnp.concatenate([-sin, zh, zero], axis=1)
    rep = WC_KV // DH_C
    return tuple(jnp.concatenate([a] * rep, axis=1) for a in (c, s1, s2))


def _attn_c(u, sink, *, nb, s, row0, pos0, prefix=None, out=None):
    t = u.shape[0]
    bq = min(QBLK, s)
    nq = s // bq
    s_pad = max(s, QBLK)
    kw = PRE_C + QBLK
    per = HQ_C // HKV_C
    tables = _rope_tables(pos0 + jnp.arange(s))
    band = np.where(_band_table(QBLK, PRE_C, BACK_C), 0.0, NEG).astype(np.float32)
    band = jnp.asarray(np.tile(band.T, (1, per)))
    sink_t = jnp.broadcast_to(sink.astype(F32)[:, None], (HQ_C, LANE))
    rb, sb = row0 // bq, row0 // s
    in_specs = [pl.BlockSpec((bq, WC_Q), lambda b, n: (rb + b * nq + n, COL_QC // WC_Q)),
                pl.BlockSpec((s, WC_KV), lambda b, n: (sb + b, COL_KC // WC_KV)),
                pl.BlockSpec((s, WC_KV), lambda b, n: (sb + b, COL_VC // WC_KV))]
    args = [u, u, u]
    if prefix is not None:
        in_specs += [pl.BlockSpec((1, PRE_C, WC_KV), lambda b, n: (b, 0, 0))] * 2
        args += [prefix[0], prefix[1]]
    in_specs += [pl.BlockSpec((s, WC_KV), lambda b, n: (0, 0))] * 3
    args += list(tables)
    in_specs += [pl.BlockSpec((kw, per * QBLK), lambda b, n: (0, 0)),
                 pl.BlockSpec((HQ_C, LANE), lambda b, n: (0, 0))]
    args += [band, sink_t]
    aliases = {}
    body = functools.partial(_attn_c_kernel, s=s, has_prefix=prefix is not None)
    if out is not None:
        in_specs.append(pl.BlockSpec(memory_space=pl.ANY))
        args.append(out)
        aliases = {len(args) - 1: 0}
        body = _drop_last_input(body, len(args))
    keep = min(PRE_C, s)
    return pl.pallas_call(
        body,
        grid=(nb, nq),
        in_specs=in_specs,
        out_specs=[pl.BlockSpec((bq, WC_Q), lambda b, n: (rb + b * nq + n, 0)),
                   pl.BlockSpec((1, keep, WC_KV), lambda b, n: (b, 0, 0)),
                   pl.BlockSpec((1, keep, WC_KV), lambda b, n: (b, 0, 0))],
        out_shape=[jax.ShapeDtypeStruct((t, WC_Q), F32),
                   jax.ShapeDtypeStruct((nb, keep, WC_KV), F32),
                   jax.ShapeDtypeStruct((nb, keep, WC_KV), F32)],
        scratch_shapes=[pltpu.VMEM((PRE_C + s_pad, WC_KV), BF16), pltpu.VMEM((WC_KV, PRE_C + s_pad), BF16)],
        input_output_aliases=aliases,
        compiler_params=_cparams("parallel", "arbitrary"),
        name="mixer_c_sample" if prefix is not None else "mixer_c_prompt",
    )(*args)


def _log_sigmoid(x):
    return jnp.minimum(x, 0.0) - jnp.log(1.0 + jnp.exp(-jnp.abs(x)))


def _lane_cumsum(x):
    n = x.shape[1]
    col = lax.broadcasted_iota(jnp.int32, x.shape, 1)
    k = 1
    while k < n:
        x = x + jnp.where(col >= k, pltpu.roll(x, k, 1), 0.0)
        k *= 2
    return x


def _mlstm_kernel(qk_ref, v_ref, og_ref, g_ref, hist_ref, cw_ref, cb_ref, bif_ref, ng_ref,
                  c0_ref, n0_ref, m0_ref, y_ref, cf_ref, nf_ref, mf_ref, xt_ref,
                  xpad, c_s, n_s, m_s, *, L):
    c = pl.program_id(1)

    @pl.when(c == 0)
    def _():
        c_s[...] = c0_ref[0]
        n_s[...] = n0_ref[0]
        m_s[...] = m0_ref[0]
        xpad[0:8, :] = hist_ref[0]

    x = qk_ref[...]
    xpad[8:8 + L, :] = x
    conv = (cb_ref[...] + x * cw_ref[3:4, :] + xpad[7:7 + L, :] * cw_ref[2:3, :]
            + xpad[6:6 + L, :] * cw_ref[1:2, :] + xpad[5:5 + L, :] * cw_ref[0:1, :])
    xpad[0:8, :] = xpad[L:L + 8, :]
    act = conv * jax.nn.sigmoid(conv)

    gb = g_ref[...] + bif_ref[:, 0:1]
    b_all = _lane_cumsum(_log_sigmoid(gb))
    ig_rows = gb[0:H_B, :]
    b_rows = b_all[H_B:2 * H_B, :]
    both = jnp.where(lax.broadcasted_iota(jnp.int32, gb.shape, 0) < H_B, gb, b_all)
    cols = jnp.concatenate([both, jnp.zeros((LANE - 2 * H_B, gb.shape[1]), F32)], axis=0).T

    row = lax.broadcasted_iota(jnp.int32, (L, L), 0)
    colm = lax.broadcasted_iota(jnp.int32, (L, L), 1)
    causal = colm <= row

    heads = range(H_B)
    nt = (((1,), (1,)), ((), ()))
    qf = [act[:, h * DK_B:(h + 1) * DK_B] for h in heads]
    kf = [act[:, WB_QK + h * DK_B:WB_QK + (h + 1) * DK_B] * (DK_B ** -0.5) for h in heads]
    qb = [q.astype(BF16) for q in qf]
    vb = [v_ref[:, h * DV_B:(h + 1) * DV_B].astype(BF16) for h in heads]
    cmat = [c_s[h] for h in heads]
    nvec = [n_s[h] for h in heads]
    qk = [lax.dot_general(qb[h], kf[h].astype(BF16), nt, preferred_element_type=F32) for h in heads]
    qc = [jnp.dot(qb[h], cmat[h].astype(BF16), preferred_element_type=F32) for h in heads]

    a, wi, mt, bc, igc = [], [], [], [], []
    for h in heads:
        br = b_rows[h:h + 1, 0:L]
        igr = ig_rows[h:h + 1, 0:L]
        bc.append(cols[0:L, H_B + h:H_B + h + 1])
        igc.append(cols[0:L, h:h + 1])
        logd = jnp.where(causal, bc[h] + (igr - br), NEG)
        li = bc[h] + m_s[h][:, 0:1]
        mt.append(jnp.maximum(li, jnp.max(logd, axis=1, keepdims=True)))
        wi.append(jnp.exp(li - mt[h]))
        a.append(qk[h] * jnp.exp(logd - mt[h]))

    pv = [jnp.dot(a[h].astype(BF16), vb[h], preferred_element_type=F32) for h in heads]
    for h in heads:
        num = pv[h] + wi[h] * qc[h]
        den = jnp.sum(a[h], axis=1, keepdims=True) + wi[h] * jnp.sum(qf[h] * nvec[h], axis=1, keepdims=True)
        hh = num / jnp.maximum(jnp.abs(den), jnp.exp(-mt[h]))
        hn = hh * lax.rsqrt(jnp.mean(hh * hh, axis=-1, keepdims=True) + EPS)
        vs = slice(h * DV_B, (h + 1) * DV_B)
        y_ref[:, vs] = hn * ng_ref[:, vs] * jax.nn.sigmoid(og_ref[:, vs].astype(F32))

    for h in heads:
        bl, ml, wprev = bc[h][L - 1:L, :], mt[h][L - 1:L, :], wi[h][L - 1:L, :]
        kwt = kf[h] * jnp.exp(bl + igc[h] - bc[h] - ml)
        c_s[h] = wprev * cmat[h] + lax.dot_general(kwt.astype(BF16), vb[h], (((0,), (0,)), ((), ())),
                                                   preferred_element_type=F32)
        n_s[h] = wprev * nvec[h] + jnp.sum(kwt, axis=0, keepdims=True)
        m_s[h] = jnp.broadcast_to(ml, (1, LANE))

    @pl.when(c == pl.num_programs(1) - 1)
    def _():
        cf_ref[0] = c_s[...]
        nf_ref[0] = n_s[...]
        mf_ref[0] = m_s[...]
        xt_ref[0] = xpad[0:8, :]


def _mlstm(u, ub, g_rows, hist, cw, cb, bif, ng, c0, n0, m0, *, nb, s, row0, L, out=None):
    t = u.shape[0]
    nc = s // L
    lp = g_rows.shape[2] // nc
    rb = row0 // L
    n0 = n0.reshape(nb, H_B, 1, DK_B)
    m0 = jnp.broadcast_to(m0.reshape(nb, H_B, 1, 1), (nb, H_B, 1, LANE))
    bif_t = jnp.broadcast_to(bif.astype(F32)[:, None], (2 * H_B, LANE))
    row_spec = lambda col: pl.BlockSpec((L, WB_V), lambda b, c: (rb + b * nc + c, col))
    full = lambda shape: pl.BlockSpec(shape, lambda b, c: (0,) * len(shape))
    state = lambda shape: pl.BlockSpec((1,) + shape, lambda b, c: (b,) + (0,) * len(shape))
    in_specs = [row_spec(COL_QKB // WB_V), row_spec(COLB_VB // WB_V), row_spec(COLB_OB // WB_V),
                pl.BlockSpec((None, 2 * H_B, lp), lambda b, c: (b, 0, c)),
                state((8, 2 * WB_QK)),
                full((CONV_B, 2 * WB_QK)), full((1, 2 * WB_QK)), full((2 * H_B, LANE)), full((1, WB_V)),
                state((H_B, DK_B, DV_B)), state((H_B, 1, DK_B)), state((H_B, 1, LANE))]
    args = [u, ub, ub, g_rows, hist, cw, cb.reshape(1, -1), bif_t, ng.reshape(1, -1), c0, n0, m0]
    n_real = len(args)
    aliases = {}
    body = functools.partial(_mlstm_kernel, L=L)
    if out is not None:
        in_specs.append(pl.BlockSpec(memory_space=pl.ANY))
        args.append(out)
        aliases = {n_real: 0}
        body = _drop_last_input(body, len(args))
    y, cf, nf, mf, xt = pl.pallas_call(
        body,
        grid=(nb, nc),
        in_specs=in_specs,
        out_specs=[pl.BlockSpec((L, WB_V), lambda b, c: (rb + b * nc + c, 0)),
                   state((H_B, DK_B, DV_B)), state((H_B, 1, DK_B)), state((H_B, 1, LANE)),
                   state((8, 2 * WB_QK))],
        out_shape=[jax.ShapeDtypeStruct((t, WB_V), F32),
                   jax.ShapeDtypeStruct((nb, H_B, DK_B, DV_B), F32),
                   jax.ShapeDtypeStruct((nb, H_B, 1, DK_B), F32),
                   jax.ShapeDtypeStruct((nb, H_B, 1, LANE), F32),
                   jax.ShapeDtypeStruct((nb, 8, 2 * WB_QK), F32)],
        scratch_shapes=[pltpu.VMEM((L + 8, 2 * WB_QK), F32),
                        pltpu.VMEM((H_B, DK_B, DV_B), F32),
                        pltpu.VMEM((H_B, 1, DK_B), F32),
                        pltpu.VMEM((H_B, 1, LANE), F32)],
        input_output_aliases=aliases,
        compiler_params=_cparams("parallel", "arbitrary"),
        name="mlstm_sample" if out is not None else "mlstm_prompt",
    )(*args)
    return y, cf, nf.reshape(nb, H_B, DK_B), mf[:, :, 0, 0], xt[:, 8 - (CONV_B - 1):, :]


def _merge_kernel(ya_ref, yb_ref, yc_ref, gt_ref, x_ref, g1_ref, wa_ref, wb_ref, wc_ref, wo_ref, o_ref):
    tm, d = x_ref.shape
    sg = jax.nn.sigmoid(gt_ref[...].astype(F32))
    m = (sg[:, 0:d] * jnp.dot(ya_ref[...].astype(BF16), wa_ref[...], preferred_element_type=F32)
         + sg[:, d:2 * d] * jnp.dot(yb_ref[...].astype(BF16), wb_ref[...], preferred_element_type=F32)
         + sg[:, 2 * d:3 * d] * jnp.dot(yc_ref[...].astype(BF16), wc_ref[...], preferred_element_type=F32))
    o = jnp.dot(m.astype(BF16), wo_ref[...], preferred_element_type=F32)
    o = (o.reshape(tm // CHUNK, CHUNK, d) * g1_ref[...]).reshape(tm, d)
    o_ref[...] = x_ref[...] + o


def _merge(ya, yb, yc, ub, x, mod_g, wa, wb, wc, wo):
    t, d = x.shape
    tm = _row_tile(t)
    row = lambda w, col=0: pl.BlockSpec((tm, w), lambda i: (i, col))
    full = lambda a: _resident(a.shape)
    return pl.pallas_call(
        _merge_kernel,
        grid=(t // tm,),
        in_specs=[row(WA), row(WB_V), row(WC_Q), row(3 * d, COLB_GATES // (3 * d)), row(d),
                  _mod_spec(tm, 2), full(wa), full(wb), full(wc), full(wo)],
        out_specs=row(d),
        out_shape=jax.ShapeDtypeStruct((t, d), F32),
        compiler_params=_cparams("parallel"),
        name="merge",
    )(ya, yb, yc, ub, x, mod_g, wa, wb, wc, wo)


def _ffn_kernel(x_ref, sh_ref, sc_ref, g_ref, g2_ref, w1_ref, w3_ref, w2_ref, o_ref, h_scr):
    tm, d = x_ref.shape
    h_scr[...] = _ln_mod(x_ref[...], g_ref[...], sc_ref[...], sh_ref[...]).astype(BF16)
    acc = None
    for c, w in _col_chunks(w1_ref.shape[1]):
        a = jnp.dot(h_scr[...], w1_ref[:, c:c + w], preferred_element_type=F32)
        b = jnp.dot(h_scr[...], w3_ref[:, c:c + w], preferred_element_type=F32)
        tt = (a * jax.nn.sigmoid(a) * b).astype(BF16)
        p = jnp.dot(tt, w2_ref[c:c + w, :], preferred_element_type=F32)
        acc = p if acc is None else acc + p
    f = (acc.reshape(tm // CHUNK, CHUNK, d) * g2_ref[...]).reshape(tm, d)
    o_ref[...] = x_ref[...] + f


def _ffn(x, mod_g, g, w1, w3, w2):
    t, d = x.shape
    tm = _row_tile(t)
    ff = w1.shape[1]
    return pl.pallas_call(
        _ffn_kernel,
        grid=(t // tm,),
        in_specs=[pl.BlockSpec((tm, d), lambda i: (i, 0)),
                  _mod_spec(tm, 3), _mod_spec(tm, 4),
                  _resident((1, d)),
                  _mod_spec(tm, 5),
                  _resident((d, ff)), _resident((d, ff)), _resident((ff, d))],
        out_specs=pl.BlockSpec((tm, d), lambda i: (i, 0)),
        out_shape=jax.ShapeDtypeStruct((t, d), F32),
        scratch_shapes=[pltpu.VMEM((tm, d), BF16)],
        compiler_params=_cparams("parallel"),
        name="ffn_dense",
    )(x, mod_g, mod_g, g.reshape(1, d), mod_g, w1, w3, w2)


def _router_kernel(x_ref, sh_ref, sc_ref, g_ref, wr_ref, br_ref, h_ref, sel_ref):
    h = _ln_mod(x_ref[...], g_ref[...], sc_ref[...], sh_ref[...])
    h_ref[...] = h.reshape(h_ref.shape)
    lg = jnp.dot(h.astype(BF16), wr_ref[...], preferred_element_type=F32) + br_ref[...]
    lane = lax.broadcasted_iota(jnp.int32, lg.shape, 1).astype(F32)
    m1 = jnp.max(lg, axis=1, keepdims=True)
    i1 = jnp.min(jnp.where(lg == m1, lane, float(LANE)), axis=1, keepdims=True)
    lg2 = jnp.where(lane == i1, 2.0 * NEG, lg)
    m2 = jnp.max(lg2, axis=1, keepdims=True)
    i2 = jnp.min(jnp.where(lg2 == m2, lane, float(LANE)), axis=1, keepdims=True)
    e2 = jnp.exp(m2 - m1)
    w1 = 1.0 / (1.0 + e2)
    sel_ref[...] = (jnp.where(lane == 0.0, i1, 0.0) + jnp.where(lane == 1.0, i2, 0.0)
                    + jnp.where(lane == 2.0, w1, 0.0) + jnp.where(lane == 3.0, e2 * w1, 0.0))


def _router(x, mod_g, g, w_r, b_r):
    t, d = x.shape
    tm = _row_tile(t)
    wr = jnp.zeros((d, LANE), BF16).at[:, :N_EXPERTS].set(w_r.astype(BF16))
    br = jnp.full((1, LANE), NEG, F32).at[0, :N_EXPERTS].set(b_r.astype(F32))
    return pl.pallas_call(
        _router_kernel,
        grid=(t // tm,),
        in_specs=[pl.BlockSpec((tm, d), lambda i: (i, 0)),
                  _mod_spec(tm, 3), _mod_spec(tm, 4),
                  pl.BlockSpec((1, d), lambda i: (0, 0)),
                  pl.BlockSpec((d, LANE), lambda i: (0, 0)),
                  pl.BlockSpec((1, LANE), lambda i: (0, 0))],
        out_specs=[pl.BlockSpec((tm // SUBLANES, SUBLANES, d), lambda i: (i, 0, 0)),
                   pl.BlockSpec((tm, LANE), lambda i: (i, 0))],
        out_shape=[jax.ShapeDtypeStruct((t // SUBLANES, SUBLANES, d), F32), jax.ShapeDtypeStruct((t, LANE), F32)],
        compiler_params=_cparams("parallel"),
        name="router",
    )(x, mod_g, mod_g, g.reshape(1, d), wr, br)


def _route(sel, tmr):
    t = sel.shape[0]
    i32 = jnp.int32
    experts = jnp.arange(N_EXPERTS, dtype=i32)[None, :]
    e = sel[:, 0:2].astype(i32).reshape(-1)
    oh = (e[:, None] == experts).astype(i32)
    csum = jnp.cumsum(oh, axis=0)
    cnt = csum[-1]
    rank = jnp.sum(csum * oh, axis=1) - 1
    ntile_e = (cnt + tmr - 1) // tmr
    tile_end = jnp.cumsum(ntile_e)
    tile_start = tile_end - ntile_e
    pos = jnp.sum(oh * (tile_start * tmr)[None, :], axis=1) + rank
    n_tiles = -(-2 * t // tmr) + N_EXPERTS
    n_used = tile_end[-1]
    tiles = jnp.arange(n_tiles, dtype=i32)
    tc = jnp.minimum(tiles, n_used - 1)
    te = jnp.sum((tc[:, None] >= tile_end[None, :]).astype(i32), axis=1)
    ohe = (te[:, None] == experts).astype(i32)
    nvalid = jnp.sum(ohe * cnt[None, :], axis=1) - (tc - jnp.sum(ohe * tile_start[None, :], axis=1)) * tmr
    nvalid = jnp.where(tiles < n_used, jnp.clip(nvalid, 0, tmr), 0)
    r = jnp.arange(EXPERT_SUB, dtype=i32)[None, :]
    n_pad = ((-cnt) % EXPERT_SUB)[:, None]
    spare = n_tiles * tmr + jnp.arange(N_EXPERTS * EXPERT_SUB, dtype=i32).reshape(N_EXPERTS, EXPERT_SUB)
    pad_rows = jnp.where(r < n_pad, (tile_start * tmr + cnt)[:, None] + r, spare).reshape(-1)
    return pos.astype(i32), te.astype(i32), nvalid.astype(i32), n_used.reshape(1).astype(i32), pad_rows.astype(i32)


EXPERT_TILE = 1024
EXPERT_SUB = 256
ROUTE_TILE = 256


def _row_copies(n_groups, make_copy):
    def issue(g, carry):
        for sub in range(SUBLANES):
            for k in range(2):
                make_copy(g, sub, k).start(priority=k)
        return carry

    def drain(g, carry):
        for sub in range(SUBLANES):
            for k in range(2):
                make_copy(g, sub, k).wait()
        return carry

    lax.fori_loop(0, n_groups, issue, 0)
    lax.fori_loop(0, n_groups, drain, 0)


def _slot_row(hi_ref, lo_ref, g, sub, k):
    idx = 2 * SUBLANES * g + 2 * sub + k
    return hi_ref[0, 0, idx], lo_ref[0, 0, idx]


def _split_pos(pos, t, tm):
    hi = lax.shift_right_logical(pos, 3).reshape(t // tm, 1, 2 * tm)
    lo = jnp.bitwise_and(pos, SUBLANES - 1).reshape(t // tm, 1, 2 * tm)
    return hi, lo


def _dispatch_kernel(hi_ref, lo_ref, pad_hi_ref, pad_lo_ref, h_ref, xs_ref, zrow, sem):
    @pl.when(pl.program_id(0) == 0)
    def _():
        zrow[...] = jnp.zeros_like(zrow)

        def zero_copy(g, sub, k):
            hi, lo = _slot_row(pad_hi_ref, pad_lo_ref, g, sub, k)
            return pltpu.make_async_copy(zrow.at[pl.ds(sub, 1), :], xs_ref.at[hi, pl.ds(lo, 1), :], sem)

        _row_copies(pad_hi_ref.shape[2] // (2 * SUBLANES), zero_copy)

    def make_copy(g, sub, k):
        hi, lo = _slot_row(hi_ref, lo_ref, g, sub, k)
        return pltpu.make_async_copy(h_ref.at[g, pl.ds(sub, 1), :], xs_ref.at[hi, pl.ds(lo, 1), :], sem)

    _row_copies(h_ref.shape[0], make_copy)


def _dispatch(h, pos, pad_rows, n_rows):
    t, d = h.shape[0] * SUBLANES, h.shape[2]
    tm = min(ROUTE_TILE, _row_tile(t))
    n_pad = pad_rows.shape[0]
    smem = pl.BlockSpec((1, 1, 2 * tm), lambda i: (i, 0, 0), memory_space=pltpu.SMEM)
    smem_pad = pl.BlockSpec((1, 1, n_pad), lambda i: (0, 0, 0), memory_space=pltpu.SMEM)
    pad_hi = lax.shift_right_logical(pad_rows, 3).reshape(1, 1, n_pad)
    pad_lo = jnp.bitwise_and(pad_rows, SUBLANES - 1).reshape(1, 1, n_pad)
    return pl.pallas_call(
        _dispatch_kernel,
        grid=(t // tm,),
        in_specs=[smem, smem, smem_pad, smem_pad,
                  pl.BlockSpec((tm // SUBLANES, SUBLANES, d), lambda i: (i, 0, 0))],
        out_specs=pl.BlockSpec(memory_space=pl.ANY),
        out_shape=jax.ShapeDtypeStruct(((n_rows + n_pad) // SUBLANES, SUBLANES, d), F32),
        scratch_shapes=[pltpu.VMEM((SUBLANES, d), F32), pltpu.SemaphoreType.DMA(())],
        compiler_params=_cparams("arbitrary"),
        name="moe_dispatch",
    )(*_split_pos(pos, t, tm), pad_hi, pad_lo, h)


def _experts_kernel(te_ref, nv_ref, nu_ref, x_ref, w1_ref, w3_ref, w2_ref, y_ref, xb, acc):
    del te_ref, nu_ref
    i, j = pl.program_id(0), pl.program_id(1)
    nv = nv_ref[i]

    @pl.when(nv > 0)
    def _():
        @pl.when(j == 0)
        def _():
            acc[...] = jnp.zeros_like(acc)

        for m in range(EXPERT_SUB, EXPERT_TILE + 1, EXPERT_SUB):
            @pl.when((nv > m - EXPERT_SUB) & (nv <= m))
            def _(m=m):
                @pl.when(j == 0)
                def _():
                    xb[0:m, :] = x_ref[0:m, :].astype(BF16)

                xs = xb[0:m, :]
                a = jnp.dot(xs, w1_ref[0].astype(BF16), preferred_element_type=F32)
                b = jnp.dot(xs, w3_ref[0].astype(BF16), preferred_element_type=F32)
                tt = (a * jax.nn.sigmoid(a) * b).astype(BF16)
                acc[0:m, :] += jnp.dot(tt, w2_ref[0].astype(BF16), preferred_element_type=F32)

        @pl.when(j == pl.num_programs(1) - 1)
        def _():
            y_ref[...] = acc[...]


def _experts(xs, te, nvalid, n_used, w1, w3, w2):
    d = xs.shape[1]
    ne, _, ff = w1.shape
    tf = 512
    nj = ff // tf
    n_tiles = te.shape[0]
    n_rows = n_tiles * EXPERT_TILE
    tile_rows = (EXPERT_TILE, d)

    def row_map(i, j, te_ref, nv_ref, nu_ref):
        return (jnp.minimum(i, nu_ref[0] - 1), 0)

    def col(i, j, nu_ref):
        return jnp.where(i < nu_ref[0], j, nj - 1)

    grid_spec = pltpu.PrefetchScalarGridSpec(
        num_scalar_prefetch=3,
        grid=(n_tiles, nj),
        in_specs=[pl.BlockSpec(tile_rows, row_map),
                  pl.BlockSpec((1, d, tf), lambda i, j, te_ref, nv_ref, nu_ref: (te_ref[i], 0, col(i, j, nu_ref))),
                  pl.BlockSpec((1, d, tf), lambda i, j, te_ref, nv_ref, nu_ref: (te_ref[i], 0, col(i, j, nu_ref))),
                  pl.BlockSpec((1, tf, d), lambda i, j, te_ref, nv_ref, nu_ref: (te_ref[i], col(i, j, nu_ref), 0))],
        out_specs=pl.BlockSpec(tile_rows, row_map),
        scratch_shapes=[pltpu.VMEM((EXPERT_TILE, d), BF16), pltpu.VMEM((EXPERT_TILE, d), F32)])
    return pl.pallas_call(
        _experts_kernel,
        grid_spec=grid_spec,
        out_shape=jax.ShapeDtypeStruct((n_rows, d), F32),
        compiler_params=_cparams("arbitrary", "arbitrary"),
        name="moe_experts",
    )(te, nvalid, n_used, xs, w1, w3, w2)


def _combine_kernel(hi_ref, lo_ref, sel_ref, x_ref, g2_ref, y_ref, *rest, n_prompt_tiles):
    tm, d = x_ref.shape
    rows, sem = rest[-2:]

    def make_copy(g, sub, k):
        hi, lo = _slot_row(hi_ref, lo_ref, g, sub, k)
        return pltpu.make_async_copy(y_ref.at[hi, pl.ds(lo, 1), :], rows.at[k, g, pl.ds(sub, 1), :], sem)

    _row_copies(tm // SUBLANES, make_copy)
    sel = sel_ref[...]
    f = sel[:, 2:3] * rows[0].reshape(tm, d) + sel[:, 3:4] * rows[1].reshape(tm, d)
    f = (f.reshape(tm // CHUNK, CHUNK, d) * g2_ref[...]).reshape(tm, d)
    xo = x_ref[...] + f
    if n_prompt_tiles is None:
        rest[0][...] = xo
        return
    gf_ref, op_ref, os_ref = rest[:3]
    yo = xo * lax.rsqrt(jnp.mean(xo * xo, axis=-1, keepdims=True) + EPS) * gf_ref[...]
    i = pl.program_id(0)

    @pl.when(i < n_prompt_tiles)
    def _():
        op_ref[...] = yo

    @pl.when(i >= n_prompt_tiles)
    def _():
        os_ref[...] = yo


def _combine(y, pos, sel, x, mod_g, final=None):
    t, d = x.shape
    tm = min(ROUTE_TILE, _row_tile(t if final is None else np.gcd(final[1], t - final[1])))
    smem = pl.BlockSpec((1, 1, 2 * tm), lambda i: (i, 0, 0), memory_space=pltpu.SMEM)
    in_specs = [smem, smem,
                pl.BlockSpec((tm, LANE), lambda i: (i, 0)),
                pl.BlockSpec((tm, d), lambda i: (i, 0)),
                _mod_spec(tm, 5),
                pl.BlockSpec(memory_space=pl.ANY)]
    args = [*_split_pos(pos, t, tm), sel, x, mod_g, y]
    if final is None:
        n_p = None
        out_specs = pl.BlockSpec((tm, d), lambda i: (i, 0))
        out_shape = jax.ShapeDtypeStruct((t, d), F32)
    else:
        gain, tp = final
        n_p = tp // tm
        in_specs.append(pl.BlockSpec((1, d), lambda i: (0, 0)))
        args.append(gain.reshape(1, d))
        out_specs = [pl.BlockSpec((tm, d), lambda i: (jnp.minimum(i, n_p - 1), 0)),
                     pl.BlockSpec((tm, d), lambda i: (jnp.maximum(i - n_p, 0), 0))]
        out_shape = [jax.ShapeDtypeStruct((tp, d), F32), jax.ShapeDtypeStruct((t - tp, d), F32)]
    return pl.pallas_call(
        functools.partial(_combine_kernel, n_prompt_tiles=n_p),
        grid=(t // tm,),
        in_specs=in_specs,
        out_specs=out_specs,
        out_shape=out_shape,
        scratch_shapes=[pltpu.VMEM((2, tm // SUBLANES, SUBLANES, d), F32), pltpu.SemaphoreType.DMA(())],
        compiler_params=_cparams("arbitrary"),
        name="moe_combine",
    )(*args)


def _moe(x, mod_g, g, w_r, b_r, w1, w3, w2, final=None):
    d = x.shape[1]
    h2, sel = _router(x, mod_g, g, w_r, b_r)
    pos, te, nvalid, n_used, pad_rows = _route(sel, EXPERT_TILE)
    n_rows = te.shape[0] * EXPERT_TILE
    xs = _dispatch(h2, pos, pad_rows, n_rows)
    y = _experts(xs.reshape(-1, d), te, nvalid, n_used, w1, w3, w2)
    return _combine(y.reshape(n_rows // SUBLANES, SUBLANES, d), pos, sel, x, mod_g, final)


def _final_kernel(x_ref, g_ref, op_ref, os_ref, *, n_prompt_tiles):
    x = x_ref[...]
    y = x * lax.rsqrt(jnp.mean(x * x, axis=-1, keepdims=True) + EPS) * g_ref[...]
    i = pl.program_id(0)

    @pl.when(i < n_prompt_tiles)
    def _():
        op_ref[...] = y

    @pl.when(i >= n_prompt_tiles)
    def _():
        os_ref[...] = y


def _final_norm(x, g, tp):
    t, d = x.shape
    tm = _row_tile(np.gcd(tp, t - tp))
    n_p = tp // tm
    return pl.pallas_call(
        functools.partial(_final_kernel, n_prompt_tiles=n_p),
        grid=(t // tm,),
        in_specs=[pl.BlockSpec((tm, d), lambda i: (i, 0)), pl.BlockSpec((1, d), lambda i: (0, 0))],
        out_specs=[pl.BlockSpec((tm, d), lambda i: (jnp.minimum(i, n_p - 1), 0)),
                   pl.BlockSpec((tm, d), lambda i: (jnp.maximum(i - n_p, 0), 0))],
        out_shape=[jax.ShapeDtypeStruct((tp, d), F32), jax.ShapeDtypeStruct((t - tp, d), F32)],
        compiler_params=_cparams("arbitrary"),
        name="final_norm",
    )(x, g.reshape(1, d))


def _split_w_in(w):
    o = np.cumsum([0, WA, WA, WA, 2 * WB_QK, WB_V, 2 * H_B, WB_V, WC_Q, WC_KV, WC_KV, 3 * D_MODEL])
    seg = lambda k: w[:, o[k]:o[k + 1]]
    qa, ka, va, qkb, vb, ifb, ob, qc, kc, vc, gates = (seg(k) for k in range(11))
    main = jnp.concatenate([qa, ka, va, qc, qkb, kc, vc, gates, vb, ob], axis=1).astype(BF16)
    return main, ifb.T.astype(BF16)


def kernel(x_prompt, x_sample, c_prompt, c_sample, cache_a_k, cache_a_v, state_b_C, state_b_n, state_b_m, state_b_conv, cache_c_k, cache_c_v, norm1_g, norm2_g, w_ada, b_ada, w_in, b_if_b, conv_w_b, conv_b_b, norm_b_g, rel_a, sink_c, w_br_a, w_br_b, w_br_c, w_o, w_ff1, w_ff3, w_ff2, w_router, b_router, w_e1, w_e3, w_e2, norm_f_g):
    nbp, sp, d = x_prompt.shape
    nbs, ss, _ = x_sample.shape
    tp, ts = nbp * sp, nbs * ss
    x = jnp.concatenate([x_prompt.reshape(tp, d), x_sample.reshape(ts, d)], axis=0)
    cond = jnp.concatenate([c_prompt, c_sample], axis=0)
    group_batch = np.concatenate([np.repeat(np.arange(nbp), sp // CHUNK), nbp + np.repeat(np.arange(nbs), ss // CHUNK)])

    lp_s = max(ss, LANE)
    l_b = 256 if sp % 256 == 0 else CHUNK
    keep_a, keep_c = min(PRE_A, sp), min(PRE_C, sp)
    zeros_state = (jnp.zeros((nbp, H_B, DK_B, DV_B), F32), jnp.zeros((nbp, H_B, DK_B), F32), jnp.zeros((nbp, H_B), F32))
    new = {k: [] for k in ("pak", "pav", "pbc", "pbn", "pbm", "pbx", "pck", "pcv",
                           "sak", "sav", "sbc", "sbn", "sbm", "sbx", "sck", "scv")}

    for l in range(DEPTH):
        mod = _ada(cond, w_ada, b_ada, l)
        mod_g = mod[group_batch].reshape(-1, 1, 6 * d)
        w_main, w_if_t = _split_w_in(w_in[l])
        u, ub, g_t = _inproj(x, mod_g, norm1_g[l], w_main, w_if_t)

        bias_t = _rel_bias_table(rel_a[l])
        ya, pak, pav = _attn_a(u, bias_t, nb=nbp, s=sp, row0=0)
        ya, sak, sav = _attn_a(u, bias_t, nb=nbs, s=ss, row0=tp,
                               prefix=(cache_a_k[l].reshape(nbs, PRE_A, WA), cache_a_v[l].reshape(nbs, PRE_A, WA)),
                               out=ya)
        yc, pck, pcv = _attn_c(u, sink_c[l], nb=nbp, s=sp, row0=0, pos0=0)
        yc, sck, scv = _attn_c(u, sink_c[l], nb=nbs, s=ss, row0=tp, pos0=PAST_LEN,
                               prefix=(cache_c_k[l].reshape(nbs, PRE_C, WC_KV),
                                       cache_c_v[l].reshape(nbs, PRE_C, WC_KV)), out=yc)
        g_p = g_t[:, :tp].reshape(2 * H_B, nbp, sp).transpose(1, 0, 2)
        g_s = g_t[:, tp:].reshape(2 * H_B, nbs, ss).transpose(1, 0, 2)
        g_s = jnp.pad(g_s, ((0, 0), (0, 0), (0, lp_s - ss)))
        hist_p = jnp.zeros((nbp, 8, 2 * WB_QK), F32)
        hist_s = jnp.pad(state_b_conv[l], ((0, 0), (8 - (CONV_B - 1), 0), (0, 0)))
        bargs = (conv_w_b[l], conv_b_b[l], b_if_b[l], norm_b_g[l])
        yb, pbc, pbn, pbm, pbx = _mlstm(u, ub, g_p, hist_p, *bargs, *zeros_state, nb=nbp, s=sp, row0=0, L=l_b)
        yb, sbc, sbn, sbm, sbx = _mlstm(u, ub, g_s, hist_s, *bargs, state_b_C[l], state_b_n[l], state_b_m[l],
                                        nb=nbs, s=ss, row0=tp, L=ss, out=yb)

        x = _merge(ya, yb, yc, ub, x, mod_g, w_br_a[l].astype(BF16), w_br_b[l].astype(BF16),
                   w_br_c[l].astype(BF16), w_o[l].astype(BF16))
        i = l // 2
        if l % 2 == 0:
            x = _ffn(x, mod_g, norm2_g[l], w_ff1[i].astype(BF16), w_ff3[i].astype(BF16), w_ff2[i].astype(BF16))
        else:
            x = _moe(x, mod_g, norm2_g[l], w_router[i], b_router[i], w_e1[i], w_e3[i], w_e2[i],
                     final=(norm_f_g, tp) if l == DEPTH - 1 else None)

        for name, val in (("pak", pak), ("pav", pav), ("pbc", pbc), ("pbn", pbn), ("pbm", pbm), ("pbx", pbx),
                          ("pck", pck), ("pcv", pcv), ("sak", sak), ("sav", sav), ("sbc", sbc), ("sbn", sbn),
                          ("sbm", sbm), ("sbx", sbx), ("sck", sck), ("scv", scv)):
            new[name].append(val)

    y_p, y_s = x if DEPTH % 2 == 0 else _final_norm(x, norm_f_g, tp)
    heads = {"pak": (H_A, DH_A), "pav": (H_A, DH_A), "sak": (H_A, DH_A), "sav": (H_A, DH_A),
             "pck": (HKV_C, DH_C), "pcv": (HKV_C, DH_C), "sck": (HKV_C, DH_C), "scv": (HKV_C, DH_C)}
    st = {k: jnp.stack(v) for k, v in new.items()}
    st = {k: v.reshape(v.shape[:-1] + heads[k]) if k in heads else v for k, v in st.items()}
    return (y_p.reshape(nbp, sp, d), y_s.reshape(nbs, ss, d),
            st["pak"], st["pav"], st["pbc"], st["pbn"], st["pbm"], st["pbx"], st["pck"], st["pcv"],
            st["sak"], st["sav"], st["sbc"], st["sbn"], st["sbm"], st["sbx"], st["sck"], st["scv"])
```

```python
import functools

import numpy as np
import jax
import jax.numpy as jnp
from jax import lax
from jax.experimental import pallas as pl
from jax.experimental.pallas import tpu as pltpu

F32 = jnp.float32
BF16 = jnp.bfloat16

D_MODEL = 1024
DEPTH = 2
PAST_LEN = 2048
CHUNK = 64
H_A, DH_A, BACK_A, REL_CLIP = 8, 64, 8, 128
H_B, DK_B, DV_B, CONV_B = 4, 128, 256, 4
HQ_C, HKV_C, DH_C, BACK_C = 8, 2, 64, 2
ROT_DIM = DH_C // 4
ROPE_THETA = 500000.0
D_FF = 2816
N_EXPERTS = 8
D_FF_E = 3584
EPS = 1e-6
NEG = -1e30

WA = H_A * DH_A
WB_QK = H_B * DK_B
WB_V = H_B * DV_B
WC_Q = HQ_C * DH_C
WC_KV = HKV_C * DH_C
PRE_A = BACK_A * CHUNK
PRE_C = BACK_C * CHUNK

COL_QA, COL_KA, COL_VA, COL_QC = 0, 512, 1024, 1536
COL_QKB = 2048
COL_KC = 3072
COL_VC = 3200
N_F32 = 3328
COLB_GATES = 0
COLB_VB = 3072
COLB_OB = 4096
N_B16 = 5120
N_MAIN = N_F32 + N_B16

LANE = 128
SUBLANES = 8
VMEM_LIMIT = 48 * 1024 * 1024
INPROJ_VMEM_SLACK = 8 * 1024 * 1024


def _cparams(*sem, vmem=VMEM_LIMIT):
    return pltpu.CompilerParams(dimension_semantics=sem, vmem_limit_bytes=vmem)


def _row_tile(t):
    for tm in (512, 256, 128, 64):
        if t % tm == 0:
            return tm
    raise ValueError(f"token count {t} is not a multiple of 64")


def _ln_mod(x, g, sc, sh):
    tm, d = x.shape
    y = x * lax.rsqrt(jnp.mean(x * x, axis=-1, keepdims=True) + EPS) * g
    y = y.reshape(tm // CHUNK, CHUNK, d) * (1.0 + sc) + sh
    return y.reshape(tm, d)


def _mod_spec(tm, kind):
    return pl.BlockSpec((tm // CHUNK, 1, D_MODEL), lambda i, *_, k=kind: (i, 0, k))


def _ada_kernel(c_ref, w_ref, b_ref, o_ref):
    c = c_ref[...]
    a = (c * jax.nn.sigmoid(c)).astype(BF16)
    o_ref[...] = jnp.dot(a, w_ref[...].astype(BF16), preferred_element_type=F32) + b_ref[...]


def _ada(c, w, b, layer):
    nb, d = c.shape
    n = w.shape[2]
    tn = 1536
    return pl.pallas_call(
        _ada_kernel,
        grid=(n // tn,),
        in_specs=[pl.BlockSpec((nb, d), lambda j: (0, 0)),
                  pl.BlockSpec((None, d, tn), lambda j: (layer, 0, j)),
                  pl.BlockSpec((None, 1, tn), lambda j: (layer, 0, j))],
        out_specs=pl.BlockSpec((nb, tn), lambda j: (0, j)),
        out_shape=jax.ShapeDtypeStruct((nb, n), F32),
        compiler_params=_cparams("arbitrary"),
        name="adaln",
    )(c, w, b.reshape(b.shape[0], 1, n))


def _col_chunks(n, width=768):
    return [(c, min(width, n - c)) for c in range(0, n, width)]


def _resident(shape):
    return pl.BlockSpec(shape, lambda *_: (0,) * len(shape), pipeline_mode=pl.Buffered(1))


def _inproj_kernel(x_ref, sh_ref, sc_ref, g_ref, w_ref, wif_ref, u_ref, ub_ref, gt_ref, h_scr):
    h_scr[...] = _ln_mod(x_ref[...], g_ref[...], sc_ref[...], sh_ref[...]).astype(BF16)
    gt_ref[...] = lax.dot_general(wif_ref[...], h_scr[...], (((1,), (1,)), ((), ())), preferred_element_type=F32)
    for c, w in _col_chunks(N_F32):
        u_ref[:, c:c + w] = jnp.dot(h_scr[...], w_ref[:, c:c + w], preferred_element_type=F32)
    for c, w in _col_chunks(N_B16):
        ub_ref[:, c:c + w] = jnp.dot(h_scr[...], w_ref[:, N_F32 + c:N_F32 + c + w],
                                     preferred_element_type=F32).astype(BF16)


def _inproj(x, mod_g, g, w_main, w_if_t):
    t, d = x.shape
    tm = _row_tile(t)
    vmem = (d * N_MAIN * 2 + 2 * tm * (d * 4 + N_F32 * 4 + N_B16 * 2) + tm * d * 2) + INPROJ_VMEM_SLACK
    return pl.pallas_call(
        _inproj_kernel,
        grid=(t // tm,),
        in_specs=[pl.BlockSpec((tm, d), lambda i: (i, 0)),
                  _mod_spec(tm, 0), _mod_spec(tm, 1),
                  _resident((1, d)), _resident((d, N_MAIN)), _resident((2 * H_B, d))],
        out_specs=[pl.BlockSpec((tm, N_F32), lambda i: (i, 0)),
                   pl.BlockSpec((tm, N_B16), lambda i: (i, 0)),
                   pl.BlockSpec((2 * H_B, tm), lambda i: (0, i))],
        out_shape=[jax.ShapeDtypeStruct((t, N_F32), F32),
                   jax.ShapeDtypeStruct((t, N_B16), BF16),
                   jax.ShapeDtypeStruct((2 * H_B, t), F32)],
        scratch_shapes=[pltpu.VMEM((tm, d), BF16)],
        compiler_params=_cparams("parallel", vmem=vmem),
        name="inproj",
    )(x, mod_g, mod_g, g.reshape(1, d), w_main, w_if_t)


def _band_table(bq, pre, back):
    qi = np.arange(bq)[:, None] // CHUNK
    kj = np.arange(pre + bq)[None, :] // CHUNK
    return (kj >= qi) & (kj <= qi + back)


QBLK = 128
STEP_QBLKS = 2
KV_CHUNK = 512


def _step_rows(s):
    return STEP_QBLKS * QBLK if s % (STEP_QBLKS * QBLK) == 0 else min(QBLK, s)


def _pad_rows(x, rows):
    return x if x.shape[0] == rows else jnp.concatenate(
        [x, jnp.zeros((rows - x.shape[0], x.shape[1]), x.dtype)], axis=0)


def _stage_keys_values(k_rows, v_ref, prefix_k, prefix_v, kpad, vt, pre, s):
    w = kpad.shape[1]
    s_pad = kpad.shape[0] - pre
    if prefix_k is None:
        kpad[0:pre, :] = jnp.zeros((pre, w), BF16)
        vt[:, 0:pre] = jnp.zeros((w, pre), BF16)
    else:
        kpad[0:pre, :] = prefix_k.astype(BF16)
        vt[:, 0:pre] = prefix_v.T.astype(BF16)
    for c in range(0, s_pad, KV_CHUNK):
        rows = min(KV_CHUNK, s_pad - c)
        real = max(0, min(rows, s - c))
        kpad[pre + c:pre + c + rows, :] = _pad_rows(k_rows(c, real), rows).astype(BF16)
        vt[:, pre + c:pre + c + rows] = _pad_rows(v_ref[c:c + real, :], rows).T.astype(BF16)


def _pair_store(o_ref, ots, lane0, row0, rows):
    pair = jnp.concatenate(ots, axis=0).T
    o_ref[row0:row0 + rows, lane0:lane0 + pair.shape[1]] = pair[0:rows, :].astype(o_ref.dtype)


def _query_blocks(q_ref):
    rows = q_ref.shape[0]
    return max(1, rows // QBLK), min(QBLK, rows)


def _attn_a_kernel(*refs, s, has_prefix):
    if has_prefix:
        q_ref, k_ref, v_ref, pk_ref, pv_ref, bias_ref, o_ref, kt_ref, vt_ref, kpad, vt = refs
    else:
        q_ref, k_ref, v_ref, bias_ref, o_ref, kt_ref, vt_ref, kpad, vt = refs
    n = pl.program_id(1)
    kw = PRE_A + QBLK
    keep = kt_ref.shape[1]
    nsub, sq = _query_blocks(q_ref)

    @pl.when(n == 0)
    def _():
        _stage_keys_values(lambda c, rows: k_ref[c:c + rows, :], v_ref,
                           pk_ref[0] if has_prefix else None, pv_ref[0] if has_prefix else None,
                           kpad, vt, PRE_A, s)
        kt_ref[0] = k_ref[s - keep:s, :]
        vt_ref[0] = v_ref[s - keep:s, :]

    base = pl.multiple_of(n * (nsub * QBLK), nsub * QBLK)
    heads = [slice(h * DH_A, (h + 1) * DH_A) for h in range(H_A)]
    nt = (((1,), (1,)), ((), ()))

    def attend(mask_keys):
        for sub in range(nsub):
            r0 = base + sub * QBLK
            q = _pad_rows((q_ref[sub * sq:(sub + 1) * sq, :] * (DH_A ** -0.5)).astype(BF16), QBLK)
            scores = [lax.dot_general(kpad[pl.ds(r0, kw), hs], q[:, hs], nt, preferred_element_type=F32)
                      for hs in heads]
            if mask_keys:
                key = lax.broadcasted_iota(jnp.int32, (kw, QBLK), 0)
                invalid = jnp.where(key >= PRE_A - r0, 0.0, NEG).astype(F32)
            probs, sums = [], []
            for h in range(H_A):
                sc = scores[h] + bias_ref[h]
                if mask_keys:
                    sc = sc + invalid
                e = jnp.exp(sc - jnp.max(sc, axis=0, keepdims=True))
                sums.append(jnp.sum(e, axis=0, keepdims=True))
                probs.append(e.astype(BF16))
            for h in range(0, H_A, 2):
                ots = [jnp.dot(vt[heads[i], pl.ds(r0, kw)], probs[i], preferred_element_type=F32) / sums[i]
                       for i in (h, h + 1)]
                _pair_store(o_ref, ots, h * DH_A, sub * sq, sq)

    if has_prefix:
        attend(False)
    else:
        pl.when(base < PRE_A)(lambda: attend(True))
        pl.when(base >= PRE_A)(lambda: attend(False))


def _rel_bias_table(rel):
    kw = PRE_A + QBLK
    j0 = PRE_A - REL_CLIP
    rows = kw - j0
    n = QBLK + rows
    m = np.arange(n)
    relf = rel.astype(F32)
    diag = relf[np.clip(m + 1 - QBLK, -REL_CLIP, REL_CLIP) + REL_CLIP].T
    skew = jnp.tile(diag, (1, rows + 1))[:, :rows * (n - 1)].reshape(H_A, rows, n - 1)
    far = jnp.broadcast_to(relf[2 * REL_CLIP][:, None, None], (H_A, j0, QBLK))
    table = jnp.concatenate([far, skew[:, :, rows - 1:rows - 1 + QBLK]], axis=1)
    return jnp.where(_band_table(QBLK, PRE_A, BACK_A).T[None], table, NEG)


def _attn_a(u, bias_t, *, nb, s, row0, prefix=None, out=None):
    t = u.shape[0]
    bq = _step_rows(s)
    nq = s // bq
    s_pad = max(s, QBLK)
    kw = PRE_A + QBLK
    keep = min(PRE_A, s)
    rb, sb = row0 // bq, row0 // s
    in_specs = [pl.BlockSpec((bq, WA), lambda b, n: (rb + b * nq + n, COL_QA // WA)),
                pl.BlockSpec((s, WA), lambda b, n: (sb + b, COL_KA // WA)),
                pl.BlockSpec((s, WA), lambda b, n: (sb + b, COL_VA // WA))]
    args = [u, u, u]
    if prefix is not None:
        in_specs += [pl.BlockSpec((1, PRE_A, WA), lambda b, n: (b, 0, 0))] * 2
        args += [prefix[0], prefix[1]]
    in_specs.append(pl.BlockSpec((H_A, kw, QBLK), lambda b, n: (0, 0, 0)))
    args.append(bias_t)
    aliases = {}
    if out is not None:
        in_specs.append(pl.BlockSpec(memory_space=pl.ANY))
        args.append(out)
        aliases = {len(args) - 1: 0}
    body = functools.partial(_attn_a_kernel, s=s, has_prefix=prefix is not None)
    if out is not None:
        body = _drop_last_input(body, len(args))
    return pl.pallas_call(
        body,
        grid=(nb, nq),
        in_specs=in_specs,
        out_specs=[pl.BlockSpec((bq, WA), lambda b, n: (rb + b * nq + n, 0)),
                   pl.BlockSpec((1, keep, WA), lambda b, n: (b, 0, 0)),
                   pl.BlockSpec((1, keep, WA), lambda b, n: (b, 0, 0))],
        out_shape=[jax.ShapeDtypeStruct((t, WA), BF16),
                   jax.ShapeDtypeStruct((nb, keep, WA), F32),
                   jax.ShapeDtypeStruct((nb, keep, WA), F32)],
        scratch_shapes=[pltpu.VMEM((PRE_A + s_pad, WA), BF16), pltpu.VMEM((WA, PRE_A + s_pad), BF16)],
        input_output_aliases=aliases,
        compiler_params=_cparams("parallel", "arbitrary"),
        name="mixer_a_sample" if prefix is not None else "mixer_a_prompt",
    )(*args)


def _drop_last_input(body, n_in):
    def wrapped(*refs):
        return body(*refs[:n_in - 1], *refs[n_in:])
    return wrapped


def _rope(x, c, s1, s2):
    w = x.shape[1]
    return x * c + pltpu.roll(x, 8, 1) * s1 + pltpu.roll(x, w - 8, 1) * s2


def _attn_c_kernel(*refs, s, has_prefix):
    if has_prefix:
        (q_ref, k_ref, v_ref, pk_ref, pv_ref, rc_ref, rs1_ref, rs2_ref, band_ref, sink_ref,
         o_ref, kt_ref, vt_ref, kpad, vt) = refs
    else:
        (q_ref, k_ref, v_ref, rc_ref, rs1_ref, rs2_ref, band_ref, sink_ref,
         o_ref, kt_ref, vt_ref, kpad, vt) = refs
    n = pl.program_id(1)
    kw = PRE_C + QBLK
    keep = kt_ref.shape[1]
    nsub, sq = _query_blocks(q_ref)

    def rope_at(x, start, rep=1):
        tables = [jnp.concatenate([t_ref[pl.ds(start, x.shape[0]), :]] * rep, axis=1)
                  for t_ref in (rc_ref, rs1_ref, rs2_ref)]
        return _rope(x, *tables)

    @pl.when(n == 0)
    def _():
        _stage_keys_values(lambda c, rows: rope_at(k_ref[c:c + rows, :], c), v_ref,
                           pk_ref[0] if has_prefix else None, pv_ref[0] if has_prefix else None,
                           kpad, vt, PRE_C, s)
        kt_ref[0] = rope_at(k_ref[s - keep:s, :], s - keep)
        vt_ref[0] = v_ref[s - keep:s, :]

    base = pl.multiple_of(n * (nsub * QBLK), nsub * QBLK)
    per = HQ_C // HKV_C
    groups = [slice(g * DH_C, (g + 1) * DH_C) for g in range(HKV_C)]
    nt = (((1,), (1,)), ((), ()))

    def attend(mask_keys):
        for sub in range(nsub):
            r0 = base + sub * QBLK
            q = rope_at(q_ref[sub * sq:(sub + 1) * sq, :], pl.multiple_of(n * (nsub * sq) + sub * sq, sq),
                        rep=WC_Q // WC_KV)
            q = _pad_rows((q * (DH_C ** -0.5)).astype(BF16), QBLK)
            scores = []
            for g, gs in enumerate(groups):
                qs = jnp.concatenate([q[:, (g * per + i) * DH_C:(g * per + i + 1) * DH_C] for i in range(per)],
                                     axis=0)
                scores.append(lax.dot_general(kpad[pl.ds(r0, kw), gs], qs, nt,
                                              preferred_element_type=F32))
            mask = band_ref[...]
            if mask_keys:
                key = lax.broadcasted_iota(jnp.int32, mask.shape, 0)
                mask = mask + jnp.where(key >= PRE_C - r0, 0.0, NEG).astype(F32)
            probs, sums = [], []
            for g in range(HKV_C):
                sk = jnp.concatenate([jnp.broadcast_to(sink_ref[g * per + i:g * per + i + 1, 0:1], (1, QBLK))
                                      for i in range(per)], axis=1)
                sc = scores[g] + mask
                m = jnp.maximum(jnp.max(sc, axis=0, keepdims=True), sk)
                e = jnp.exp(sc - m)
                sums.append(jnp.sum(e, axis=0, keepdims=True) + jnp.exp(sk - m))
                probs.append(e.astype(BF16))
            for g, gs in enumerate(groups):
                ot = jnp.dot(vt[gs, pl.ds(r0, kw)], probs[g], preferred_element_type=F32) / sums[g]
                for i in range(0, per, 2):
                    _pair_store(o_ref, [ot[:, (i + j) * QBLK:(i + j + 1) * QBLK] for j in (0, 1)],
                                (g * per + i) * DH_C, sub * sq, sq)

    if has_prefix:
        attend(False)
    else:
        pl.when(base < PRE_C)(lambda: attend(True))
        pl.when(base >= PRE_C)(lambda: attend(False))


def _rope_tables(pos):
    half = ROT_DIM // 2
    inv_freq = 1.0 / (ROPE_THETA ** (jnp.arange(half, dtype=F32) * (2.0 / ROT_DIM)))
    ang = pos.astype(F32)[:, None] * inv_freq[None, :]
    cos, sin = jnp.cos(ang), jnp.sin(ang)
    n = pos.shape[0]
    one = jnp.ones((n, DH_C - ROT_DIM), F32)
    zero = jnp.zeros((n, DH_C - ROT_DIM), F32)
    zh = jnp.zeros((n, half), F32)
    c = jnp.concatenate([cos, cos, one], axis=1)
    s1 = jnp.concatenate([zh, sin, zero], axis=1)
    s2 = jnp.concatenate([-sin, zh, zero], axis=1)
    rep = WC_KV // DH_C
    return tuple(jnp.concatenate([a] * rep, axis=1) for a in (c, s1, s2))


def _attn_c(u, sink, *, nb, s, row0, pos0, prefix=None, out=None):
    t = u.shape[0]
    bq = _step_rows(s)
    nq = s // bq
    s_pad = max(s, QBLK)
    kw = PRE_C + QBLK
    per = HQ_C // HKV_C
    tables = _rope_tables(pos0 + jnp.arange(s))
    band = np.where(_band_table(QBLK, PRE_C, BACK_C), 0.0, NEG).astype(np.float32)
    band = jnp.asarray(np.tile(band.T, (1, per)))
    sink_t = jnp.broadcast_to(sink.astype(F32)[:, None], (HQ_C, LANE))
    rb, sb = row0 // bq, row0 // s
    in_specs = [pl.BlockSpec((bq, WC_Q), lambda b, n: (rb + b * nq + n, COL_QC // WC_Q)),
                pl.BlockSpec((s, WC_KV), lambda b, n: (sb + b, COL_KC // WC_KV)),
                pl.BlockSpec((s, WC_KV), lambda b, n: (sb + b, COL_VC // WC_KV))]
    args = [u, u, u]
    if prefix is not None:
        in_specs += [pl.BlockSpec((1, PRE_C, WC_KV), lambda b, n: (b, 0, 0))] * 2
        args += [prefix[0], prefix[1]]
    in_specs += [pl.BlockSpec((s, WC_KV), lambda b, n: (0, 0))] * 3
    args += list(tables)
    in_specs += [pl.BlockSpec((kw, per * QBLK), lambda b, n: (0, 0)),
                 pl.BlockSpec((HQ_C, LANE), lambda b, n: (0, 0))]
    args += [band, sink_t]
    aliases = {}
    body = functools.partial(_attn_c_kernel, s=s, has_prefix=prefix is not None)
    if out is not None:
        in_specs.append(pl.BlockSpec(memory_space=pl.ANY))
        args.append(out)
        aliases = {len(args) - 1: 0}
        body = _drop_last_input(body, len(args))
    keep = min(PRE_C, s)
    return pl.pallas_call(
        body,
        grid=(nb, nq),
        in_specs=in_specs,
        out_specs=[pl.BlockSpec((bq, WC_Q), lambda b, n: (rb + b * nq + n, 0)),
                   pl.BlockSpec((1, keep, WC_KV), lambda b, n: (b, 0, 0)),
                   pl.BlockSpec((1, keep, WC_KV), lambda b, n: (b, 0, 0))],
        out_shape=[jax.ShapeDtypeStruct((t, WC_Q), BF16),
                   jax.ShapeDtypeStruct((nb, keep, WC_KV), F32),
                   jax.ShapeDtypeStruct((nb, keep, WC_KV), F32)],
        scratch_shapes=[pltpu.VMEM((PRE_C + s_pad, WC_KV), BF16), pltpu.VMEM((WC_KV, PRE_C + s_pad), BF16)],
        input_output_aliases=aliases,
        compiler_params=_cparams("parallel", "arbitrary"),
        name="mixer_c_sample" if prefix is not None else "mixer_c_prompt",
    )(*args)


def _sigmoid(x):
    return 0.5 * jnp.tanh(0.5 * x) + 0.5


def _log_sigmoid(x):
    return jnp.minimum(x, 0.0) - jnp.log(1.0 + jnp.exp(-jnp.abs(x)))


def _lane_cumsum(x):
    n = x.shape[1]
    col = lax.broadcasted_iota(jnp.int32, x.shape, 1)
    k = 1
    while k < n:
        x = x + jnp.where(col >= k, pltpu.roll(x, k, 1), 0.0)
        k *= 2
    return x


def _mlstm_kernel(qk_ref, v_ref, og_ref, g_ref, hist_ref, cw_ref, cb_ref, bif_ref, ng_ref,
                  c0_ref, n0_ref, m0_ref, y_ref, cf_ref, nf_ref, mf_ref, xt_ref,
                  xpad, c_s, n_s, m_s, *, L):
    c = pl.program_id(1)

    @pl.when(c == 0)
    def _():
        c_s[...] = c0_ref[0]
        n_s[...] = n0_ref[0]
        m_s[...] = m0_ref[0]
        xpad[0:8, :] = hist_ref[0]

    x = qk_ref[...]
    xpad[8:8 + L, :] = x
    conv = (cb_ref[...] + x * cw_ref[3:4, :] + xpad[7:7 + L, :] * cw_ref[2:3, :]
            + xpad[6:6 + L, :] * cw_ref[1:2, :] + xpad[5:5 + L, :] * cw_ref[0:1, :])
    xpad[0:8, :] = xpad[L:L + 8, :]
    act = conv * _sigmoid(conv)

    gb = g_ref[...] + bif_ref[:, 0:1]
    b_all = _lane_cumsum(_log_sigmoid(gb))
    ig_rows = gb[0:H_B, :]
    b_rows = b_all[H_B:2 * H_B, :]
    both = jnp.where(lax.broadcasted_iota(jnp.int32, gb.shape, 0) < H_B, gb, b_all)
    cols = jnp.concatenate([both, jnp.zeros((LANE - 2 * H_B, gb.shape[1]), F32)], axis=0).T

    row = lax.broadcasted_iota(jnp.int32, (L, L), 0)
    colm = lax.broadcasted_iota(jnp.int32, (L, L), 1)
    causal = colm <= row

    heads = range(H_B)
    nt = (((1,), (1,)), ((), ()))
    qf = [act[:, h * DK_B:(h + 1) * DK_B] for h in heads]
    kf = [act[:, WB_QK + h * DK_B:WB_QK + (h + 1) * DK_B] * (DK_B ** -0.5) for h in heads]
    qb = [q.astype(BF16) for q in qf]
    vb = [v_ref[:, h * DV_B:(h + 1) * DV_B].astype(BF16) for h in heads]
    cmat = [c_s[h] for h in heads]
    nvec = [n_s[h] for h in heads]
    qk = [lax.dot_general(qb[h], kf[h].astype(BF16), nt, preferred_element_type=F32) for h in heads]
    qc = [jnp.dot(qb[h], cmat[h].astype(BF16), preferred_element_type=F32) for h in heads]

    a, wi, mt, bc, igc = [], [], [], [], []
    for h in heads:
        br = b_rows[h:h + 1, 0:L]
        igr = ig_rows[h:h + 1, 0:L]
        bc.append(cols[0:L, H_B + h:H_B + h + 1])
        igc.append(cols[0:L, h:h + 1])
        logd = jnp.where(causal, bc[h] + (igr - br), NEG)
        li = bc[h] + m_s[h][:, 0:1]
        mt.append(jnp.maximum(li, jnp.max(logd, axis=1, keepdims=True)))
        wi.append(jnp.exp(li - mt[h]))
        a.append(qk[h] * jnp.exp(logd - mt[h]))

    pv = [jnp.dot(a[h].astype(BF16), vb[h], preferred_element_type=F32) for h in heads]
    for h in heads:
        num = pv[h] + wi[h] * qc[h]
        den = jnp.sum(a[h], axis=1, keepdims=True) + wi[h] * jnp.sum(qf[h] * nvec[h], axis=1, keepdims=True)
        hh = num / jnp.maximum(jnp.abs(den), jnp.exp(-mt[h]))
        hn = hh * lax.rsqrt(jnp.mean(hh * hh, axis=-1, keepdims=True) + EPS)
        vs = slice(h * DV_B, (h + 1) * DV_B)
        y_ref[:, vs] = (hn * ng_ref[:, vs] * _sigmoid(og_ref[:, vs].astype(F32))).astype(y_ref.dtype)

    for h in heads:
        bl, ml, wprev = bc[h][L - 1:L, :], mt[h][L - 1:L, :], wi[h][L - 1:L, :]
        kwt = kf[h] * jnp.exp(bl + igc[h] - bc[h] - ml)
        c_s[h] = wprev * cmat[h] + lax.dot_general(kwt.astype(BF16), vb[h], (((0,), (0,)), ((), ())),
                                                   preferred_element_type=F32)
        n_s[h] = wprev * nvec[h] + jnp.sum(kwt, axis=0, keepdims=True)
        m_s[h] = jnp.broadcast_to(ml, (1, LANE))

    @pl.when(c == pl.num_programs(1) - 1)
    def _():
        cf_ref[0] = c_s[...]
        nf_ref[0] = n_s[...]
        mf_ref[0] = m_s[...]
        xt_ref[0] = xpad[0:8, :]


def _mlstm(u, ub, g_rows, hist, cw, cb, bif, ng, c0, n0, m0, *, nb, s, row0, L, out=None):
    t = u.shape[0]
    nc = s // L
    lp = g_rows.shape[2] // nc
    rb = row0 // L
    n0 = n0.reshape(nb, H_B, 1, DK_B)
    m0 = jnp.broadcast_to(m0.reshape(nb, H_B, 1, 1), (nb, H_B, 1, LANE))
    bif_t = jnp.broadcast_to(bif.astype(F32)[:, None], (2 * H_B, LANE))
    row_spec = lambda col: pl.BlockSpec((L, WB_V), lambda b, c: (rb + b * nc + c, col))
    full = lambda shape: pl.BlockSpec(shape, lambda b, c: (0,) * len(shape))
    state = lambda shape: pl.BlockSpec((1,) + shape, lambda b, c: (b,) + (0,) * len(shape))
    in_specs = [row_spec(COL_QKB // WB_V), row_spec(COLB_VB // WB_V), row_spec(COLB_OB // WB_V),
                pl.BlockSpec((None, 2 * H_B, lp), lambda b, c: (b, 0, c)),
                state((8, 2 * WB_QK)),
                full((CONV_B, 2 * WB_QK)), full((1, 2 * WB_QK)), full((2 * H_B, LANE)), full((1, WB_V)),
                state((H_B, DK_B, DV_B)), state((H_B, 1, DK_B)), state((H_B, 1, LANE))]
    args = [u, ub, ub, g_rows, hist, cw, cb.reshape(1, -1), bif_t, ng.reshape(1, -1), c0, n0, m0]
    n_real = len(args)
    aliases = {}
    body = functools.partial(_mlstm_kernel, L=L)
    if out is not None:
        in_specs.append(pl.BlockSpec(memory_space=pl.ANY))
        args.append(out)
        aliases = {n_real: 0}
        body = _drop_last_input(body, len(args))
    y, cf, nf, mf, xt = pl.pallas_call(
        body,
        grid=(nb, nc),
        in_specs=in_specs,
        out_specs=[pl.BlockSpec((L, WB_V), lambda b, c: (rb + b * nc + c, 0)),
                   state((H_B, DK_B, DV_B)), state((H_B, 1, DK_B)), state((H_B, 1, LANE)),
                   state((8, 2 * WB_QK))],
        out_shape=[jax.ShapeDtypeStruct((t, WB_V), BF16),
                   jax.ShapeDtypeStruct((nb, H_B, DK_B, DV_B), F32),
                   jax.ShapeDtypeStruct((nb, H_B, 1, DK_B), F32),
                   jax.ShapeDtypeStruct((nb, H_B, 1, LANE), F32),
                   jax.ShapeDtypeStruct((nb, 8, 2 * WB_QK), F32)],
        scratch_shapes=[pltpu.VMEM((L + 8, 2 * WB_QK), F32),
                        pltpu.VMEM((H_B, DK_B, DV_B), F32),
                        pltpu.VMEM((H_B, 1, DK_B), F32),
                        pltpu.VMEM((H_B, 1, LANE), F32)],
        input_output_aliases=aliases,
        compiler_params=_cparams("parallel", "arbitrary"),
        name="mlstm_sample" if out is not None else "mlstm_prompt",
    )(*args)
    return y, cf, nf.reshape(nb, H_B, DK_B), mf[:, :, 0, 0], xt[:, 8 - (CONV_B - 1):, :]


def _merge_kernel(ya_ref, yb_ref, yc_ref, gt_ref, x_ref, g1_ref, wa_ref, wb_ref, wc_ref, wo_ref, o_ref):
    tm, d = x_ref.shape
    sg = jax.nn.sigmoid(gt_ref[...].astype(F32))
    m = (sg[:, 0:d] * jnp.dot(ya_ref[...].astype(BF16), wa_ref[...], preferred_element_type=F32)
         + sg[:, d:2 * d] * jnp.dot(yb_ref[...].astype(BF16), wb_ref[...], preferred_element_type=F32)
         + sg[:, 2 * d:3 * d] * jnp.dot(yc_ref[...].astype(BF16), wc_ref[...], preferred_element_type=F32))
    o = jnp.dot(m.astype(BF16), wo_ref[...], preferred_element_type=F32)
    o = (o.reshape(tm // CHUNK, CHUNK, d) * g1_ref[...]).reshape(tm, d)
    o_ref[...] = x_ref[...] + o


def _merge(ya, yb, yc, ub, x, mod_g, wa, wb, wc, wo):
    t, d = x.shape
    tm = _row_tile(t)
    row = lambda w, col=0: pl.BlockSpec((tm, w), lambda i: (i, col))
    full = lambda a: _resident(a.shape)
    return pl.pallas_call(
        _merge_kernel,
        grid=(t // tm,),
        in_specs=[row(WA), row(WB_V), row(WC_Q), row(3 * d, COLB_GATES // (3 * d)), row(d),
                  _mod_spec(tm, 2), full(wa), full(wb), full(wc), full(wo)],
        out_specs=row(d),
        out_shape=jax.ShapeDtypeStruct((t, d), F32),
        compiler_params=_cparams("parallel"),
        name="merge",
    )(ya, yb, yc, ub, x, mod_g, wa, wb, wc, wo)


def _ffn_kernel(x_ref, sh_ref, sc_ref, g_ref, g2_ref, w1_ref, w3_ref, w2_ref, o_ref, h_scr):
    tm, d = x_ref.shape
    h_scr[...] = _ln_mod(x_ref[...], g_ref[...], sc_ref[...], sh_ref[...]).astype(BF16)
    acc = None
    for c, w in _col_chunks(w1_ref.shape[1]):
        a = jnp.dot(h_scr[...], w1_ref[:, c:c + w], preferred_element_type=F32)
        b = jnp.dot(h_scr[...], w3_ref[:, c:c + w], preferred_element_type=F32)
        tt = (a * jax.nn.sigmoid(a) * b).astype(BF16)
        p = jnp.dot(tt, w2_ref[c:c + w, :], preferred_element_type=F32)
        acc = p if acc is None else acc + p
    f = (acc.reshape(tm // CHUNK, CHUNK, d) * g2_ref[...]).reshape(tm, d)
    o_ref[...] = x_ref[...] + f


def _ffn(x, mod_g, g, w1, w3, w2):
    t, d = x.shape
    tm = _row_tile(t)
    ff = w1.shape[1]
    return pl.pallas_call(
        _ffn_kernel,
        grid=(t // tm,),
        in_specs=[pl.BlockSpec((tm, d), lambda i: (i, 0)),
                  _mod_spec(tm, 3), _mod_spec(tm, 4),
                  _resident((1, d)),
                  _mod_spec(tm, 5),
                  _resident((d, ff)), _resident((d, ff)), _resident((ff, d))],
        out_specs=pl.BlockSpec((tm, d), lambda i: (i, 0)),
        out_shape=jax.ShapeDtypeStruct((t, d), F32),
        scratch_shapes=[pltpu.VMEM((tm, d), BF16)],
        compiler_params=_cparams("parallel"),
        name="ffn_dense",
    )(x, mod_g, mod_g, g.reshape(1, d), mod_g, w1, w3, w2)


def _router_kernel(x_ref, sh_ref, sc_ref, g_ref, wr_ref, br_ref, h_ref, sel_ref):
    h = _ln_mod(x_ref[...], g_ref[...], sc_ref[...], sh_ref[...])
    h_ref[...] = h.reshape(h_ref.shape)
    lg = jnp.dot(h.astype(BF16), wr_ref[...], preferred_element_type=F32) + br_ref[...]
    lane = lax.broadcasted_iota(jnp.int32, lg.shape, 1).astype(F32)
    m1 = jnp.max(lg, axis=1, keepdims=True)
    i1 = jnp.min(jnp.where(lg == m1, lane, float(LANE)), axis=1, keepdims=True)
    lg2 = jnp.where(lane == i1, 2.0 * NEG, lg)
    m2 = jnp.max(lg2, axis=1, keepdims=True)
    i2 = jnp.min(jnp.where(lg2 == m2, lane, float(LANE)), axis=1, keepdims=True)
    e2 = jnp.exp(m2 - m1)
    w1 = 1.0 / (1.0 + e2)
    sel_ref[...] = (jnp.where(lane == 0.0, i1, 0.0) + jnp.where(lane == 1.0, i2, 0.0)
                    + jnp.where(lane == 2.0, w1, 0.0) + jnp.where(lane == 3.0, e2 * w1, 0.0))


def _router(x, mod_g, g, w_r, b_r):
    t, d = x.shape
    tm = _row_tile(t)
    wr = jnp.zeros((d, LANE), BF16).at[:, :N_EXPERTS].set(w_r.astype(BF16))
    br = jnp.full((1, LANE), NEG, F32).at[0, :N_EXPERTS].set(b_r.astype(F32))
    return pl.pallas_call(
        _router_kernel,
        grid=(t // tm,),
        in_specs=[pl.BlockSpec((tm, d), lambda i: (i, 0)),
                  _mod_spec(tm, 3), _mod_spec(tm, 4),
                  pl.BlockSpec((1, d), lambda i: (0, 0)),
                  pl.BlockSpec((d, LANE), lambda i: (0, 0)),
                  pl.BlockSpec((1, LANE), lambda i: (0, 0))],
        out_specs=[pl.BlockSpec((tm // SUBLANES, SUBLANES, d), lambda i: (i, 0, 0)),
                   pl.BlockSpec((tm, LANE), lambda i: (i, 0))],
        out_shape=[jax.ShapeDtypeStruct((t // SUBLANES, SUBLANES, d), F32), jax.ShapeDtypeStruct((t, LANE), F32)],
        compiler_params=_cparams("parallel"),
        name="router",
    )(x, mod_g, mod_g, g.reshape(1, d), wr, br)


def _route(sel, tmr):
    t = sel.shape[0]
    i32 = jnp.int32
    experts = jnp.arange(N_EXPERTS, dtype=i32)[None, :]
    e = sel[:, 0:2].astype(i32).reshape(-1)
    oh = (e[:, None] == experts).astype(i32)
    csum = jnp.cumsum(oh, axis=0)
    cnt = csum[-1]
    rank = jnp.sum(csum * oh, axis=1) - 1
    ntile_e = (cnt + tmr - 1) // tmr
    tile_end = jnp.cumsum(ntile_e)
    tile_start = tile_end - ntile_e
    pos = jnp.sum(oh * (tile_start * tmr)[None, :], axis=1) + rank
    n_tiles = -(-2 * t // tmr) + N_EXPERTS
    n_used = tile_end[-1]
    tiles = jnp.arange(n_tiles, dtype=i32)
    tc = jnp.minimum(tiles, n_used - 1)
    te = jnp.sum((tc[:, None] >= tile_end[None, :]).astype(i32), axis=1)
    ohe = (te[:, None] == experts).astype(i32)
    nvalid = jnp.sum(ohe * cnt[None, :], axis=1) - (tc - jnp.sum(ohe * tile_start[None, :], axis=1)) * tmr
    nvalid = jnp.where(tiles < n_used, jnp.clip(nvalid, 0, tmr), 0)
    r = jnp.arange(EXPERT_SUB, dtype=i32)[None, :]
    n_pad = ((-cnt) % EXPERT_SUB)[:, None]
    spare = n_tiles * tmr + jnp.arange(N_EXPERTS * EXPERT_SUB, dtype=i32).reshape(N_EXPERTS, EXPERT_SUB)
    pad_rows = jnp.where(r < n_pad, (tile_start * tmr + cnt)[:, None] + r, spare).reshape(-1)
    return pos.astype(i32), te.astype(i32), nvalid.astype(i32), n_used.reshape(1).astype(i32), pad_rows.astype(i32)


EXPERT_TILE = 1024
EXPERT_SUB = 256
ROUTE_TILE = 256


def _row_copies(n_groups, make_copy):
    def issue(g, carry):
        for sub in range(SUBLANES):
            for k in range(2):
                make_copy(g, sub, k).start(priority=k)
        return carry

    def drain(g, carry):
        for sub in range(SUBLANES):
            for k in range(2):
                make_copy(g, sub, k).wait()
        return carry

    lax.fori_loop(0, n_groups, issue, 0)
    lax.fori_loop(0, n_groups, drain, 0)


def _slot_row(hi_ref, lo_ref, g, sub, k):
    idx = 2 * SUBLANES * g + 2 * sub + k
    return hi_ref[0, 0, idx], lo_ref[0, 0, idx]


def _split_pos(pos, t, tm):
    hi = lax.shift_right_logical(pos, 3).reshape(t // tm, 1, 2 * tm)
    lo = jnp.bitwise_and(pos, SUBLANES - 1).reshape(t // tm, 1, 2 * tm)
    return hi, lo


def _dispatch_kernel(hi_ref, lo_ref, pad_hi_ref, pad_lo_ref, h_ref, xs_ref, zrow, sem):
    @pl.when(pl.program_id(0) == 0)
    def _():
        zrow[...] = jnp.zeros_like(zrow)

        def zero_copy(g, sub, k):
            hi, lo = _slot_row(pad_hi_ref, pad_lo_ref, g, sub, k)
            return pltpu.make_async_copy(zrow.at[pl.ds(sub, 1), :], xs_ref.at[hi, pl.ds(lo, 1), :], sem)

        _row_copies(pad_hi_ref.shape[2] // (2 * SUBLANES), zero_copy)

    def make_copy(g, sub, k):
        hi, lo = _slot_row(hi_ref, lo_ref, g, sub, k)
        return pltpu.make_async_copy(h_ref.at[g, pl.ds(sub, 1), :], xs_ref.at[hi, pl.ds(lo, 1), :], sem)

    _row_copies(h_ref.shape[0], make_copy)


def _dispatch(h, pos, pad_rows, n_rows):
    t, d = h.shape[0] * SUBLANES, h.shape[2]
    tm = min(ROUTE_TILE, _row_tile(t))
    n_pad = pad_rows.shape[0]
    smem = pl.BlockSpec((1, 1, 2 * tm), lambda i: (i, 0, 0), memory_space=pltpu.SMEM)
    smem_pad = pl.BlockSpec((1, 1, n_pad), lambda i: (0, 0, 0), memory_space=pltpu.SMEM)
    pad_hi = lax.shift_right_logical(pad_rows, 3).reshape(1, 1, n_pad)
    pad_lo = jnp.bitwise_and(pad_rows, SUBLANES - 1).reshape(1, 1, n_pad)
    return pl.pallas_call(
        _dispatch_kernel,
        grid=(t // tm,),
        in_specs=[smem, smem, smem_pad, smem_pad,
                  pl.BlockSpec((tm // SUBLANES, SUBLANES, d), lambda i: (i, 0, 0))],
        out_specs=pl.BlockSpec(memory_space=pl.ANY),
        out_shape=jax.ShapeDtypeStruct(((n_rows + n_pad) // SUBLANES, SUBLANES, d), F32),
        scratch_shapes=[pltpu.VMEM((SUBLANES, d), F32), pltpu.SemaphoreType.DMA(())],
        compiler_params=_cparams("arbitrary"),
        name="moe_dispatch",
    )(*_split_pos(pos, t, tm), pad_hi, pad_lo, h)


def _experts_kernel(te_ref, nv_ref, nu_ref, x_ref, w1_ref, w3_ref, w2_ref, y_ref, xb, acc):
    del te_ref, nu_ref
    i, j = pl.program_id(0), pl.program_id(1)
    nv = nv_ref[i]

    @pl.when(nv > 0)
    def _():
        @pl.when(j == 0)
        def _():
            acc[...] = jnp.zeros_like(acc)

        for m in range(EXPERT_SUB, EXPERT_TILE + 1, EXPERT_SUB):
            @pl.when((nv > m - EXPERT_SUB) & (nv <= m))
            def _(m=m):
                @pl.when(j == 0)
                def _():
                    xb[0:m, :] = x_ref[0:m, :].astype(BF16)

                xs = xb[0:m, :]
                a = jnp.dot(xs, w1_ref[0].astype(BF16), preferred_element_type=F32)
                b = jnp.dot(xs, w3_ref[0].astype(BF16), preferred_element_type=F32)
                tt = (a * jax.nn.sigmoid(a) * b).astype(BF16)
                acc[0:m, :] += jnp.dot(tt, w2_ref[0].astype(BF16), preferred_element_type=F32)

        @pl.when(j == pl.num_programs(1) - 1)
        def _():
            y_ref[...] = acc[...]


def _experts(xs, te, nvalid, n_used, w1, w3, w2):
    d = xs.shape[1]
    ne, _, ff = w1.shape
    tf = 512
    nj = ff // tf
    n_tiles = te.shape[0]
    n_rows = n_tiles * EXPERT_TILE
    tile_rows = (EXPERT_TILE, d)

    def row_map(i, j, te_ref, nv_ref, nu_ref):
        return (jnp.minimum(i, nu_ref[0] - 1), 0)

    def col(i, j, nu_ref):
        return jnp.where(i < nu_ref[0], j, nj - 1)

    grid_spec = pltpu.PrefetchScalarGridSpec(
        num_scalar_prefetch=3,
        grid=(n_tiles, nj),
        in_specs=[pl.BlockSpec(tile_rows, row_map),
                  pl.BlockSpec((1, d, tf), lambda i, j, te_ref, nv_ref, nu_ref: (te_ref[i], 0, col(i, j, nu_ref))),
                  pl.BlockSpec((1, d, tf), lambda i, j, te_ref, nv_ref, nu_ref: (te_ref[i], 0, col(i, j, nu_ref))),
                  pl.BlockSpec((1, tf, d), lambda i, j, te_ref, nv_ref, nu_ref: (te_ref[i], col(i, j, nu_ref), 0))],
        out_specs=pl.BlockSpec(tile_rows, row_map),
        scratch_shapes=[pltpu.VMEM((EXPERT_TILE, d), BF16), pltpu.VMEM((EXPERT_TILE, d), F32)])
    return pl.pallas_call(
        _experts_kernel,
        grid_spec=grid_spec,
        out_shape=jax.ShapeDtypeStruct((n_rows, d), F32),
        compiler_params=_cparams("arbitrary", "arbitrary"),
        name="moe_experts",
    )(te, nvalid, n_used, xs, w1, w3, w2)


def _combine_kernel(hi_ref, lo_ref, sel_ref, x_ref, g2_ref, y_ref, *rest, n_prompt_tiles):
    tm, d = x_ref.shape
    rows, sem = rest[-2:]

    def make_copy(g, sub, k):
        hi, lo = _slot_row(hi_ref, lo_ref, g, sub, k)
        return pltpu.make_async_copy(y_ref.at[hi, pl.ds(lo, 1), :], rows.at[k, g, pl.ds(sub, 1), :], sem)

    _row_copies(tm // SUBLANES, make_copy)
    sel = sel_ref[...]
    f = sel[:, 2:3] * rows[0].reshape(tm, d) + sel[:, 3:4] * rows[1].reshape(tm, d)
    f = (f.reshape(tm // CHUNK, CHUNK, d) * g2_ref[...]).reshape(tm, d)
    xo = x_ref[...] + f
    if n_prompt_tiles is None:
        rest[0][...] = xo
        return
    gf_ref, op_ref, os_ref = rest[:3]
    yo = xo * lax.rsqrt(jnp.mean(xo * xo, axis=-1, keepdims=True) + EPS) * gf_ref[...]
    i = pl.program_id(0)

    @pl.when(i < n_prompt_tiles)
    def _():
        op_ref[...] = yo

    @pl.when(i >= n_prompt_tiles)
    def _():
        os_ref[...] = yo


def _combine(y, pos, sel, x, mod_g, final=None):
    t, d = x.shape
    tm = min(ROUTE_TILE, _row_tile(t if final is None else np.gcd(final[1], t - final[1])))
    smem = pl.BlockSpec((1, 1, 2 * tm), lambda i: (i, 0, 0), memory_space=pltpu.SMEM)
    in_specs = [smem, smem,
                pl.BlockSpec((tm, LANE), lambda i: (i, 0)),
                pl.BlockSpec((tm, d), lambda i: (i, 0)),
                _mod_spec(tm, 5),
                pl.BlockSpec(memory_space=pl.ANY)]
    args = [*_split_pos(pos, t, tm), sel, x, mod_g, y]
    if final is None:
        n_p = None
        out_specs = pl.BlockSpec((tm, d), lambda i: (i, 0))
        out_shape = jax.ShapeDtypeStruct((t, d), F32)
    else:
        gain, tp = final
        n_p = tp // tm
        in_specs.append(pl.BlockSpec((1, d), lambda i: (0, 0)))
        args.append(gain.reshape(1, d))
        out_specs = [pl.BlockSpec((tm, d), lambda i: (jnp.minimum(i, n_p - 1), 0)),
                     pl.BlockSpec((tm, d), lambda i: (jnp.maximum(i - n_p, 0), 0))]
        out_shape = [jax.ShapeDtypeStruct((tp, d), F32), jax.ShapeDtypeStruct((t - tp, d), F32)]
    return pl.pallas_call(
        functools.partial(_combine_kernel, n_prompt_tiles=n_p),
        grid=(t // tm,),
        in_specs=in_specs,
        out_specs=out_specs,
        out_shape=out_shape,
        scratch_shapes=[pltpu.VMEM((2, tm // SUBLANES, SUBLANES, d), F32), pltpu.SemaphoreType.DMA(())],
        compiler_params=_cparams("arbitrary"),
        name="moe_combine",
    )(*args)


def _moe(x, mod_g, g, w_r, b_r, w1, w3, w2, final=None):
    d = x.shape[1]
    h2, sel = _router(x, mod_g, g, w_r, b_r)
    pos, te, nvalid, n_used, pad_rows = _route(sel, EXPERT_TILE)
    n_rows = te.shape[0] * EXPERT_TILE
    xs = _dispatch(h2, pos, pad_rows, n_rows)
    y = _experts(xs.reshape(-1, d), te, nvalid, n_used, w1, w3, w2)
    return _combine(y.reshape(n_rows // SUBLANES, SUBLANES, d), pos, sel, x, mod_g, final)


def _final_kernel(x_ref, g_ref, op_ref, os_ref, *, n_prompt_tiles):
    x = x_ref[...]
    y = x * lax.rsqrt(jnp.mean(x * x, axis=-1, keepdims=True) + EPS) * g_ref[...]
    i = pl.program_id(0)

    @pl.when(i < n_prompt_tiles)
    def _():
        op_ref[...] = y

    @pl.when(i >= n_prompt_tiles)
    def _():
        os_ref[...] = y


def _final_norm(x, g, tp):
    t, d = x.shape
    tm = _row_tile(np.gcd(tp, t - tp))
    n_p = tp // tm
    return pl.pallas_call(
        functools.partial(_final_kernel, n_prompt_tiles=n_p),
        grid=(t // tm,),
        in_specs=[pl.BlockSpec((tm, d), lambda i: (i, 0)), pl.BlockSpec((1, d), lambda i: (0, 0))],
        out_specs=[pl.BlockSpec((tm, d), lambda i: (jnp.minimum(i, n_p - 1), 0)),
                   pl.BlockSpec((tm, d), lambda i: (jnp.maximum(i - n_p, 0), 0))],
        out_shape=[jax.ShapeDtypeStruct((tp, d), F32), jax.ShapeDtypeStruct((t - tp, d), F32)],
        compiler_params=_cparams("arbitrary"),
        name="final_norm",
    )(x, g.reshape(1, d))


def _split_w_in(w):
    o = np.cumsum([0, WA, WA, WA, 2 * WB_QK, WB_V, 2 * H_B, WB_V, WC_Q, WC_KV, WC_KV, 3 * D_MODEL])
    seg = lambda k: w[:, o[k]:o[k + 1]]
    qa, ka, va, qkb, vb, ifb, ob, qc, kc, vc, gates = (seg(k) for k in range(11))
    main = jnp.concatenate([qa, ka, va, qc, qkb, kc, vc, gates, vb, ob], axis=1).astype(BF16)
    return main, ifb.T.astype(BF16)


def kernel(x_prompt, x_sample, c_prompt, c_sample, cache_a_k, cache_a_v, state_b_C, state_b_n, state_b_m, state_b_conv, cache_c_k, cache_c_v, norm1_g, norm2_g, w_ada, b_ada, w_in, b_if_b, conv_w_b, conv_b_b, norm_b_g, rel_a, sink_c, w_br_a, w_br_b, w_br_c, w_o, w_ff1, w_ff3, w_ff2, w_router, b_router, w_e1, w_e3, w_e2, norm_f_g):
    nbp, sp, d = x_prompt.shape
    nbs, ss, _ = x_sample.shape
    tp, ts = nbp * sp, nbs * ss
    x = jnp.concatenate([x_prompt.reshape(tp, d), x_sample.reshape(ts, d)], axis=0)
    cond = jnp.concatenate([c_prompt, c_sample], axis=0)
    group_batch = np.concatenate([np.repeat(np.arange(nbp), sp // CHUNK), nbp + np.repeat(np.arange(nbs), ss // CHUNK)])

    lp_s = max(ss, LANE)
    l_b = 256 if sp % 256 == 0 else CHUNK
    keep_a, keep_c = min(PRE_A, sp), min(PRE_C, sp)
    zeros_state = (jnp.zeros((nbp, H_B, DK_B, DV_B), F32), jnp.zeros((nbp, H_B, DK_B), F32), jnp.zeros((nbp, H_B), F32))
    new = {k: [] for k in ("pak", "pav", "pbc", "pbn", "pbm", "pbx", "pck", "pcv",
                           "sak", "sav", "sbc", "sbn", "sbm", "sbx", "sck", "scv")}

    for l in range(DEPTH):
        mod = _ada(cond, w_ada, b_ada, l)
        mod_g = mod[group_batch].reshape(-1, 1, 6 * d)
        w_main, w_if_t = _split_w_in(w_in[l])
        u, ub, g_t = _inproj(x, mod_g, norm1_g[l], w_main, w_if_t)

        bias_t = _rel_bias_table(rel_a[l])
        ya, pak, pav = _attn_a(u, bias_t, nb=nbp, s=sp, row0=0)
        ya, sak, sav = _attn_a(u, bias_t, nb=nbs, s=ss, row0=tp,
                               prefix=(cache_a_k[l].reshape(nbs, PRE_A, WA), cache_a_v[l].reshape(nbs, PRE_A, WA)),
                               out=ya)
        yc, pck, pcv = _attn_c(u, sink_c[l], nb=nbp, s=sp, row0=0, pos0=0)
        yc, sck, scv = _attn_c(u, sink_c[l], nb=nbs, s=ss, row0=tp, pos0=PAST_LEN,
                               prefix=(cache_c_k[l].reshape(nbs, PRE_C, WC_KV),
                                       cache_c_v[l].reshape(nbs, PRE_C, WC_KV)), out=yc)
        g_p = g_t[:, :tp].reshape(2 * H_B, nbp, sp).transpose(1, 0, 2)
        g_s = g_t[:, tp:].reshape(2 * H_B, nbs, ss).transpose(1, 0, 2)
        g_s = jnp.pad(g_s, ((0, 0), (0, 0), (0, lp_s - ss)))
        hist_p = jnp.zeros((nbp, 8, 2 * WB_QK), F32)
        hist_s = jnp.pad(state_b_conv[l], ((0, 0), (8 - (CONV_B - 1), 0), (0, 0)))
        bargs = (conv_w_b[l], conv_b_b[l], b_if_b[l], norm_b_g[l])
        yb, pbc, pbn, pbm, pbx = _mlstm(u, ub, g_p, hist_p, *bargs, *zeros_state, nb=nbp, s=sp, row0=0, L=l_b)
        yb, sbc, sbn, sbm, sbx = _mlstm(u, ub, g_s, hist_s, *bargs, state_b_C[l], state_b_n[l], state_b_m[l],
                                        nb=nbs, s=ss, row0=tp, L=ss, out=yb)

        x = _merge(ya, yb, yc, ub, x, mod_g, w_br_a[l].astype(BF16), w_br_b[l].astype(BF16),
                   w_br_c[l].astype(BF16), w_o[l].astype(BF16))
        i = l // 2
        if l % 2 == 0:
            x = _ffn(x, mod_g, norm2_g[l], w_ff1[i].astype(BF16), w_ff3[i].astype(BF16), w_ff2[i].astype(BF16))
        else:
            x = _moe(x, mod_g, norm2_g[l], w_router[i], b_router[i], w_e1[i], w_e3[i], w_e2[i],
                     final=(norm_f_g, tp) if l == DEPTH - 1 else None)

        for name, val in (("pak", pak), ("pav", pav), ("pbc", pbc), ("pbn", pbn), ("pbm", pbm), ("pbx", pbx),
                          ("pck", pck), ("pcv", pcv), ("sak", sak), ("sav", sav), ("sbc", sbc), ("sbn", sbn),
                          ("sbm", sbm), ("sbx", sbx), ("sck", sck), ("scv", scv)):
            new[name].append(val)

    y_p, y_s = x if DEPTH % 2 == 0 else _final_norm(x, norm_f_g, tp)
    heads = {"pak": (H_A, DH_A), "pav": (H_A, DH_A), "sak": (H_A, DH_A), "sav": (H_A, DH_A),
             "pck": (HKV_C, DH_C), "pcv": (HKV_C, DH_C), "sck": (HKV_C, DH_C), "scv": (HKV_C, DH_C)}
    st = {k: jnp.stack(v) for k, v in new.items()}
    st = {k: v.reshape(v.shape[:-1] + heads[k]) if k in heads else v for k, v in st.items()}
    return (y_p.reshape(nbp, sp, d), y_s.reshape(nbs, ss, d),
            st["pak"], st["pav"], st["pbc"], st["pbn"], st["pbm"], st["pbx"], st["pck"], st["pcv"],
            st["sak"], st["sav"], st["sbc"], st["sbn"], st["sbm"], st["sbx"], st["sck"], st["scv"])
```

```python
import functools

import numpy as np
import jax
import jax.numpy as jnp
from jax import lax
from jax.experimental import pallas as pl
from jax.experimental.pallas import tpu as pltpu

F32 = jnp.float32
BF16 = jnp.bfloat16

D_MODEL = 1024
DEPTH = 2
PAST_LEN = 2048
CHUNK = 64
H_A, DH_A, BACK_A, REL_CLIP = 8, 64, 8, 128
H_B, DK_B, DV_B, CONV_B = 4, 128, 256, 4
HQ_C, HKV_C, DH_C, BACK_C = 8, 2, 64, 2
ROT_DIM = DH_C // 4
ROPE_THETA = 500000.0
D_FF = 2816
N_EXPERTS = 8
D_FF_E = 3584
EPS = 1e-6
NEG = -1e30

WA = H_A * DH_A
WB_QK = H_B * DK_B
WB_V = H_B * DV_B
WC_Q = HQ_C * DH_C
WC_KV = HKV_C * DH_C
PRE_A = BACK_A * CHUNK
PRE_C = BACK_C * CHUNK

COL_QA, COL_KA, COL_VA, COL_QC = 0, 512, 1024, 1536
COL_QKB = 2048
COL_KC = 3072
COL_VC = 3200
N_F32 = 3328
COLB_GATES = 0
COLB_VB = 3072
COLB_OB = 4096
N_B16 = 5120
N_MAIN = N_F32 + N_B16

LANE = 128
SUBLANES = 8
VMEM_LIMIT = 48 * 1024 * 1024
INPROJ_VMEM_SLACK = 8 * 1024 * 1024


def _cparams(*sem, vmem=VMEM_LIMIT):
    return pltpu.CompilerParams(dimension_semantics=sem, vmem_limit_bytes=vmem)


def _row_tile(t):
    for tm in (512, 256, 128, 64):
        if t % tm == 0:
            return tm
    raise ValueError(f"token count {t} is not a multiple of 64")


def _ln_mod(x, g, sc, sh):
    tm, d = x.shape
    y = x * lax.rsqrt(jnp.mean(x * x, axis=-1, keepdims=True) + EPS) * g
    y = y.reshape(tm // CHUNK, CHUNK, d) * (1.0 + sc) + sh
    return y.reshape(tm, d)


def _mod_spec(tm, kind):
    return pl.BlockSpec((tm // CHUNK, 1, D_MODEL), lambda i, *_, k=kind: (i, 0, k))


def _ada_kernel(c_ref, w_ref, b_ref, o_ref):
    c = c_ref[...]
    a = (c * jax.nn.sigmoid(c)).astype(BF16)
    o_ref[...] = jnp.dot(a, w_ref[...].astype(BF16), preferred_element_type=F32) + b_ref[...]


def _ada(c, w, b, layer):
    nb, d = c.shape
    n = w.shape[2]
    tn = 1536
    return pl.pallas_call(
        _ada_kernel,
        grid=(n // tn,),
        in_specs=[pl.BlockSpec((nb, d), lambda j: (0, 0)),
                  pl.BlockSpec((None, d, tn), lambda j: (layer, 0, j)),
                  pl.BlockSpec((None, 1, tn), lambda j: (layer, 0, j))],
        out_specs=pl.BlockSpec((nb, tn), lambda j: (0, j)),
        out_shape=jax.ShapeDtypeStruct((nb, n), F32),
        compiler_params=_cparams("arbitrary"),
        name="adaln",
    )(c, w, b.reshape(b.shape[0], 1, n))


def _col_chunks(n, width=768):
    return [(c, min(width, n - c)) for c in range(0, n, width)]


def _resident(shape):
    return pl.BlockSpec(shape, lambda *_: (0,) * len(shape), pipeline_mode=pl.Buffered(1))


W_LO_COLS = 3 * WA + 2 * WB_QK + WB_V
W_HI_COLS = WB_V + WC_Q + 2 * WC_KV + 3 * D_MODEL
_INPROJ_MOVES = (
    ("lo", 0, False, COL_QA, 3 * WA),
    ("hi", WB_V, False, COL_QC, WC_Q),
    ("lo", 3 * WA, False, COL_QKB, 2 * WB_QK),
    ("hi", WB_V + WC_Q, False, COL_KC, 2 * WC_KV),
    ("hi", WB_V + WC_Q + 2 * WC_KV, True, COLB_GATES, 3 * D_MODEL),
    ("lo", 3 * WA + 2 * WB_QK, True, COLB_VB, WB_V),
    ("hi", 0, True, COLB_OB, WB_V),
)


def _inproj_kernel(x_ref, sh_ref, sc_ref, g_ref, wlo_ref, whi_ref, wif_ref, u_ref, ub_ref, gt_ref, h_scr):
    h_scr[...] = _ln_mod(x_ref[...], g_ref[...], sc_ref[...], sh_ref[...]).astype(BF16)
    gt_ref[...] = lax.dot_general(wif_ref[...], h_scr[...], (((1,), (1,)), ((), ())), preferred_element_type=F32)
    for half, src, to_b16, dst, width in _INPROJ_MOVES:
        w_ref = wlo_ref if half == "lo" else whi_ref
        for c, w in _col_chunks(width):
            r = jnp.dot(h_scr[...], w_ref[:, src + c:src + c + w], preferred_element_type=F32)
            if to_b16:
                ub_ref[:, dst + c:dst + c + w] = r.astype(BF16)
            else:
                u_ref[:, dst + c:dst + c + w] = r


def _inproj(x, mod_g, g, w_lo, w_hi, w_if_t):
    t, d = x.shape
    tm = _row_tile(t)
    vmem = (d * N_MAIN * 2 + 2 * tm * (d * 4 + N_F32 * 4 + N_B16 * 2) + tm * d * 2) + INPROJ_VMEM_SLACK
    return pl.pallas_call(
        _inproj_kernel,
        grid=(t // tm,),
        in_specs=[pl.BlockSpec((tm, d), lambda i: (i, 0)),
                  _mod_spec(tm, 0), _mod_spec(tm, 1),
                  _resident((1, d)), _resident((d, W_LO_COLS)), _resident((d, W_HI_COLS)),
                  _resident((2 * H_B, d))],
        out_specs=[pl.BlockSpec((tm, N_F32), lambda i: (i, 0)),
                   pl.BlockSpec((tm, N_B16), lambda i: (i, 0)),
                   pl.BlockSpec((2 * H_B, tm), lambda i: (0, i))],
        out_shape=[jax.ShapeDtypeStruct((t, N_F32), F32),
                   jax.ShapeDtypeStruct((t, N_B16), BF16),
                   jax.ShapeDtypeStruct((2 * H_B, t), F32)],
        scratch_shapes=[pltpu.VMEM((tm, d), BF16)],
        compiler_params=_cparams("parallel", vmem=vmem),
        name="inproj",
    )(x, mod_g, mod_g, g.reshape(1, d), w_lo, w_hi, w_if_t)


def _band_table(bq, pre, back):
    qi = np.arange(bq)[:, None] // CHUNK
    kj = np.arange(pre + bq)[None, :] // CHUNK
    return (kj >= qi) & (kj <= qi + back)


QBLK = 128
STEP_QBLKS = 2
KV_CHUNK = 512


def _step_rows(s):
    return STEP_QBLKS * QBLK if s % (STEP_QBLKS * QBLK) == 0 else min(QBLK, s)


def _pad_rows(x, rows):
    return x if x.shape[0] == rows else jnp.concatenate(
        [x, jnp.zeros((rows - x.shape[0], x.shape[1]), x.dtype)], axis=0)


def _stage_keys_values(k_rows, v_ref, prefix_k, prefix_v, kpad, vt, pre, s):
    w = kpad.shape[1]
    s_pad = kpad.shape[0] - pre
    if prefix_k is None:
        kpad[0:pre, :] = jnp.zeros((pre, w), BF16)
        vt[:, 0:pre] = jnp.zeros((w, pre), BF16)
    else:
        kpad[0:pre, :] = prefix_k.astype(BF16)
        vt[:, 0:pre] = prefix_v.T.astype(BF16)
    for c in range(0, s_pad, KV_CHUNK):
        rows = min(KV_CHUNK, s_pad - c)
        real = max(0, min(rows, s - c))
        kpad[pre + c:pre + c + rows, :] = _pad_rows(k_rows(c, real), rows).astype(BF16)
        vt[:, pre + c:pre + c + rows] = _pad_rows(v_ref[c:c + real, :], rows).T.astype(BF16)


def _pair_store(o_ref, ots, lane0, row0, rows):
    pair = jnp.concatenate(ots, axis=0).T
    o_ref[row0:row0 + rows, lane0:lane0 + pair.shape[1]] = pair[0:rows, :].astype(o_ref.dtype)


def _query_blocks(q_ref):
    rows = q_ref.shape[0]
    return max(1, rows // QBLK), min(QBLK, rows)


def _attn_a_kernel(*refs, s, has_prefix):
    if has_prefix:
        q_ref, k_ref, v_ref, pk_ref, pv_ref, bias_ref, o_ref, kt_ref, vt_ref, kpad, vt = refs
    else:
        q_ref, k_ref, v_ref, bias_ref, o_ref, kt_ref, vt_ref, kpad, vt = refs
    n = pl.program_id(1)
    kw = PRE_A + QBLK
    keep = kt_ref.shape[1]
    nsub, sq = _query_blocks(q_ref)

    @pl.when(n == 0)
    def _():
        _stage_keys_values(lambda c, rows: k_ref[c:c + rows, :], v_ref,
                           pk_ref[0] if has_prefix else None, pv_ref[0] if has_prefix else None,
                           kpad, vt, PRE_A, s)
        kt_ref[0] = k_ref[s - keep:s, :]
        vt_ref[0] = v_ref[s - keep:s, :]

    base = pl.multiple_of(n * (nsub * QBLK), nsub * QBLK)
    heads = [slice(h * DH_A, (h + 1) * DH_A) for h in range(H_A)]
    nt = (((1,), (1,)), ((), ()))

    def attend(mask_keys):
        for sub in range(nsub):
            r0 = base + sub * QBLK
            q = _pad_rows((q_ref[sub * sq:(sub + 1) * sq, :] * (DH_A ** -0.5)).astype(BF16), QBLK)
            scores = [lax.dot_general(kpad[pl.ds(r0, kw), hs], q[:, hs], nt, preferred_element_type=F32)
                      for hs in heads]
            if mask_keys:
                key = lax.broadcasted_iota(jnp.int32, (kw, QBLK), 0)
                invalid = jnp.where(key >= PRE_A - r0, 0.0, NEG).astype(F32)
            probs, sums = [], []
            for h in range(H_A):
                sc = scores[h] + bias_ref[h]
                if mask_keys:
                    sc = sc + invalid
                e = jnp.exp(sc - jnp.max(sc, axis=0, keepdims=True))
                sums.append(jnp.sum(e, axis=0, keepdims=True))
                probs.append(e.astype(BF16))
            for h in range(0, H_A, 2):
                ots = [jnp.dot(vt[heads[i], pl.ds(r0, kw)], probs[i], preferred_element_type=F32) / sums[i]
                       for i in (h, h + 1)]
                _pair_store(o_ref, ots, h * DH_A, sub * sq, sq)

    if has_prefix:
        attend(False)
    else:
        pl.when(base < PRE_A)(lambda: attend(True))
        pl.when(base >= PRE_A)(lambda: attend(False))


def _rel_bias_table(rel):
    kw = PRE_A + QBLK
    j0 = PRE_A - REL_CLIP
    rows = kw - j0
    n = QBLK + rows
    m = np.arange(n)
    relf = rel.astype(F32)
    diag = relf[np.clip(m + 1 - QBLK, -REL_CLIP, REL_CLIP) + REL_CLIP].T
    skew = jnp.tile(diag, (1, rows + 1))[:, :rows * (n - 1)].reshape(H_A, rows, n - 1)
    far = jnp.broadcast_to(relf[2 * REL_CLIP][:, None, None], (H_A, j0, QBLK))
    table = jnp.concatenate([far, skew[:, :, rows - 1:rows - 1 + QBLK]], axis=1)
    return jnp.where(_band_table(QBLK, PRE_A, BACK_A).T[None], table, NEG)


def _attn_a(u, bias_t, *, nb, s, row0, prefix=None, out=None):
    t = u.shape[0]
    bq = _step_rows(s)
    nq = s // bq
    s_pad = max(s, QBLK)
    kw = PRE_A + QBLK
    keep = min(PRE_A, s)
    rb, sb = row0 // bq, row0 // s
    in_specs = [pl.BlockSpec((bq, WA), lambda b, n: (rb + b * nq + n, COL_QA // WA)),
                pl.BlockSpec((s, WA), lambda b, n: (sb + b, COL_KA // WA)),
                pl.BlockSpec((s, WA), lambda b, n: (sb + b, COL_VA // WA))]
    args = [u, u, u]
    if prefix is not None:
        in_specs += [pl.BlockSpec((1, PRE_A, WA), lambda b, n: (b, 0, 0))] * 2
        args += [prefix[0], prefix[1]]
    in_specs.append(pl.BlockSpec((H_A, kw, QBLK), lambda b, n: (0, 0, 0)))
    args.append(bias_t)
    aliases = {}
    if out is not None:
        in_specs.append(pl.BlockSpec(memory_space=pl.ANY))
        args.append(out)
        aliases = {len(args) - 1: 0}
    body = functools.partial(_attn_a_kernel, s=s, has_prefix=prefix is not None)
    if out is not None:
        body = _drop_last_input(body, len(args))
    return pl.pallas_call(
        body,
        grid=(nb, nq),
        in_specs=in_specs,
        out_specs=[pl.BlockSpec((bq, WA), lambda b, n: (rb + b * nq + n, 0)),
                   pl.BlockSpec((1, keep, WA), lambda b, n: (b, 0, 0)),
                   pl.BlockSpec((1, keep, WA), lambda b, n: (b, 0, 0))],
        out_shape=[jax.ShapeDtypeStruct((t, WA), BF16),
                   jax.ShapeDtypeStruct((nb, keep, WA), F32),
                   jax.ShapeDtypeStruct((nb, keep, WA), F32)],
        scratch_shapes=[pltpu.VMEM((PRE_A + s_pad, WA), BF16), pltpu.VMEM((WA, PRE_A + s_pad), BF16)],
        input_output_aliases=aliases,
        compiler_params=_cparams("parallel", "arbitrary"),
        name="mixer_a_sample" if prefix is not None else "mixer_a_prompt",
    )(*args)


def _drop_last_input(body, n_in):
    def wrapped(*refs):
        return body(*refs[:n_in - 1], *refs[n_in:])
    return wrapped


def _rope(x, c, s1, s2):
    w = x.shape[1]
    return x * c + pltpu.roll(x, 8, 1) * s1 + pltpu.roll(x, w - 8, 1) * s2


def _attn_c_kernel(*refs, s, has_prefix):
    if has_prefix:
        (q_ref, k_ref, v_ref, pk_ref, pv_ref, rc_ref, rs1_ref, rs2_ref, band_ref, sink_ref,
         o_ref, kt_ref, vt_ref, kpad, vt) = refs
    else:
        (q_ref, k_ref, v_ref, rc_ref, rs1_ref, rs2_ref, band_ref, sink_ref,
         o_ref, kt_ref, vt_ref, kpad, vt) = refs
    n = pl.program_id(1)
    kw = PRE_C + QBLK
    keep = kt_ref.shape[1]
    nsub, sq = _query_blocks(q_ref)

    def rope_at(x, start, rep=1):
        tables = [jnp.concatenate([t_ref[pl.ds(start, x.shape[0]), :]] * rep, axis=1)
                  for t_ref in (rc_ref, rs1_ref, rs2_ref)]
        return _rope(x, *tables)

    @pl.when(n == 0)
    def _():
        _stage_keys_values(lambda c, rows: rope_at(k_ref[c:c + rows, :], c), v_ref,
                           pk_ref[0] if has_prefix else None, pv_ref[0] if has_prefix else None,
                           kpad, vt, PRE_C, s)
        kt_ref[0] = rope_at(k_ref[s - keep:s, :], s - keep)
        vt_ref[0] = v_ref[s - keep:s, :]

    base = pl.multiple_of(n * (nsub * QBLK), nsub * QBLK)
    per = HQ_C // HKV_C
    groups = [slice(g * DH_C, (g + 1) * DH_C) for g in range(HKV_C)]
    nt = (((1,), (1,)), ((), ()))

    def attend(mask_keys):
        for sub in range(nsub):
            r0 = base + sub * QBLK
            q = rope_at(q_ref[sub * sq:(sub + 1) * sq, :], pl.multiple_of(n * (nsub * sq) + sub * sq, sq),
                        rep=WC_Q // WC_KV)
            q = _pad_rows((q * (DH_C ** -0.5)).astype(BF16), QBLK)
            scores = []
            for g, gs in enumerate(groups):
                qs = jnp.concatenate([q[:, (g * per + i) * DH_C:(g * per + i + 1) * DH_C] for i in range(per)],
                                     axis=0)
                scores.append(lax.dot_general(kpad[pl.ds(r0, kw), gs], qs, nt,
                                              preferred_element_type=F32))
            mask = band_ref[...]
            if mask_keys:
                key = lax.broadcasted_iota(jnp.int32, mask.shape, 0)
                mask = mask + jnp.where(key >= PRE_C - r0, 0.0, NEG).astype(F32)
            probs, sums = [], []
            for g in range(HKV_C):
                sk = jnp.concatenate([jnp.broadcast_to(sink_ref[g * per + i:g * per + i + 1, 0:1], (1, QBLK))
                                      for i in range(per)], axis=1)
                sc = scores[g] + mask
                m = jnp.maximum(jnp.max(sc, axis=0, keepdims=True), sk)
                e = jnp.exp(sc - m)
                sums.append(jnp.sum(e, axis=0, keepdims=True) + jnp.exp(sk - m))
                probs.append(e.astype(BF16))
            for g, gs in enumerate(groups):
                ot = jnp.dot(vt[gs, pl.ds(r0, kw)], probs[g], preferred_element_type=F32) / sums[g]
                for i in range(0, per, 2):
                    _pair_store(o_ref, [ot[:, (i + j) * QBLK:(i + j + 1) * QBLK] for j in (0, 1)],
                                (g * per + i) * DH_C, sub * sq, sq)

    if has_prefix:
        attend(False)
    else:
        pl.when(base < PRE_C)(lambda: attend(True))
        pl.when(base >= PRE_C)(lambda: attend(False))


def _rope_tables(pos):
    half = ROT_DIM // 2
    inv_freq = 1.0 / (ROPE_THETA ** (jnp.arange(half, dtype=F32) * (2.0 / ROT_DIM)))
    ang = pos.astype(F32)[:, None] * inv_freq[None, :]
    cos, sin = jnp.cos(ang), jnp.sin(ang)
    n = pos.shape[0]
    one = jnp.ones((n, DH_C - ROT_DIM), F32)
    zero = jnp.zeros((n, DH_C - ROT_DIM), F32)
    zh = jnp.zeros((n, half), F32)
    c = jnp.concatenate([cos, cos, one], axis=1)
    s1 = jnp.concatenate([zh, sin, zero], axis=1)
    s2 = jnp.concatenate([-sin, zh, zero], axis=1)
    rep = WC_KV // DH_C
    return tuple(jnp.concatenate([a] * rep, axis=1) for a in (c, s1, s2))


def _attn_c(u, sink, *, nb, s, row0, pos0, prefix=None, out=None):
    t = u.shape[0]
    bq = _step_rows(s)
    nq = s // bq
    s_pad = max(s, QBLK)
    kw = PRE_C + QBLK
    per = HQ_C // HKV_C
    tables = _rope_tables(pos0 + jnp.arange(s))
    band = np.where(_band_table(QBLK, PRE_C, BACK_C), 0.0, NEG).astype(np.float32)
    band = jnp.asarray(np.tile(band.T, (1, per)))
    sink_t = jnp.broadcast_to(sink.astype(F32)[:, None], (HQ_C, LANE))
    rb, sb = row0 // bq, row0 // s
    in_specs = [pl.BlockSpec((bq, WC_Q), lambda b, n: (rb + b * nq + n, COL_QC // WC_Q)),
                pl.BlockSpec((s, WC_KV), lambda b, n: (sb + b, COL_KC // WC_KV)),
                pl.BlockSpec((s, WC_KV), lambda b, n: (sb + b, COL_VC // WC_KV))]
    args = [u, u, u]
    if prefix is not None:
        in_specs += [pl.BlockSpec((1, PRE_C, WC_KV), lambda b, n: (b, 0, 0))] * 2
        args += [prefix[0], prefix[1]]
    in_specs += [pl.BlockSpec((s, WC_KV), lambda b, n: (0, 0))] * 3
    args += list(tables)
    in_specs += [pl.BlockSpec((kw, per * QBLK), lambda b, n: (0, 0)),
                 pl.BlockSpec((HQ_C, LANE), lambda b, n: (0, 0))]
    args += [band, sink_t]
    aliases = {}
    body = functools.partial(_attn_c_kernel, s=s, has_prefix=prefix is not None)
    if out is not None:
        in_specs.append(pl.BlockSpec(memory_space=pl.ANY))
        args.append(out)
        aliases = {len(args) - 1: 0}
        body = _drop_last_input(body, len(args))
    keep = min(PRE_C, s)
    return pl.pallas_call(
        body,
        grid=(nb, nq),
        in_specs=in_specs,
        out_specs=[pl.BlockSpec((bq, WC_Q), lambda b, n: (rb + b * nq + n, 0)),
                   pl.BlockSpec((1, keep, WC_KV), lambda b, n: (b, 0, 0)),
                   pl.BlockSpec((1, keep, WC_KV), lambda b, n: (b, 0, 0))],
        out_shape=[jax.ShapeDtypeStruct((t, WC_Q), BF16),
                   jax.ShapeDtypeStruct((nb, keep, WC_KV), F32),
                   jax.ShapeDtypeStruct((nb, keep, WC_KV), F32)],
        scratch_shapes=[pltpu.VMEM((PRE_C + s_pad, WC_KV), BF16), pltpu.VMEM((WC_KV, PRE_C + s_pad), BF16)],
        input_output_aliases=aliases,
        compiler_params=_cparams("parallel", "arbitrary"),
        name="mixer_c_sample" if prefix is not None else "mixer_c_prompt",
    )(*args)


def _sigmoid(x):
    return 0.5 * jnp.tanh(0.5 * x) + 0.5


def _log_sigmoid(x):
    return jnp.minimum(x, 0.0) - jnp.log(1.0 + jnp.exp(-jnp.abs(x)))


def _lane_cumsum(x):
    n = x.shape[1]
    col = lax.broadcasted_iota(jnp.int32, x.shape, 1)
    k = 1
    while k < n:
        x = x + jnp.where(col >= k, pltpu.roll(x, k, 1), 0.0)
        k *= 2
    return x


def _mlstm_kernel(qk_ref, v_ref, og_ref, g_ref, hist_ref, cw_ref, cb_ref, bif_ref, ng_ref,
                  c0_ref, n0_ref, m0_ref, y_ref, cf_ref, nf_ref, mf_ref, xt_ref,
                  xpad, c_s, n_s, m_s, *, L):
    c = pl.program_id(1)

    @pl.when(c == 0)
    def _():
        c_s[...] = c0_ref[0]
        n_s[...] = n0_ref[0]
        m_s[...] = m0_ref[0]
        xpad[0:8, :] = hist_ref[0]

    x = qk_ref[...]
    xpad[8:8 + L, :] = x
    conv = (cb_ref[...] + x * cw_ref[3:4, :] + xpad[7:7 + L, :] * cw_ref[2:3, :]
            + xpad[6:6 + L, :] * cw_ref[1:2, :] + xpad[5:5 + L, :] * cw_ref[0:1, :])
    xpad[0:8, :] = xpad[L:L + 8, :]
    act = conv * _sigmoid(conv)

    gb = g_ref[...] + bif_ref[:, 0:1]
    b_all = _lane_cumsum(_log_sigmoid(gb))
    ig_rows = gb[0:H_B, :]
    b_rows = b_all[H_B:2 * H_B, :]
    both = jnp.where(lax.broadcasted_iota(jnp.int32, gb.shape, 0) < H_B, gb, b_all)
    cols = jnp.concatenate([both, jnp.zeros((LANE - 2 * H_B, gb.shape[1]), F32)], axis=0).T

    row = lax.broadcasted_iota(jnp.int32, (L, L), 0)
    colm = lax.broadcasted_iota(jnp.int32, (L, L), 1)
    causal = colm <= row

    heads = range(H_B)
    nt = (((1,), (1,)), ((), ()))
    qf = [act[:, h * DK_B:(h + 1) * DK_B] for h in heads]
    kf = [act[:, WB_QK + h * DK_B:WB_QK + (h + 1) * DK_B] * (DK_B ** -0.5) for h in heads]
    qb = [q.astype(BF16) for q in qf]
    vb = [v_ref[:, h * DV_B:(h + 1) * DV_B].astype(BF16) for h in heads]
    cmat = [c_s[h] for h in heads]
    nvec = [n_s[h] for h in heads]
    qk = [lax.dot_general(qb[h], kf[h].astype(BF16), nt, preferred_element_type=F32) for h in heads]
    qc = [jnp.dot(qb[h], cmat[h].astype(BF16), preferred_element_type=F32) for h in heads]

    a, wi, mt, bc, igc = [], [], [], [], []
    for h in heads:
        br = b_rows[h:h + 1, 0:L]
        igr = ig_rows[h:h + 1, 0:L]
        bc.append(cols[0:L, H_B + h:H_B + h + 1])
        igc.append(cols[0:L, h:h + 1])
        logd = jnp.where(causal, bc[h] + (igr - br), NEG)
        li = bc[h] + m_s[h][:, 0:1]
        mt.append(jnp.maximum(li, jnp.max(logd, axis=1, keepdims=True)))
        wi.append(jnp.exp(li - mt[h]))
        a.append(qk[h] * jnp.exp(logd - mt[h]))

    pv = [jnp.dot(a[h].astype(BF16), vb[h], preferred_element_type=F32) for h in heads]
    for h in heads:
        num = pv[h] + wi[h] * qc[h]
        den = jnp.sum(a[h], axis=1, keepdims=True) + wi[h] * jnp.sum(qf[h] * nvec[h], axis=1, keepdims=True)
        hh = num / jnp.maximum(jnp.abs(den), jnp.exp(-mt[h]))
        hn = hh * lax.rsqrt(jnp.mean(hh * hh, axis=-1, keepdims=True) + EPS)
        vs = slice(h * DV_B, (h + 1) * DV_B)
        y_ref[:, vs] = (hn * ng_ref[:, vs] * _sigmoid(og_ref[:, vs].astype(F32))).astype(y_ref.dtype)

    for h in heads:
        bl, ml, wprev = bc[h][L - 1:L, :], mt[h][L - 1:L, :], wi[h][L - 1:L, :]
        kwt = kf[h] * jnp.exp(bl + igc[h] - bc[h] - ml)
        c_s[h] = wprev * cmat[h] + lax.dot_general(kwt.astype(BF16), vb[h], (((0,), (0,)), ((), ())),
                                                   preferred_element_type=F32)
        n_s[h] = wprev * nvec[h] + jnp.sum(kwt, axis=0, keepdims=True)
        m_s[h] = jnp.broadcast_to(ml, (1, LANE))

    @pl.when(c == pl.num_programs(1) - 1)
    def _():
        cf_ref[0] = c_s[...]
        nf_ref[0] = n_s[...]
        mf_ref[0] = m_s[...]
        xt_ref[0] = xpad[0:8, :]


def _mlstm(u, ub, g_rows, hist, cw, cb, bif, ng, c0, n0, m0, *, nb, s, row0, L, out=None):
    t = u.shape[0]
    nc = s // L
    lp = g_rows.shape[2] // nc
    rb = row0 // L
    n0 = n0.reshape(nb, H_B, 1, DK_B)
    m0 = jnp.broadcast_to(m0.reshape(nb, H_B, 1, 1), (nb, H_B, 1, LANE))
    bif_t = jnp.broadcast_to(bif.astype(F32)[:, None], (2 * H_B, LANE))
    row_spec = lambda col: pl.BlockSpec((L, WB_V), lambda b, c: (rb + b * nc + c, col))
    full = lambda shape: pl.BlockSpec(shape, lambda b, c: (0,) * len(shape))
    state = lambda shape: pl.BlockSpec((1,) + shape, lambda b, c: (b,) + (0,) * len(shape))
    in_specs = [row_spec(COL_QKB // WB_V), row_spec(COLB_VB // WB_V), row_spec(COLB_OB // WB_V),
                pl.BlockSpec((None, 2 * H_B, lp), lambda b, c: (b, 0, c)),
                state((8, 2 * WB_QK)),
                full((CONV_B, 2 * WB_QK)), full((1, 2 * WB_QK)), full((2 * H_B, LANE)), full((1, WB_V)),
                state((H_B, DK_B, DV_B)), state((H_B, 1, DK_B)), state((H_B, 1, LANE))]
    args = [u, ub, ub, g_rows, hist, cw, cb.reshape(1, -1), bif_t, ng.reshape(1, -1), c0, n0, m0]
    n_real = len(args)
    aliases = {}
    body = functools.partial(_mlstm_kernel, L=L)
    if out is not None:
        in_specs.append(pl.BlockSpec(memory_space=pl.ANY))
        args.append(out)
        aliases = {n_real: 0}
        body = _drop_last_input(body, len(args))
    y, cf, nf, mf, xt = pl.pallas_call(
        body,
        grid=(nb, nc),
        in_specs=in_specs,
        out_specs=[pl.BlockSpec((L, WB_V), lambda b, c: (rb + b * nc + c, 0)),
                   state((H_B, DK_B, DV_B)), state((H_B, 1, DK_B)), state((H_B, 1, LANE)),
                   state((8, 2 * WB_QK))],
        out_shape=[jax.ShapeDtypeStruct((t, WB_V), BF16),
                   jax.ShapeDtypeStruct((nb, H_B, DK_B, DV_B), F32),
                   jax.ShapeDtypeStruct((nb, H_B, 1, DK_B), F32),
                   jax.ShapeDtypeStruct((nb, H_B, 1, LANE), F32),
                   jax.ShapeDtypeStruct((nb, 8, 2 * WB_QK), F32)],
        scratch_shapes=[pltpu.VMEM((L + 8, 2 * WB_QK), F32),
                        pltpu.VMEM((H_B, DK_B, DV_B), F32),
                        pltpu.VMEM((H_B, 1, DK_B), F32),
                        pltpu.VMEM((H_B, 1, LANE), F32)],
        input_output_aliases=aliases,
        compiler_params=_cparams("parallel", "arbitrary"),
        name="mlstm_sample" if out is not None else "mlstm_prompt",
    )(*args)
    return y, cf, nf.reshape(nb, H_B, DK_B), mf[:, :, 0, 0], xt[:, 8 - (CONV_B - 1):, :]


def _merge_kernel(ya_ref, yb_ref, yc_ref, gt_ref, x_ref, g1_ref, wa_ref, wb_ref, wc_ref, wo_ref, o_ref):
    tm, d = x_ref.shape
    sg = jax.nn.sigmoid(gt_ref[...].astype(F32))
    m = (sg[:, 0:d] * jnp.dot(ya_ref[...].astype(BF16), wa_ref[...], preferred_element_type=F32)
         + sg[:, d:2 * d] * jnp.dot(yb_ref[...].astype(BF16), wb_ref[...], preferred_element_type=F32)
         + sg[:, 2 * d:3 * d] * jnp.dot(yc_ref[...].astype(BF16), wc_ref[...], preferred_element_type=F32))
    o = jnp.dot(m.astype(BF16), wo_ref[...], preferred_element_type=F32)
    o = (o.reshape(tm // CHUNK, CHUNK, d) * g1_ref[...]).reshape(tm, d)
    o_ref[...] = x_ref[...] + o


def _merge(ya, yb, yc, ub, x, mod_g, wa, wb, wc, wo):
    t, d = x.shape
    tm = _row_tile(t)
    row = lambda w, col=0: pl.BlockSpec((tm, w), lambda i: (i, col))
    full = lambda a: _resident(a.shape)
    return pl.pallas_call(
        _merge_kernel,
        grid=(t // tm,),
        in_specs=[row(WA), row(WB_V), row(WC_Q), row(3 * d, COLB_GATES // (3 * d)), row(d),
                  _mod_spec(tm, 2), full(wa), full(wb), full(wc), full(wo)],
        out_specs=row(d),
        out_shape=jax.ShapeDtypeStruct((t, d), F32),
        compiler_params=_cparams("parallel"),
        name="merge",
    )(ya, yb, yc, ub, x, mod_g, wa, wb, wc, wo)


def _ffn_kernel(x_ref, sh_ref, sc_ref, g_ref, g2_ref, w1_ref, w3_ref, w2_ref, o_ref, h_scr):
    tm, d = x_ref.shape
    h_scr[...] = _ln_mod(x_ref[...], g_ref[...], sc_ref[...], sh_ref[...]).astype(BF16)
    acc = None
    for c, w in _col_chunks(w1_ref.shape[1]):
        a = jnp.dot(h_scr[...], w1_ref[:, c:c + w], preferred_element_type=F32)
        b = jnp.dot(h_scr[...], w3_ref[:, c:c + w], preferred_element_type=F32)
        tt = (a * jax.nn.sigmoid(a) * b).astype(BF16)
        p = jnp.dot(tt, w2_ref[c:c + w, :], preferred_element_type=F32)
        acc = p if acc is None else acc + p
    f = (acc.reshape(tm // CHUNK, CHUNK, d) * g2_ref[...]).reshape(tm, d)
    o_ref[...] = x_ref[...] + f


def _ffn(x, mod_g, g, w1, w3, w2):
    t, d = x.shape
    tm = _row_tile(t)
    ff = w1.shape[1]
    return pl.pallas_call(
        _ffn_kernel,
        grid=(t // tm,),
        in_specs=[pl.BlockSpec((tm, d), lambda i: (i, 0)),
                  _mod_spec(tm, 3), _mod_spec(tm, 4),
                  _resident((1, d)),
                  _mod_spec(tm, 5),
                  _resident((d, ff)), _resident((d, ff)), _resident((ff, d))],
        out_specs=pl.BlockSpec((tm, d), lambda i: (i, 0)),
        out_shape=jax.ShapeDtypeStruct((t, d), F32),
        scratch_shapes=[pltpu.VMEM((tm, d), BF16)],
        compiler_params=_cparams("parallel"),
        name="ffn_dense",
    )(x, mod_g, mod_g, g.reshape(1, d), mod_g, w1, w3, w2)


def _router_kernel(x_ref, sh_ref, sc_ref, g_ref, wr_ref, br_ref, h_ref, sel_ref):
    h = _ln_mod(x_ref[...], g_ref[...], sc_ref[...], sh_ref[...])
    h_ref[...] = h.reshape(h_ref.shape)
    lg = jnp.dot(h.astype(BF16), wr_ref[...], preferred_element_type=F32) + br_ref[...]
    lane = lax.broadcasted_iota(jnp.int32, lg.shape, 1).astype(F32)
    m1 = jnp.max(lg, axis=1, keepdims=True)
    i1 = jnp.min(jnp.where(lg == m1, lane, float(LANE)), axis=1, keepdims=True)
    lg2 = jnp.where(lane == i1, 2.0 * NEG, lg)
    m2 = jnp.max(lg2, axis=1, keepdims=True)
    i2 = jnp.min(jnp.where(lg2 == m2, lane, float(LANE)), axis=1, keepdims=True)
    e2 = jnp.exp(m2 - m1)
    w1 = 1.0 / (1.0 + e2)
    sel_ref[...] = (jnp.where(lane == 0.0, i1, 0.0) + jnp.where(lane == 1.0, i2, 0.0)
                    + jnp.where(lane == 2.0, w1, 0.0) + jnp.where(lane == 3.0, e2 * w1, 0.0))


def _router(x, mod_g, g, w_r, b_r):
    t, d = x.shape
    tm = _row_tile(t)
    wr = jnp.zeros((d, LANE), BF16).at[:, :N_EXPERTS].set(w_r.astype(BF16))
    br = jnp.full((1, LANE), NEG, F32).at[0, :N_EXPERTS].set(b_r.astype(F32))
    return pl.pallas_call(
        _router_kernel,
        grid=(t // tm,),
        in_specs=[pl.BlockSpec((tm, d), lambda i: (i, 0)),
                  _mod_spec(tm, 3), _mod_spec(tm, 4),
                  pl.BlockSpec((1, d), lambda i: (0, 0)),
                  pl.BlockSpec((d, LANE), lambda i: (0, 0)),
                  pl.BlockSpec((1, LANE), lambda i: (0, 0))],
        out_specs=[pl.BlockSpec((tm // SUBLANES, SUBLANES, d), lambda i: (i, 0, 0)),
                   pl.BlockSpec((tm, LANE), lambda i: (i, 0))],
        out_shape=[jax.ShapeDtypeStruct((t // SUBLANES, SUBLANES, d), F32), jax.ShapeDtypeStruct((t, LANE), F32)],
        compiler_params=_cparams("parallel"),
        name="router",
    )(x, mod_g, mod_g, g.reshape(1, d), wr, br)


def _route(sel, tmr):
    t = sel.shape[0]
    i32 = jnp.int32
    experts = jnp.arange(N_EXPERTS, dtype=i32)[None, :]
    e = sel[:, 0:2].astype(i32).reshape(-1)
    oh = (e[:, None] == experts).astype(i32)
    csum = jnp.cumsum(oh, axis=0)
    cnt = csum[-1]
    rank = jnp.sum(csum * oh, axis=1) - 1
    ntile_e = (cnt + tmr - 1) // tmr
    tile_end = jnp.cumsum(ntile_e)
    tile_start = tile_end - ntile_e
    pos = jnp.sum(oh * (tile_start * tmr)[None, :], axis=1) + rank
    n_tiles = -(-2 * t // tmr) + N_EXPERTS
    n_used = tile_end[-1]
    tiles = jnp.arange(n_tiles, dtype=i32)
    tc = jnp.minimum(tiles, n_used - 1)
    te = jnp.sum((tc[:, None] >= tile_end[None, :]).astype(i32), axis=1)
    ohe = (te[:, None] == experts).astype(i32)
    nvalid = jnp.sum(ohe * cnt[None, :], axis=1) - (tc - jnp.sum(ohe * tile_start[None, :], axis=1)) * tmr
    nvalid = jnp.where(tiles < n_used, jnp.clip(nvalid, 0, tmr), 0)
    r = jnp.arange(EXPERT_SUB, dtype=i32)[None, :]
    n_pad = ((-cnt) % EXPERT_SUB)[:, None]
    spare = n_tiles * tmr + jnp.arange(N_EXPERTS * EXPERT_SUB, dtype=i32).reshape(N_EXPERTS, EXPERT_SUB)
    pad_rows = jnp.where(r < n_pad, (tile_start * tmr + cnt)[:, None] + r, spare).reshape(-1)
    return pos.astype(i32), te.astype(i32), nvalid.astype(i32), n_used.reshape(1).astype(i32), pad_rows.astype(i32)


EXPERT_TILE = 1024
EXPERT_SUB = 256
ROUTE_TILE = 256


def _row_copies(n_groups, make_copy):
    def issue(g, carry):
        for sub in range(SUBLANES):
            for k in range(2):
                make_copy(g, sub, k).start(priority=k)
        return carry

    def drain(g, carry):
        for sub in range(SUBLANES):
            for k in range(2):
                make_copy(g, sub, k).wait()
        return carry

    lax.fori_loop(0, n_groups, issue, 0)
    lax.fori_loop(0, n_groups, drain, 0)


def _slot_row(hi_ref, lo_ref, g, sub, k):
    idx = 2 * SUBLANES * g + 2 * sub + k
    return hi_ref[0, 0, idx], lo_ref[0, 0, idx]


def _split_pos(pos, t, tm):
    hi = lax.shift_right_logical(pos, 3).reshape(t // tm, 1, 2 * tm)
    lo = jnp.bitwise_and(pos, SUBLANES - 1).reshape(t // tm, 1, 2 * tm)
    return hi, lo


def _dispatch_kernel(hi_ref, lo_ref, pad_hi_ref, pad_lo_ref, h_ref, xs_ref, zrow, sem):
    @pl.when(pl.program_id(0) == 0)
    def _():
        zrow[...] = jnp.zeros_like(zrow)

        def zero_copy(g, sub, k):
            hi, lo = _slot_row(pad_hi_ref, pad_lo_ref, g, sub, k)
            return pltpu.make_async_copy(zrow.at[pl.ds(sub, 1), :], xs_ref.at[hi, pl.ds(lo, 1), :], sem)

        _row_copies(pad_hi_ref.shape[2] // (2 * SUBLANES), zero_copy)

    def make_copy(g, sub, k):
        hi, lo = _slot_row(hi_ref, lo_ref, g, sub, k)
        return pltpu.make_async_copy(h_ref.at[g, pl.ds(sub, 1), :], xs_ref.at[hi, pl.ds(lo, 1), :], sem)

    _row_copies(h_ref.shape[0], make_copy)


def _dispatch(h, pos, pad_rows, n_rows):
    t, d = h.shape[0] * SUBLANES, h.shape[2]
    tm = min(ROUTE_TILE, _row_tile(t))
    n_pad = pad_rows.shape[0]
    smem = pl.BlockSpec((1, 1, 2 * tm), lambda i: (i, 0, 0), memory_space=pltpu.SMEM)
    smem_pad = pl.BlockSpec((1, 1, n_pad), lambda i: (0, 0, 0), memory_space=pltpu.SMEM)
    pad_hi = lax.shift_right_logical(pad_rows, 3).reshape(1, 1, n_pad)
    pad_lo = jnp.bitwise_and(pad_rows, SUBLANES - 1).reshape(1, 1, n_pad)
    return pl.pallas_call(
        _dispatch_kernel,
        grid=(t // tm,),
        in_specs=[smem, smem, smem_pad, smem_pad,
                  pl.BlockSpec((tm // SUBLANES, SUBLANES, d), lambda i: (i, 0, 0))],
        out_specs=pl.BlockSpec(memory_space=pl.ANY),
        out_shape=jax.ShapeDtypeStruct(((n_rows + n_pad) // SUBLANES, SUBLANES, d), F32),
        scratch_shapes=[pltpu.VMEM((SUBLANES, d), F32), pltpu.SemaphoreType.DMA(())],
        compiler_params=_cparams("arbitrary"),
        name="moe_dispatch",
    )(*_split_pos(pos, t, tm), pad_hi, pad_lo, h)


def _experts_kernel(te_ref, nv_ref, nu_ref, x_ref, w1_ref, w3_ref, w2_ref, y_ref, xb, acc):
    del te_ref, nu_ref
    i, j = pl.program_id(0), pl.program_id(1)
    nv = nv_ref[i]

    @pl.when(nv > 0)
    def _():
        @pl.when(j == 0)
        def _():
            acc[...] = jnp.zeros_like(acc)

        for m in range(EXPERT_SUB, EXPERT_TILE + 1, EXPERT_SUB):
            @pl.when((nv > m - EXPERT_SUB) & (nv <= m))
            def _(m=m):
                @pl.when(j == 0)
                def _():
                    xb[0:m, :] = x_ref[0:m, :].astype(BF16)

                xs = xb[0:m, :]
                a = jnp.dot(xs, w1_ref[0].astype(BF16), preferred_element_type=F32)
                b = jnp.dot(xs, w3_ref[0].astype(BF16), preferred_element_type=F32)
                tt = (a * jax.nn.sigmoid(a) * b).astype(BF16)
                acc[0:m, :] += jnp.dot(tt, w2_ref[0].astype(BF16), preferred_element_type=F32)

        @pl.when(j == pl.num_programs(1) - 1)
        def _():
            y_ref[...] = acc[...]


def _experts(xs, te, nvalid, n_used, w1, w3, w2):
    d = xs.shape[1]
    ne, _, ff = w1.shape
    tf = 512
    nj = ff // tf
    n_tiles = te.shape[0]
    n_rows = n_tiles * EXPERT_TILE
    tile_rows = (EXPERT_TILE, d)

    def row_map(i, j, te_ref, nv_ref, nu_ref):
        return (jnp.minimum(i, nu_ref[0] - 1), 0)

    def col(i, j, nu_ref):
        return jnp.where(i < nu_ref[0], j, nj - 1)

    grid_spec = pltpu.PrefetchScalarGridSpec(
        num_scalar_prefetch=3,
        grid=(n_tiles, nj),
        in_specs=[pl.BlockSpec(tile_rows, row_map),
                  pl.BlockSpec((1, d, tf), lambda i, j, te_ref, nv_ref, nu_ref: (te_ref[i], 0, col(i, j, nu_ref))),
                  pl.BlockSpec((1, d, tf), lambda i, j, te_ref, nv_ref, nu_ref: (te_ref[i], 0, col(i, j, nu_ref))),
                  pl.BlockSpec((1, tf, d), lambda i, j, te_ref, nv_ref, nu_ref: (te_ref[i], col(i, j, nu_ref), 0))],
        out_specs=pl.BlockSpec(tile_rows, row_map),
        scratch_shapes=[pltpu.VMEM((EXPERT_TILE, d), BF16), pltpu.VMEM((EXPERT_TILE, d), F32)])
    return pl.pallas_call(
        _experts_kernel,
        grid_spec=grid_spec,
        out_shape=jax.ShapeDtypeStruct((n_rows, d), F32),
        compiler_params=_cparams("arbitrary", "arbitrary"),
        name="moe_experts",
    )(te, nvalid, n_used, xs, w1, w3, w2)


def _combine_kernel(hi_ref, lo_ref, sel_ref, x_ref, g2_ref, y_ref, *rest, n_prompt_tiles):
    tm, d = x_ref.shape
    rows, sem = rest[-2:]

    def make_copy(g, sub, k):
        hi, lo = _slot_row(hi_ref, lo_ref, g, sub, k)
        return pltpu.make_async_copy(y_ref.at[hi, pl.ds(lo, 1), :], rows.at[k, g, pl.ds(sub, 1), :], sem)

    _row_copies(tm // SUBLANES, make_copy)
    sel = sel_ref[...]
    f = sel[:, 2:3] * rows[0].reshape(tm, d) + sel[:, 3:4] * rows[1].reshape(tm, d)
    f = (f.reshape(tm // CHUNK, CHUNK, d) * g2_ref[...]).reshape(tm, d)
    xo = x_ref[...] + f
    if n_prompt_tiles is None:
        rest[0][...] = xo
        return
    gf_ref, op_ref, os_ref = rest[:3]
    yo = xo * lax.rsqrt(jnp.mean(xo * xo, axis=-1, keepdims=True) + EPS) * gf_ref[...]
    i = pl.program_id(0)

    @pl.when(i < n_prompt_tiles)
    def _():
        op_ref[...] = yo

    @pl.when(i >= n_prompt_tiles)
    def _():
        os_ref[...] = yo


def _combine(y, pos, sel, x, mod_g, final=None):
    t, d = x.shape
    tm = min(ROUTE_TILE, _row_tile(t if final is None else np.gcd(final[1], t - final[1])))
    smem = pl.BlockSpec((1, 1, 2 * tm), lambda i: (i, 0, 0), memory_space=pltpu.SMEM)
    in_specs = [smem, smem,
                pl.BlockSpec((tm, LANE), lambda i: (i, 0)),
                pl.BlockSpec((tm, d), lambda i: (i, 0)),
                _mod_spec(tm, 5),
                pl.BlockSpec(memory_space=pl.ANY)]
    args = [*_split_pos(pos, t, tm), sel, x, mod_g, y]
    if final is None:
        n_p = None
        out_specs = pl.BlockSpec((tm, d), lambda i: (i, 0))
        out_shape = jax.ShapeDtypeStruct((t, d), F32)
    else:
        gain, tp = final
        n_p = tp // tm
        in_specs.append(pl.BlockSpec((1, d), lambda i: (0, 0)))
        args.append(gain.reshape(1, d))
        out_specs = [pl.BlockSpec((tm, d), lambda i: (jnp.minimum(i, n_p - 1), 0)),
                     pl.BlockSpec((tm, d), lambda i: (jnp.maximum(i - n_p, 0), 0))]
        out_shape = [jax.ShapeDtypeStruct((tp, d), F32), jax.ShapeDtypeStruct((t - tp, d), F32)]
    return pl.pallas_call(
        functools.partial(_combine_kernel, n_prompt_tiles=n_p),
        grid=(t // tm,),
        in_specs=in_specs,
        out_specs=out_specs,
        out_shape=out_shape,
        scratch_shapes=[pltpu.VMEM((2, tm // SUBLANES, SUBLANES, d), F32), pltpu.SemaphoreType.DMA(())],
        compiler_params=_cparams("arbitrary"),
        name="moe_combine",
    )(*args)


def _moe(x, mod_g, g, w_r, b_r, w1, w3, w2, final=None):
    d = x.shape[1]
    h2, sel = _router(x, mod_g, g, w_r, b_r)
    pos, te, nvalid, n_used, pad_rows = _route(sel, EXPERT_TILE)
    n_rows = te.shape[0] * EXPERT_TILE
    xs = _dispatch(h2, pos, pad_rows, n_rows)
    y = _experts(xs.reshape(-1, d), te, nvalid, n_used, w1, w3, w2)
    return _combine(y.reshape(n_rows // SUBLANES, SUBLANES, d), pos, sel, x, mod_g, final)


def _final_kernel(x_ref, g_ref, op_ref, os_ref, *, n_prompt_tiles):
    x = x_ref[...]
    y = x * lax.rsqrt(jnp.mean(x * x, axis=-1, keepdims=True) + EPS) * g_ref[...]
    i = pl.program_id(0)

    @pl.when(i < n_prompt_tiles)
    def _():
        op_ref[...] = y

    @pl.when(i >= n_prompt_tiles)
    def _():
        os_ref[...] = y


def _final_norm(x, g, tp):
    t, d = x.shape
    tm = _row_tile(np.gcd(tp, t - tp))
    n_p = tp // tm
    return pl.pallas_call(
        functools.partial(_final_kernel, n_prompt_tiles=n_p),
        grid=(t // tm,),
        in_specs=[pl.BlockSpec((tm, d), lambda i: (i, 0)), pl.BlockSpec((1, d), lambda i: (0, 0))],
        out_specs=[pl.BlockSpec((tm, d), lambda i: (jnp.minimum(i, n_p - 1), 0)),
                   pl.BlockSpec((tm, d), lambda i: (jnp.maximum(i - n_p, 0), 0))],
        out_shape=[jax.ShapeDtypeStruct((tp, d), F32), jax.ShapeDtypeStruct((t - tp, d), F32)],
        compiler_params=_cparams("arbitrary"),
        name="final_norm",
    )(x, g.reshape(1, d))


def _w_prep_kernel(w_ref, lo_ref, hi_ref, gate_ref):
    w = w_ref[...]
    gate0, gate1 = W_LO_COLS, W_LO_COLS + 2 * H_B
    lo_ref[...] = w[:, :gate0].astype(BF16)
    gate_ref[...] = w[:, gate0:gate1]
    hi_ref[...] = w[:, gate1:].astype(BF16)


def _split_w_in(w_all, layer):
    _, d, n = w_all.shape
    rows = 256
    lo, hi, gate = pl.pallas_call(
        _w_prep_kernel,
        grid=(d // rows,),
        in_specs=[pl.BlockSpec((None, rows, n), lambda i: (layer, i, 0))],
        out_specs=[pl.BlockSpec((rows, W_LO_COLS), lambda i: (i, 0)),
                   pl.BlockSpec((rows, W_HI_COLS), lambda i: (i, 0)),
                   pl.BlockSpec((rows, 2 * H_B), lambda i: (i, 0))],
        out_shape=[jax.ShapeDtypeStruct((d, W_LO_COLS), BF16), jax.ShapeDtypeStruct((d, W_HI_COLS), BF16),
                   jax.ShapeDtypeStruct((d, 2 * H_B), F32)],
        compiler_params=_cparams("parallel"),
        name="w_in_prep",
    )(w_all)
    return lo, hi, gate.T.astype(BF16)


def kernel(x_prompt, x_sample, c_prompt, c_sample, cache_a_k, cache_a_v, state_b_C, state_b_n, state_b_m, state_b_conv, cache_c_k, cache_c_v, norm1_g, norm2_g, w_ada, b_ada, w_in, b_if_b, conv_w_b, conv_b_b, norm_b_g, rel_a, sink_c, w_br_a, w_br_b, w_br_c, w_o, w_ff1, w_ff3, w_ff2, w_router, b_router, w_e1, w_e3, w_e2, norm_f_g):
    nbp, sp, d = x_prompt.shape
    nbs, ss, _ = x_sample.shape
    tp, ts = nbp * sp, nbs * ss
    x = jnp.concatenate([x_prompt.reshape(tp, d), x_sample.reshape(ts, d)], axis=0)
    cond = jnp.concatenate([c_prompt, c_sample], axis=0)
    group_batch = np.concatenate([np.repeat(np.arange(nbp), sp // CHUNK), nbp + np.repeat(np.arange(nbs), ss // CHUNK)])

    lp_s = max(ss, LANE)
    l_b = 256 if sp % 256 == 0 else CHUNK
    keep_a, keep_c = min(PRE_A, sp), min(PRE_C, sp)
    zeros_state = (jnp.zeros((nbp, H_B, DK_B, DV_B), F32), jnp.zeros((nbp, H_B, DK_B), F32), jnp.zeros((nbp, H_B), F32))
    new = {k: [] for k in ("pak", "pav", "pbc", "pbn", "pbm", "pbx", "pck", "pcv",
                           "sak", "sav", "sbc", "sbn", "sbm", "sbx", "sck", "scv")}

    for l in range(DEPTH):
        mod = _ada(cond, w_ada, b_ada, l)
        mod_g = mod[group_batch].reshape(-1, 1, 6 * d)
        u, ub, g_t = _inproj(x, mod_g, norm1_g[l], *_split_w_in(w_in, l))

        bias_t = _rel_bias_table(rel_a[l])
        ya, pak, pav = _attn_a(u, bias_t, nb=nbp, s=sp, row0=0)
        ya, sak, sav = _attn_a(u, bias_t, nb=nbs, s=ss, row0=tp,
                               prefix=(cache_a_k[l].reshape(nbs, PRE_A, WA), cache_a_v[l].reshape(nbs, PRE_A, WA)),
                               out=ya)
        yc, pck, pcv = _attn_c(u, sink_c[l], nb=nbp, s=sp, row0=0, pos0=0)
        yc, sck, scv = _attn_c(u, sink_c[l], nb=nbs, s=ss, row0=tp, pos0=PAST_LEN,
                               prefix=(cache_c_k[l].reshape(nbs, PRE_C, WC_KV),
                                       cache_c_v[l].reshape(nbs, PRE_C, WC_KV)), out=yc)
        g_p = g_t[:, :tp].reshape(2 * H_B, nbp, sp).transpose(1, 0, 2)
        g_s = g_t[:, tp:].reshape(2 * H_B, nbs, ss).transpose(1, 0, 2)
        g_s = jnp.pad(g_s, ((0, 0), (0, 0), (0, lp_s - ss)))
        hist_p = jnp.zeros((nbp, 8, 2 * WB_QK), F32)
        hist_s = jnp.pad(state_b_conv[l], ((0, 0), (8 - (CONV_B - 1), 0), (0, 0)))
        bargs = (conv_w_b[l], conv_b_b[l], b_if_b[l], norm_b_g[l])
        yb, pbc, pbn, pbm, pbx = _mlstm(u, ub, g_p, hist_p, *bargs, *zeros_state, nb=nbp, s=sp, row0=0, L=l_b)
        yb, sbc, sbn, sbm, sbx = _mlstm(u, ub, g_s, hist_s, *bargs, state_b_C[l], state_b_n[l], state_b_m[l],
                                        nb=nbs, s=ss, row0=tp, L=ss, out=yb)

        x = _merge(ya, yb, yc, ub, x, mod_g, w_br_a[l].astype(BF16), w_br_b[l].astype(BF16),
                   w_br_c[l].astype(BF16), w_o[l].astype(BF16))
        i = l // 2
        if l % 2 == 0:
            x = _ffn(x, mod_g, norm2_g[l], w_ff1[i].astype(BF16), w_ff3[i].astype(BF16), w_ff2[i].astype(BF16))
        else:
            x = _moe(x, mod_g, norm2_g[l], w_router[i], b_router[i], w_e1[i], w_e3[i], w_e2[i],
                     final=(norm_f_g, tp) if l == DEPTH - 1 else None)

        for name, val in (("pak", pak), ("pav", pav), ("pbc", pbc), ("pbn", pbn), ("pbm", pbm), ("pbx", pbx),
                          ("pck", pck), ("pcv", pcv), ("sak", sak), ("sav", sav), ("sbc", sbc), ("sbn", sbn),
                          ("sbm", sbm), ("sbx", sbx), ("sck", sck), ("scv", scv)):
            new[name].append(val)

    y_p, y_s = x if DEPTH % 2 == 0 else _final_norm(x, norm_f_g, tp)
    heads = {"pak": (H_A, DH_A), "pav": (H_A, DH_A), "sak": (H_A, DH_A), "sav": (H_A, DH_A),
             "pck": (HKV_C, DH_C), "pcv": (HKV_C, DH_C), "sck": (HKV_C, DH_C), "scv": (HKV_C, DH_C)}
    st = {k: jnp.stack(v) for k, v in new.items()}
    st = {k: v.reshape(v.shape[:-1] + heads[k]) if k in heads else v for k, v in st.items()}
    return (y_p.reshape(nbp, sp, d), y_s.reshape(nbs, ss, d),
            st["pak"], st["pav"], st["pbc"], st["pbn"], st["pbm"], st["pbx"], st["pck"], st["pcv"],
            st["sak"], st["sav"], st["sbc"], st["sbn"], st["sbm"], st["sbx"], st["sck"], st["scv"])
```

```python
import functools

import numpy as np
import jax
import jax.numpy as jnp
from jax import lax
from jax.experimental import pallas as pl
from jax.experimental.pallas import tpu as pltpu

F32 = jnp.float32
BF16 = jnp.bfloat16

D_MODEL = 1024
DEPTH = 2
PAST_LEN = 2048
CHUNK = 64
H_A, DH_A, BACK_A, REL_CLIP = 8, 64, 8, 128
H_B, DK_B, DV_B, CONV_B = 4, 128, 256, 4
HQ_C, HKV_C, DH_C, BACK_C = 8, 2, 64, 2
ROT_DIM = DH_C // 4
ROPE_THETA = 500000.0
D_FF = 2816
N_EXPERTS = 8
D_FF_E = 3584
EPS = 1e-6
NEG = -1e30

WA = H_A * DH_A
WB_QK = H_B * DK_B
WB_V = H_B * DV_B
WC_Q = HQ_C * DH_C
WC_KV = HKV_C * DH_C
PRE_A = BACK_A * CHUNK
PRE_C = BACK_C * CHUNK

COL_QA, COL_KA, COL_VA, COL_QC = 0, 512, 1024, 1536
COL_QKB = 2048
COL_KC = 3072
COL_VC = 3200
N_F32 = 3328
COLB_GATES = 0
COLB_VB = 3072
COLB_OB = 4096
N_B16 = 5120
N_MAIN = N_F32 + N_B16

LANE = 128
SUBLANES = 8
VMEM_LIMIT = 48 * 1024 * 1024
INPROJ_VMEM_SLACK = 8 * 1024 * 1024


def _cparams(*sem, vmem=VMEM_LIMIT):
    return pltpu.CompilerParams(dimension_semantics=sem, vmem_limit_bytes=vmem)


def _row_tile(t):
    for tm in (512, 256, 128, 64):
        if t % tm == 0:
            return tm
    raise ValueError(f"token count {t} is not a multiple of 64")


def _ln_mod(x, g, sc, sh):
    tm, d = x.shape
    y = x * lax.rsqrt(jnp.mean(x * x, axis=-1, keepdims=True) + EPS) * g
    y = y.reshape(tm // CHUNK, CHUNK, d) * (1.0 + sc) + sh
    return y.reshape(tm, d)


def _mod_spec(tm, kind):
    return pl.BlockSpec((tm // CHUNK, 1, D_MODEL), lambda i, *_, k=kind: (i, 0, k))


def _ada_kernel(c_ref, w_ref, b_ref, o_ref):
    c = c_ref[...]
    a = (c * jax.nn.sigmoid(c)).astype(BF16)
    o_ref[...] = jnp.dot(a, w_ref[...].astype(BF16), preferred_element_type=F32) + b_ref[...]


def _ada(c, w, b, layer):
    nb, d = c.shape
    n = w.shape[2]
    tn = 1536
    return pl.pallas_call(
        _ada_kernel,
        grid=(n // tn,),
        in_specs=[pl.BlockSpec((nb, d), lambda j: (0, 0)),
                  pl.BlockSpec((None, d, tn), lambda j: (layer, 0, j)),
                  pl.BlockSpec((None, 1, tn), lambda j: (layer, 0, j))],
        out_specs=pl.BlockSpec((nb, tn), lambda j: (0, j)),
        out_shape=jax.ShapeDtypeStruct((nb, n), F32),
        compiler_params=_cparams("arbitrary"),
        name="adaln",
    )(c, w, b.reshape(b.shape[0], 1, n))


def _col_chunks(n, width=768):
    return [(c, min(width, n - c)) for c in range(0, n, width)]


def _resident(shape):
    return pl.BlockSpec(shape, lambda *_: (0,) * len(shape), pipeline_mode=pl.Buffered(1))


W_LO_COLS = 3 * WA + 2 * WB_QK + WB_V
W_HI_COLS = WB_V + WC_Q + 2 * WC_KV + 3 * D_MODEL
_INPROJ_MOVES = (
    ("lo", 0, False, COL_QA, 3 * WA),
    ("hi", WB_V, False, COL_QC, WC_Q),
    ("lo", 3 * WA, False, COL_QKB, 2 * WB_QK),
    ("hi", WB_V + WC_Q, False, COL_KC, 2 * WC_KV),
    ("hi", WB_V + WC_Q + 2 * WC_KV, True, COLB_GATES, 3 * D_MODEL),
    ("lo", 3 * WA + 2 * WB_QK, True, COLB_VB, WB_V),
    ("hi", 0, True, COLB_OB, WB_V),
)


def _inproj_kernel(x_ref, sh_ref, sc_ref, g_ref, wlo_ref, whi_ref, wif_ref, u_ref, ub_ref, gt_ref, h_scr):
    h_scr[...] = _ln_mod(x_ref[...], g_ref[...], sc_ref[...], sh_ref[...]).astype(BF16)
    gt_ref[...] = lax.dot_general(wif_ref[...], h_scr[...], (((1,), (1,)), ((), ())), preferred_element_type=F32)
    for half, src, to_b16, dst, width in _INPROJ_MOVES:
        w_ref = wlo_ref if half == "lo" else whi_ref
        for c, w in _col_chunks(width):
            r = jnp.dot(h_scr[...], w_ref[:, src + c:src + c + w], preferred_element_type=F32)
            if to_b16:
                ub_ref[:, dst + c:dst + c + w] = r.astype(BF16)
            else:
                u_ref[:, dst + c:dst + c + w] = r


def _inproj(x, mod_g, g, w_lo, w_hi, w_if_t):
    t, d = x.shape
    tm = _row_tile(t)
    vmem = (d * N_MAIN * 2 + 2 * tm * (d * 4 + N_F32 * 4 + N_B16 * 2) + tm * d * 2) + INPROJ_VMEM_SLACK
    return pl.pallas_call(
        _inproj_kernel,
        grid=(t // tm,),
        in_specs=[pl.BlockSpec((tm, d), lambda i: (i, 0)),
                  _mod_spec(tm, 0), _mod_spec(tm, 1),
                  _resident((1, d)), _resident((d, W_LO_COLS)), _resident((d, W_HI_COLS)),
                  _resident((2 * H_B, d))],
        out_specs=[pl.BlockSpec((tm, N_F32), lambda i: (i, 0)),
                   pl.BlockSpec((tm, N_B16), lambda i: (i, 0)),
                   pl.BlockSpec((2 * H_B, tm), lambda i: (0, i))],
        out_shape=[jax.ShapeDtypeStruct((t, N_F32), F32),
                   jax.ShapeDtypeStruct((t, N_B16), BF16),
                   jax.ShapeDtypeStruct((2 * H_B, t), F32)],
        scratch_shapes=[pltpu.VMEM((tm, d), BF16)],
        compiler_params=_cparams("parallel", vmem=vmem),
        name="inproj",
    )(x, mod_g, mod_g, g.reshape(1, d), w_lo, w_hi, w_if_t)


def _band_table(bq, pre, back):
    qi = np.arange(bq)[:, None] // CHUNK
    kj = np.arange(pre + bq)[None, :] // CHUNK
    return (kj >= qi) & (kj <= qi + back)


QBLK = 128
STEP_QBLKS = 4
KV_CHUNK = 512


def _step_rows(s):
    return STEP_QBLKS * QBLK if s % (STEP_QBLKS * QBLK) == 0 else min(QBLK, s)


def _pad_rows(x, rows):
    return x if x.shape[0] == rows else jnp.concatenate(
        [x, jnp.zeros((rows - x.shape[0], x.shape[1]), x.dtype)], axis=0)


def _stage_keys_values(k_rows, v_ref, prefix_k, prefix_v, kpad, vt, pre, s):
    w = kpad.shape[1]
    s_pad = kpad.shape[0] - pre
    if prefix_k is None:
        kpad[0:pre, :] = jnp.zeros((pre, w), BF16)
        vt[:, 0:pre] = jnp.zeros((w, pre), BF16)
    else:
        kpad[0:pre, :] = prefix_k.astype(BF16)
        vt[:, 0:pre] = prefix_v.T.astype(BF16)
    for c in range(0, s_pad, KV_CHUNK):
        rows = min(KV_CHUNK, s_pad - c)
        real = max(0, min(rows, s - c))
        kpad[pre + c:pre + c + rows, :] = _pad_rows(k_rows(c, real), rows).astype(BF16)
        vt[:, pre + c:pre + c + rows] = _pad_rows(v_ref[c:c + real, :], rows).T.astype(BF16)


def _pair_store(o_ref, ots, lane0, row0, rows):
    pair = jnp.concatenate(ots, axis=0).T
    o_ref[row0:row0 + rows, lane0:lane0 + pair.shape[1]] = pair[0:rows, :].astype(o_ref.dtype)


def _query_blocks(q_ref):
    rows = q_ref.shape[0]
    return max(1, rows // QBLK), min(QBLK, rows)


def _attn_a_kernel(*refs, s, has_prefix):
    if has_prefix:
        q_ref, k_ref, v_ref, pk_ref, pv_ref, bias_ref, o_ref, kt_ref, vt_ref, kpad, vt = refs
    else:
        q_ref, k_ref, v_ref, bias_ref, o_ref, kt_ref, vt_ref, kpad, vt = refs
    n = pl.program_id(1)
    kw = PRE_A + QBLK
    keep = kt_ref.shape[1]
    nsub, sq = _query_blocks(q_ref)

    @pl.when(n == 0)
    def _():
        _stage_keys_values(lambda c, rows: k_ref[c:c + rows, :], v_ref,
                           pk_ref[0] if has_prefix else None, pv_ref[0] if has_prefix else None,
                           kpad, vt, PRE_A, s)
        kt_ref[0] = k_ref[s - keep:s, :]
        vt_ref[0] = v_ref[s - keep:s, :]

    base = pl.multiple_of(n * (nsub * QBLK), nsub * QBLK)
    heads = [slice(h * DH_A, (h + 1) * DH_A) for h in range(H_A)]
    nt = (((1,), (1,)), ((), ()))

    def attend(mask_keys):
        for sub in range(nsub):
            r0 = base + sub * QBLK
            q = _pad_rows((q_ref[sub * sq:(sub + 1) * sq, :] * (DH_A ** -0.5)).astype(BF16), QBLK)
            scores = [lax.dot_general(kpad[pl.ds(r0, kw), hs], q[:, hs], nt, preferred_element_type=F32)
                      for hs in heads]
            if mask_keys:
                key = lax.broadcasted_iota(jnp.int32, (kw, QBLK), 0)
                invalid = jnp.where(key >= PRE_A - r0, 0.0, NEG).astype(F32)
            probs, sums = [], []
            for h in range(H_A):
                sc = scores[h] + bias_ref[h]
                if mask_keys:
                    sc = sc + invalid
                e = jnp.exp(sc - jnp.max(sc, axis=0, keepdims=True))
                sums.append(jnp.sum(e, axis=0, keepdims=True))
                probs.append(e.astype(BF16))
            for h in range(0, H_A, 2):
                ots = [jnp.dot(vt[heads[i], pl.ds(r0, kw)], probs[i], preferred_element_type=F32) / sums[i]
                       for i in (h, h + 1)]
                _pair_store(o_ref, ots, h * DH_A, sub * sq, sq)

    if has_prefix:
        attend(False)
    else:
        pl.when(base < PRE_A)(lambda: attend(True))
        pl.when(base >= PRE_A)(lambda: attend(False))


def _rel_bias_table(rel):
    kw = PRE_A + QBLK
    j0 = PRE_A - REL_CLIP
    rows = kw - j0
    n = QBLK + rows
    m = np.arange(n)
    relf = rel.astype(F32)
    diag = relf[np.clip(m + 1 - QBLK, -REL_CLIP, REL_CLIP) + REL_CLIP].T
    skew = jnp.tile(diag, (1, rows + 1))[:, :rows * (n - 1)].reshape(H_A, rows, n - 1)
    far = jnp.broadcast_to(relf[2 * REL_CLIP][:, None, None], (H_A, j0, QBLK))
    table = jnp.concatenate([far, skew[:, :, rows - 1:rows - 1 + QBLK]], axis=1)
    return jnp.where(_band_table(QBLK, PRE_A, BACK_A).T[None], table, NEG)


def _attn_a(u, bias_t, *, nb, s, row0, prefix=None, out=None):
    t = u.shape[0]
    bq = _step_rows(s)
    nq = s // bq
    s_pad = max(s, QBLK)
    kw = PRE_A + QBLK
    keep = min(PRE_A, s)
    rb, sb = row0 // bq, row0 // s
    in_specs = [pl.BlockSpec((bq, WA), lambda b, n: (rb + b * nq + n, COL_QA // WA)),
                pl.BlockSpec((s, WA), lambda b, n: (sb + b, COL_KA // WA)),
                pl.BlockSpec((s, WA), lambda b, n: (sb + b, COL_VA // WA))]
    args = [u, u, u]
    if prefix is not None:
        in_specs += [pl.BlockSpec((1, PRE_A, WA), lambda b, n: (b, 0, 0))] * 2
        args += [prefix[0], prefix[1]]
    in_specs.append(pl.BlockSpec((H_A, kw, QBLK), lambda b, n: (0, 0, 0)))
    args.append(bias_t)
    aliases = {}
    if out is not None:
        in_specs.append(pl.BlockSpec(memory_space=pl.ANY))
        args.append(out)
        aliases = {len(args) - 1: 0}
    body = functools.partial(_attn_a_kernel, s=s, has_prefix=prefix is not None)
    if out is not None:
        body = _drop_last_input(body, len(args))
    return pl.pallas_call(
        body,
        grid=(nb, nq),
        in_specs=in_specs,
        out_specs=[pl.BlockSpec((bq, WA), lambda b, n: (rb + b * nq + n, 0)),
                   pl.BlockSpec((1, keep, WA), lambda b, n: (b, 0, 0)),
                   pl.BlockSpec((1, keep, WA), lambda b, n: (b, 0, 0))],
        out_shape=[jax.ShapeDtypeStruct((t, WA), BF16),
                   jax.ShapeDtypeStruct((nb, keep, WA), F32),
                   jax.ShapeDtypeStruct((nb, keep, WA), F32)],
        scratch_shapes=[pltpu.VMEM((PRE_A + s_pad, WA), BF16), pltpu.VMEM((WA, PRE_A + s_pad), BF16)],
        input_output_aliases=aliases,
        compiler_params=_cparams("parallel", "arbitrary"),
        name="mixer_a_sample" if prefix is not None else "mixer_a_prompt",
    )(*args)


def _drop_last_input(body, n_in):
    def wrapped(*refs):
        return body(*refs[:n_in - 1], *refs[n_in:])
    return wrapped


def _rope(x, c, s1, s2):
    w = x.shape[1]
    return x * c + pltpu.roll(x, 8, 1) * s1 + pltpu.roll(x, w - 8, 1) * s2


def _attn_c_kernel(*refs, s, has_prefix):
    if has_prefix:
        (q_ref, k_ref, v_ref, pk_ref, pv_ref, rc_ref, rs1_ref, rs2_ref, band_ref, sink_ref,
         o_ref, kt_ref, vt_ref, kpad, vt) = refs
    else:
        (q_ref, k_ref, v_ref, rc_ref, rs1_ref, rs2_ref, band_ref, sink_ref,
         o_ref, kt_ref, vt_ref, kpad, vt) = refs
    n = pl.program_id(1)
    kw = PRE_C + QBLK
    keep = kt_ref.shape[1]
    nsub, sq = _query_blocks(q_ref)

    def rope_at(x, start, rep=1):
        tables = [jnp.concatenate([t_ref[pl.ds(start, x.shape[0]), :]] * rep, axis=1)
                  for t_ref in (rc_ref, rs1_ref, rs2_ref)]
        return _rope(x, *tables)

    @pl.when(n == 0)
    def _():
        _stage_keys_values(lambda c, rows: rope_at(k_ref[c:c + rows, :], c), v_ref,
                           pk_ref[0] if has_prefix else None, pv_ref[0] if has_prefix else None,
                           kpad, vt, PRE_C, s)
        kt_ref[0] = rope_at(k_ref[s - keep:s, :], s - keep)
        vt_ref[0] = v_ref[s - keep:s, :]

    base = pl.multiple_of(n * (nsub * QBLK), nsub * QBLK)
    per = HQ_C // HKV_C
    groups = [slice(g * DH_C, (g + 1) * DH_C) for g in range(HKV_C)]
    nt = (((1,), (1,)), ((), ()))

    def attend(mask_keys):
        for sub in range(nsub):
            r0 = base + sub * QBLK
            q = rope_at(q_ref[sub * sq:(sub + 1) * sq, :], pl.multiple_of(n * (nsub * sq) + sub * sq, sq),
                        rep=WC_Q // WC_KV)
            q = _pad_rows((q * (DH_C ** -0.5)).astype(BF16), QBLK)
            scores = []
            for g, gs in enumerate(groups):
                qs = jnp.concatenate([q[:, (g * per + i) * DH_C:(g * per + i + 1) * DH_C] for i in range(per)],
                                     axis=0)
                scores.append(lax.dot_general(kpad[pl.ds(r0, kw), gs], qs, nt,
                                              preferred_element_type=F32))
            mask = band_ref[...]
            if mask_keys:
                key = lax.broadcasted_iota(jnp.int32, mask.shape, 0)
                mask = mask + jnp.where(key >= PRE_C - r0, 0.0, NEG).astype(F32)
            probs, sums = [], []
            for g in range(HKV_C):
                sk = jnp.concatenate([jnp.broadcast_to(sink_ref[g * per + i:g * per + i + 1, 0:1], (1, QBLK))
                                      for i in range(per)], axis=1)
                sc = scores[g] + mask
                m = jnp.maximum(jnp.max(sc, axis=0, keepdims=True), sk)
                e = jnp.exp(sc - m)
                sums.append(jnp.sum(e, axis=0, keepdims=True) + jnp.exp(sk - m))
                probs.append(e.astype(BF16))
            for g, gs in enumerate(groups):
                ot = jnp.dot(vt[gs, pl.ds(r0, kw)], probs[g], preferred_element_type=F32) / sums[g]
                for i in range(0, per, 2):
                    _pair_store(o_ref, [ot[:, (i + j) * QBLK:(i + j + 1) * QBLK] for j in (0, 1)],
                                (g * per + i) * DH_C, sub * sq, sq)

    if has_prefix:
        attend(False)
    else:
        pl.when(base < PRE_C)(lambda: attend(True))
        pl.when(base >= PRE_C)(lambda: attend(False))


def _rope_tables(pos):
    half = ROT_DIM // 2
    inv_freq = 1.0 / (ROPE_THETA ** (jnp.arange(half, dtype=F32) * (2.0 / ROT_DIM)))
    ang = pos.astype(F32)[:, None] * inv_freq[None, :]
    cos, sin = jnp.cos(ang), jnp.sin(ang)
    n = pos.shape[0]
    one = jnp.ones((n, DH_C - ROT_DIM), F32)
    zero = jnp.zeros((n, DH_C - ROT_DIM), F32)
    zh = jnp.zeros((n, half), F32)
    c = jnp.concatenate([cos, cos, one], axis=1)
    s1 = jnp.concatenate([zh, sin, zero], axis=1)
    s2 = jnp.concatenate([-sin, zh, zero], axis=1)
    rep = WC_KV // DH_C
    return tuple(jnp.concatenate([a] * rep, axis=1) for a in (c, s1, s2))


def _attn_c(u, sink, *, nb, s, row0, pos0, prefix=None, out=None):
    t = u.shape[0]
    bq = _step_rows(s)
    nq = s // bq
    s_pad = max(s, QBLK)
    kw = PRE_C + QBLK
    per = HQ_C // HKV_C
    tables = _rope_tables(pos0 + jnp.arange(s))
    band = np.where(_band_table(QBLK, PRE_C, BACK_C), 0.0, NEG).astype(np.float32)
    band = jnp.asarray(np.tile(band.T, (1, per)))
    sink_t = jnp.broadcast_to(sink.astype(F32)[:, None], (HQ_C, LANE))
    rb, sb = row0 // bq, row0 // s
    in_specs = [pl.BlockSpec((bq, WC_Q), lambda b, n: (rb + b * nq + n, COL_QC // WC_Q)),
                pl.BlockSpec((s, WC_KV), lambda b, n: (sb + b, COL_KC // WC_KV)),
                pl.BlockSpec((s, WC_KV), lambda b, n: (sb + b, COL_VC // WC_KV))]
    args = [u, u, u]
    if prefix is not None:
        in_specs += [pl.BlockSpec((1, PRE_C, WC_KV), lambda b, n: (b, 0, 0))] * 2
        args += [prefix[0], prefix[1]]
    in_specs += [pl.BlockSpec((s, WC_KV), lambda b, n: (0, 0))] * 3
    args += list(tables)
    in_specs += [pl.BlockSpec((kw, per * QBLK), lambda b, n: (0, 0)),
                 pl.BlockSpec((HQ_C, LANE), lambda b, n: (0, 0))]
    args += [band, sink_t]
    aliases = {}
    body = functools.partial(_attn_c_kernel, s=s, has_prefix=prefix is not None)
    if out is not None:
        in_specs.append(pl.BlockSpec(memory_space=pl.ANY))
        args.append(out)
        aliases = {len(args) - 1: 0}
        body = _drop_last_input(body, len(args))
    keep = min(PRE_C, s)
    return pl.pallas_call(
        body,
        grid=(nb, nq),
        in_specs=in_specs,
        out_specs=[pl.BlockSpec((bq, WC_Q), lambda b, n: (rb + b * nq + n, 0)),
                   pl.BlockSpec((1, keep, WC_KV), lambda b, n: (b, 0, 0)),
                   pl.BlockSpec((1, keep, WC_KV), lambda b, n: (b, 0, 0))],
        out_shape=[jax.ShapeDtypeStruct((t, WC_Q), BF16),
                   jax.ShapeDtypeStruct((nb, keep, WC_KV), F32),
                   jax.ShapeDtypeStruct((nb, keep, WC_KV), F32)],
        scratch_shapes=[pltpu.VMEM((PRE_C + s_pad, WC_KV), BF16), pltpu.VMEM((WC_KV, PRE_C + s_pad), BF16)],
        input_output_aliases=aliases,
        compiler_params=_cparams("parallel", "arbitrary"),
        name="mixer_c_sample" if prefix is not None else "mixer_c_prompt",
    )(*args)


def _sigmoid(x):
    return 0.5 * jnp.tanh(0.5 * x) + 0.5


def _log_sigmoid(x):
    return jnp.minimum(x, 0.0) - jnp.log(1.0 + jnp.exp(-jnp.abs(x)))


def _lane_cumsum(x):
    n = x.shape[1]
    col = lax.broadcasted_iota(jnp.int32, x.shape, 1)
    k = 1
    while k < n:
        x = x + jnp.where(col >= k, pltpu.roll(x, k, 1), 0.0)
        k *= 2
    return x


def _mlstm_kernel(qk_ref, v_ref, og_ref, g_ref, hist_ref, cw_ref, cb_ref, bif_ref, ng_ref,
                  c0_ref, n0_ref, m0_ref, y_ref, cf_ref, nf_ref, mf_ref, xt_ref,
                  xpad, c_s, n_s, m_s, *, L):
    c = pl.program_id(1)

    @pl.when(c == 0)
    def _():
        c_s[...] = c0_ref[0]
        n_s[...] = n0_ref[0]
        m_s[...] = m0_ref[0]
        xpad[0:8, :] = hist_ref[0]

    x = qk_ref[...]
    xpad[8:8 + L, :] = x
    conv = (cb_ref[...] + x * cw_ref[3:4, :] + xpad[7:7 + L, :] * cw_ref[2:3, :]
            + xpad[6:6 + L, :] * cw_ref[1:2, :] + xpad[5:5 + L, :] * cw_ref[0:1, :])
    xpad[0:8, :] = xpad[L:L + 8, :]
    act = conv * _sigmoid(conv)

    gb = g_ref[...] + bif_ref[:, 0:1]
    b_all = _lane_cumsum(_log_sigmoid(gb))
    ig_rows = gb[0:H_B, :]
    b_rows = b_all[H_B:2 * H_B, :]
    both = jnp.where(lax.broadcasted_iota(jnp.int32, gb.shape, 0) < H_B, gb, b_all)
    cols = jnp.concatenate([both, jnp.zeros((LANE - 2 * H_B, gb.shape[1]), F32)], axis=0).T

    row = lax.broadcasted_iota(jnp.int32, (L, L), 0)
    colm = lax.broadcasted_iota(jnp.int32, (L, L), 1)
    causal = colm <= row

    heads = range(H_B)
    nt = (((1,), (1,)), ((), ()))
    qf = [act[:, h * DK_B:(h + 1) * DK_B] for h in heads]
    kf = [act[:, WB_QK + h * DK_B:WB_QK + (h + 1) * DK_B] * (DK_B ** -0.5) for h in heads]
    qb = [q.astype(BF16) for q in qf]
    vb = [v_ref[:, h * DV_B:(h + 1) * DV_B].astype(BF16) for h in heads]
    cmat = [c_s[h] for h in heads]
    nvec = [n_s[h] for h in heads]
    qk = [lax.dot_general(qb[h], kf[h].astype(BF16), nt, preferred_element_type=F32) for h in heads]
    qc = [jnp.dot(qb[h], cmat[h].astype(BF16), preferred_element_type=F32) for h in heads]

    a, wi, mt, bc, igc = [], [], [], [], []
    for h in heads:
        br = b_rows[h:h + 1, 0:L]
        igr = ig_rows[h:h + 1, 0:L]
        bc.append(cols[0:L, H_B + h:H_B + h + 1])
        igc.append(cols[0:L, h:h + 1])
        logd = jnp.where(causal, bc[h] + (igr - br), NEG)
        li = bc[h] + m_s[h][:, 0:1]
        mt.append(jnp.maximum(li, jnp.max(logd, axis=1, keepdims=True)))
        wi.append(jnp.exp(li - mt[h]))
        a.append(qk[h] * jnp.exp(logd - mt[h]))

    pv = [jnp.dot(a[h].astype(BF16), vb[h], preferred_element_type=F32) for h in heads]
    for h in heads:
        num = pv[h] + wi[h] * qc[h]
        den = jnp.sum(a[h], axis=1, keepdims=True) + wi[h] * jnp.sum(qf[h] * nvec[h], axis=1, keepdims=True)
        hh = num / jnp.maximum(jnp.abs(den), jnp.exp(-mt[h]))
        hn = hh * lax.rsqrt(jnp.mean(hh * hh, axis=-1, keepdims=True) + EPS)
        vs = slice(h * DV_B, (h + 1) * DV_B)
        y_ref[:, vs] = (hn * ng_ref[:, vs] * _sigmoid(og_ref[:, vs].astype(F32))).astype(y_ref.dtype)

    for h in heads:
        bl, ml, wprev = bc[h][L - 1:L, :], mt[h][L - 1:L, :], wi[h][L - 1:L, :]
        kwt = kf[h] * jnp.exp(bl + igc[h] - bc[h] - ml)
        c_s[h] = wprev * cmat[h] + lax.dot_general(kwt.astype(BF16), vb[h], (((0,), (0,)), ((), ())),
                                                   preferred_element_type=F32)
        n_s[h] = wprev * nvec[h] + jnp.sum(kwt, axis=0, keepdims=True)
        m_s[h] = jnp.broadcast_to(ml, (1, LANE))

    @pl.when(c == pl.num_programs(1) - 1)
    def _():
        cf_ref[0] = c_s[...]
        nf_ref[0] = n_s[...]
        mf_ref[0] = m_s[...]
        xt_ref[0] = xpad[0:8, :]


def _mlstm(u, ub, g_rows, hist, cw, cb, bif, ng, c0, n0, m0, *, nb, s, row0, L, out=None):
    t = u.shape[0]
    nc = s // L
    lp = g_rows.shape[2] // nc
    rb = row0 // L
    n0 = n0.reshape(nb, H_B, 1, DK_B)
    m0 = jnp.broadcast_to(m0.reshape(nb, H_B, 1, 1), (nb, H_B, 1, LANE))
    bif_t = jnp.broadcast_to(bif.astype(F32)[:, None], (2 * H_B, LANE))
    row_spec = lambda col: pl.BlockSpec((L, WB_V), lambda b, c: (rb + b * nc + c, col))
    full = lambda shape: pl.BlockSpec(shape, lambda b, c: (0,) * len(shape))
    state = lambda shape: pl.BlockSpec((1,) + shape, lambda b, c: (b,) + (0,) * len(shape))
    in_specs = [row_spec(COL_QKB // WB_V), row_spec(COLB_VB // WB_V), row_spec(COLB_OB // WB_V),
                pl.BlockSpec((None, 2 * H_B, lp), lambda b, c: (b, 0, c)),
                state((8, 2 * WB_QK)),
                full((CONV_B, 2 * WB_QK)), full((1, 2 * WB_QK)), full((2 * H_B, LANE)), full((1, WB_V)),
                state((H_B, DK_B, DV_B)), state((H_B, 1, DK_B)), state((H_B, 1, LANE))]
    args = [u, ub, ub, g_rows, hist, cw, cb.reshape(1, -1), bif_t, ng.reshape(1, -1), c0, n0, m0]
    n_real = len(args)
    aliases = {}
    body = functools.partial(_mlstm_kernel, L=L)
    if out is not None:
        in_specs.append(pl.BlockSpec(memory_space=pl.ANY))
        args.append(out)
        aliases = {n_real: 0}
        body = _drop_last_input(body, len(args))
    y, cf, nf, mf, xt = pl.pallas_call(
        body,
        grid=(nb, nc),
        in_specs=in_specs,
        out_specs=[pl.BlockSpec((L, WB_V), lambda b, c: (rb + b * nc + c, 0)),
                   state((H_B, DK_B, DV_B)), state((H_B, 1, DK_B)), state((H_B, 1, LANE)),
                   state((8, 2 * WB_QK))],
        out_shape=[jax.ShapeDtypeStruct((t, WB_V), BF16),
                   jax.ShapeDtypeStruct((nb, H_B, DK_B, DV_B), F32),
                   jax.ShapeDtypeStruct((nb, H_B, 1, DK_B), F32),
                   jax.ShapeDtypeStruct((nb, H_B, 1, LANE), F32),
                   jax.ShapeDtypeStruct((nb, 8, 2 * WB_QK), F32)],
        scratch_shapes=[pltpu.VMEM((L + 8, 2 * WB_QK), F32),
                        pltpu.VMEM((H_B, DK_B, DV_B), F32),
                        pltpu.VMEM((H_B, 1, DK_B), F32),
                        pltpu.VMEM((H_B, 1, LANE), F32)],
        input_output_aliases=aliases,
        compiler_params=_cparams("parallel", "arbitrary"),
        name="mlstm_sample" if out is not None else "mlstm_prompt",
    )(*args)
    return y, cf, nf.reshape(nb, H_B, DK_B), mf[:, :, 0, 0], xt[:, 8 - (CONV_B - 1):, :]


def _merge_kernel(ya_ref, yb_ref, yc_ref, gt_ref, x_ref, g1_ref, wa_ref, wb_ref, wc_ref, wo_ref, o_ref):
    tm, d = x_ref.shape
    sg = jax.nn.sigmoid(gt_ref[...].astype(F32))
    m = (sg[:, 0:d] * jnp.dot(ya_ref[...].astype(BF16), wa_ref[...], preferred_element_type=F32)
         + sg[:, d:2 * d] * jnp.dot(yb_ref[...].astype(BF16), wb_ref[...], preferred_element_type=F32)
         + sg[:, 2 * d:3 * d] * jnp.dot(yc_ref[...].astype(BF16), wc_ref[...], preferred_element_type=F32))
    o = jnp.dot(m.astype(BF16), wo_ref[...], preferred_element_type=F32)
    o = (o.reshape(tm // CHUNK, CHUNK, d) * g1_ref[...]).reshape(tm, d)
    o_ref[...] = x_ref[...] + o


def _merge(ya, yb, yc, ub, x, mod_g, wa, wb, wc, wo):
    t, d = x.shape
    tm = _row_tile(t)
    row = lambda w, col=0: pl.BlockSpec((tm, w), lambda i: (i, col))
    full = lambda a: _resident(a.shape)
    return pl.pallas_call(
        _merge_kernel,
        grid=(t // tm,),
        in_specs=[row(WA), row(WB_V), row(WC_Q), row(3 * d, COLB_GATES // (3 * d)), row(d),
                  _mod_spec(tm, 2), full(wa), full(wb), full(wc), full(wo)],
        out_specs=row(d),
        out_shape=jax.ShapeDtypeStruct((t, d), F32),
        compiler_params=_cparams("parallel"),
        name="merge",
    )(ya, yb, yc, ub, x, mod_g, wa, wb, wc, wo)


def _ffn_kernel(x_ref, sh_ref, sc_ref, g_ref, g2_ref, w1_ref, w3_ref, w2_ref, o_ref, h_scr):
    tm, d = x_ref.shape
    h_scr[...] = _ln_mod(x_ref[...], g_ref[...], sc_ref[...], sh_ref[...]).astype(BF16)
    acc = None
    for c, w in _col_chunks(w1_ref.shape[1]):
        a = jnp.dot(h_scr[...], w1_ref[:, c:c + w], preferred_element_type=F32)
        b = jnp.dot(h_scr[...], w3_ref[:, c:c + w], preferred_element_type=F32)
        tt = (a * jax.nn.sigmoid(a) * b).astype(BF16)
        p = jnp.dot(tt, w2_ref[c:c + w, :], preferred_element_type=F32)
        acc = p if acc is None else acc + p
    f = (acc.reshape(tm // CHUNK, CHUNK, d) * g2_ref[...]).reshape(tm, d)
    o_ref[...] = x_ref[...] + f


def _ffn(x, mod_g, g, w1, w3, w2):
    t, d = x.shape
    tm = _row_tile(t)
    ff = w1.shape[1]
    return pl.pallas_call(
        _ffn_kernel,
        grid=(t // tm,),
        in_specs=[pl.BlockSpec((tm, d), lambda i: (i, 0)),
                  _mod_spec(tm, 3), _mod_spec(tm, 4),
                  _resident((1, d)),
                  _mod_spec(tm, 5),
                  _resident((d, ff)), _resident((d, ff)), _resident((ff, d))],
        out_specs=pl.BlockSpec((tm, d), lambda i: (i, 0)),
        out_shape=jax.ShapeDtypeStruct((t, d), F32),
        scratch_shapes=[pltpu.VMEM((tm, d), BF16)],
        compiler_params=_cparams("parallel"),
        name="ffn_dense",
    )(x, mod_g, mod_g, g.reshape(1, d), mod_g, w1, w3, w2)


def _router_kernel(x_ref, sh_ref, sc_ref, g_ref, wr_ref, br_ref, h_ref, sel_ref):
    h = _ln_mod(x_ref[...], g_ref[...], sc_ref[...], sh_ref[...])
    h_ref[...] = h.reshape(h_ref.shape)
    lg = jnp.dot(h.astype(BF16), wr_ref[...], preferred_element_type=F32) + br_ref[...]
    lane = lax.broadcasted_iota(jnp.int32, lg.shape, 1).astype(F32)
    m1 = jnp.max(lg, axis=1, keepdims=True)
    i1 = jnp.min(jnp.where(lg == m1, lane, float(LANE)), axis=1, keepdims=True)
    lg2 = jnp.where(lane == i1, 2.0 * NEG, lg)
    m2 = jnp.max(lg2, axis=1, keepdims=True)
    i2 = jnp.min(jnp.where(lg2 == m2, lane, float(LANE)), axis=1, keepdims=True)
    e2 = jnp.exp(m2 - m1)
    w1 = 1.0 / (1.0 + e2)
    sel_ref[...] = (jnp.where(lane == 0.0, i1, 0.0) + jnp.where(lane == 1.0, i2, 0.0)
                    + jnp.where(lane == 2.0, w1, 0.0) + jnp.where(lane == 3.0, e2 * w1, 0.0))


def _router(x, mod_g, g, w_r, b_r):
    t, d = x.shape
    tm = _row_tile(t)
    wr = jnp.zeros((d, LANE), BF16).at[:, :N_EXPERTS].set(w_r.astype(BF16))
    br = jnp.full((1, LANE), NEG, F32).at[0, :N_EXPERTS].set(b_r.astype(F32))
    return pl.pallas_call(
        _router_kernel,
        grid=(t // tm,),
        in_specs=[pl.BlockSpec((tm, d), lambda i: (i, 0)),
                  _mod_spec(tm, 3), _mod_spec(tm, 4),
                  pl.BlockSpec((1, d), lambda i: (0, 0)),
                  pl.BlockSpec((d, LANE), lambda i: (0, 0)),
                  pl.BlockSpec((1, LANE), lambda i: (0, 0))],
        out_specs=[pl.BlockSpec((tm // SUBLANES, SUBLANES, d), lambda i: (i, 0, 0)),
                   pl.BlockSpec((tm, LANE), lambda i: (i, 0))],
        out_shape=[jax.ShapeDtypeStruct((t // SUBLANES, SUBLANES, d), F32), jax.ShapeDtypeStruct((t, LANE), F32)],
        compiler_params=_cparams("parallel"),
        name="router",
    )(x, mod_g, mod_g, g.reshape(1, d), wr, br)


def _route(sel, tmr):
    t = sel.shape[0]
    i32 = jnp.int32
    experts = jnp.arange(N_EXPERTS, dtype=i32)[None, :]
    e = sel[:, 0:2].astype(i32).reshape(-1)
    oh = (e[:, None] == experts).astype(i32)
    csum = jnp.cumsum(oh, axis=0)
    cnt = csum[-1]
    rank = jnp.sum(csum * oh, axis=1) - 1
    ntile_e = (cnt + tmr - 1) // tmr
    tile_end = jnp.cumsum(ntile_e)
    tile_start = tile_end - ntile_e
    pos = jnp.sum(oh * (tile_start * tmr)[None, :], axis=1) + rank
    n_tiles = -(-2 * t // tmr) + N_EXPERTS
    n_used = tile_end[-1]
    tiles = jnp.arange(n_tiles, dtype=i32)
    tc = jnp.minimum(tiles, n_used - 1)
    te = jnp.sum((tc[:, None] >= tile_end[None, :]).astype(i32), axis=1)
    ohe = (te[:, None] == experts).astype(i32)
    nvalid = jnp.sum(ohe * cnt[None, :], axis=1) - (tc - jnp.sum(ohe * tile_start[None, :], axis=1)) * tmr
    nvalid = jnp.where(tiles < n_used, jnp.clip(nvalid, 0, tmr), 0)
    r = jnp.arange(EXPERT_SUB, dtype=i32)[None, :]
    n_pad = ((-cnt) % EXPERT_SUB)[:, None]
    spare = n_tiles * tmr + jnp.arange(N_EXPERTS * EXPERT_SUB, dtype=i32).reshape(N_EXPERTS, EXPERT_SUB)
    pad_rows = jnp.where(r < n_pad, (tile_start * tmr + cnt)[:, None] + r, spare).reshape(-1)
    return pos.astype(i32), te.astype(i32), nvalid.astype(i32), n_used.reshape(1).astype(i32), pad_rows.astype(i32)


EXPERT_TILE = 1024
EXPERT_SUB = 256
ROUTE_TILE = 256


def _row_copies(n_groups, make_copy):
    def issue(g, carry):
        for sub in range(SUBLANES):
            for k in range(2):
                make_copy(g, sub, k).start(priority=k)
        return carry

    def drain(g, carry):
        for sub in range(SUBLANES):
            for k in range(2):
                make_copy(g, sub, k).wait()
        return carry

    lax.fori_loop(0, n_groups, issue, 0)
    lax.fori_loop(0, n_groups, drain, 0)


def _slot_row(hi_ref, lo_ref, g, sub, k):
    idx = 2 * SUBLANES * g + 2 * sub + k
    return hi_ref[0, 0, idx], lo_ref[0, 0, idx]


def _split_pos(pos, t, tm):
    hi = lax.shift_right_logical(pos, 3).reshape(t // tm, 1, 2 * tm)
    lo = jnp.bitwise_and(pos, SUBLANES - 1).reshape(t // tm, 1, 2 * tm)
    return hi, lo


def _dispatch_kernel(hi_ref, lo_ref, pad_hi_ref, pad_lo_ref, h_ref, xs_ref, zrow, sem):
    @pl.when(pl.program_id(0) == 0)
    def _():
        zrow[...] = jnp.zeros_like(zrow)

        def zero_copy(g, sub, k):
            hi, lo = _slot_row(pad_hi_ref, pad_lo_ref, g, sub, k)
            return pltpu.make_async_copy(zrow.at[pl.ds(sub, 1), :], xs_ref.at[hi, pl.ds(lo, 1), :], sem)

        _row_copies(pad_hi_ref.shape[2] // (2 * SUBLANES), zero_copy)

    def make_copy(g, sub, k):
        hi, lo = _slot_row(hi_ref, lo_ref, g, sub, k)
        return pltpu.make_async_copy(h_ref.at[g, pl.ds(sub, 1), :], xs_ref.at[hi, pl.ds(lo, 1), :], sem)

    _row_copies(h_ref.shape[0], make_copy)


def _dispatch(h, pos, pad_rows, n_rows):
    t, d = h.shape[0] * SUBLANES, h.shape[2]
    tm = min(ROUTE_TILE, _row_tile(t))
    n_pad = pad_rows.shape[0]
    smem = pl.BlockSpec((1, 1, 2 * tm), lambda i: (i, 0, 0), memory_space=pltpu.SMEM)
    smem_pad = pl.BlockSpec((1, 1, n_pad), lambda i: (0, 0, 0), memory_space=pltpu.SMEM)
    pad_hi = lax.shift_right_logical(pad_rows, 3).reshape(1, 1, n_pad)
    pad_lo = jnp.bitwise_and(pad_rows, SUBLANES - 1).reshape(1, 1, n_pad)
    return pl.pallas_call(
        _dispatch_kernel,
        grid=(t // tm,),
        in_specs=[smem, smem, smem_pad, smem_pad,
                  pl.BlockSpec((tm // SUBLANES, SUBLANES, d), lambda i: (i, 0, 0))],
        out_specs=pl.BlockSpec(memory_space=pl.ANY),
        out_shape=jax.ShapeDtypeStruct(((n_rows + n_pad) // SUBLANES, SUBLANES, d), F32),
        scratch_shapes=[pltpu.VMEM((SUBLANES, d), F32), pltpu.SemaphoreType.DMA(())],
        compiler_params=_cparams("arbitrary"),
        name="moe_dispatch",
    )(*_split_pos(pos, t, tm), pad_hi, pad_lo, h)


def _experts_kernel(te_ref, nv_ref, nu_ref, x_ref, w1_ref, w3_ref, w2_ref, y_ref, xb, acc):
    del te_ref, nu_ref
    i, j = pl.program_id(0), pl.program_id(1)
    nv = nv_ref[i]

    @pl.when(nv > 0)
    def _():
        @pl.when(j == 0)
        def _():
            acc[...] = jnp.zeros_like(acc)

        for m in range(EXPERT_SUB, EXPERT_TILE + 1, EXPERT_SUB):
            @pl.when((nv > m - EXPERT_SUB) & (nv <= m))
            def _(m=m):
                @pl.when(j == 0)
                def _():
                    xb[0:m, :] = x_ref[0:m, :].astype(BF16)

                xs = xb[0:m, :]
                a = jnp.dot(xs, w1_ref[0].astype(BF16), preferred_element_type=F32)
                b = jnp.dot(xs, w3_ref[0].astype(BF16), preferred_element_type=F32)
                tt = (a * jax.nn.sigmoid(a) * b).astype(BF16)
                acc[0:m, :] += jnp.dot(tt, w2_ref[0].astype(BF16), preferred_element_type=F32)

        @pl.when(j == pl.num_programs(1) - 1)
        def _():
            y_ref[...] = acc[...]


def _experts(xs, te, nvalid, n_used, w1, w3, w2):
    d = xs.shape[1]
    ne, _, ff = w1.shape
    tf = 512
    nj = ff // tf
    n_tiles = te.shape[0]
    n_rows = n_tiles * EXPERT_TILE
    tile_rows = (EXPERT_TILE, d)

    def row_map(i, j, te_ref, nv_ref, nu_ref):
        return (jnp.minimum(i, nu_ref[0] - 1), 0)

    def col(i, j, nu_ref):
        return jnp.where(i < nu_ref[0], j, nj - 1)

    grid_spec = pltpu.PrefetchScalarGridSpec(
        num_scalar_prefetch=3,
        grid=(n_tiles, nj),
        in_specs=[pl.BlockSpec(tile_rows, row_map),
                  pl.BlockSpec((1, d, tf), lambda i, j, te_ref, nv_ref, nu_ref: (te_ref[i], 0, col(i, j, nu_ref))),
                  pl.BlockSpec((1, d, tf), lambda i, j, te_ref, nv_ref, nu_ref: (te_ref[i], 0, col(i, j, nu_ref))),
                  pl.BlockSpec((1, tf, d), lambda i, j, te_ref, nv_ref, nu_ref: (te_ref[i], col(i, j, nu_ref), 0))],
        out_specs=pl.BlockSpec(tile_rows, row_map),
        scratch_shapes=[pltpu.VMEM((EXPERT_TILE, d), BF16), pltpu.VMEM((EXPERT_TILE, d), F32)])
    return pl.pallas_call(
        _experts_kernel,
        grid_spec=grid_spec,
        out_shape=jax.ShapeDtypeStruct((n_rows, d), F32),
        compiler_params=_cparams("arbitrary", "arbitrary"),
        name="moe_experts",
    )(te, nvalid, n_used, xs, w1, w3, w2)


def _combine_kernel(hi_ref, lo_ref, sel_ref, x_ref, g2_ref, y_ref, *rest, n_prompt_tiles):
    tm, d = x_ref.shape
    rows, sem = rest[-2:]

    def make_copy(g, sub, k):
        hi, lo = _slot_row(hi_ref, lo_ref, g, sub, k)
        return pltpu.make_async_copy(y_ref.at[hi, pl.ds(lo, 1), :], rows.at[k, g, pl.ds(sub, 1), :], sem)

    _row_copies(tm // SUBLANES, make_copy)
    sel = sel_ref[...]
    f = sel[:, 2:3] * rows[0].reshape(tm, d) + sel[:, 3:4] * rows[1].reshape(tm, d)
    f = (f.reshape(tm // CHUNK, CHUNK, d) * g2_ref[...]).reshape(tm, d)
    xo = x_ref[...] + f
    if n_prompt_tiles is None:
        rest[0][...] = xo
        return
    gf_ref, op_ref, os_ref = rest[:3]
    yo = xo * lax.rsqrt(jnp.mean(xo * xo, axis=-1, keepdims=True) + EPS) * gf_ref[...]
    i = pl.program_id(0)

    @pl.when(i < n_prompt_tiles)
    def _():
        op_ref[...] = yo

    @pl.when(i >= n_prompt_tiles)
    def _():
        os_ref[...] = yo


def _combine(y, pos, sel, x, mod_g, final=None):
    t, d = x.shape
    tm = min(ROUTE_TILE, _row_tile(t if final is None else np.gcd(final[1], t - final[1])))
    smem = pl.BlockSpec((1, 1, 2 * tm), lambda i: (i, 0, 0), memory_space=pltpu.SMEM)
    in_specs = [smem, smem,
                pl.BlockSpec((tm, LANE), lambda i: (i, 0)),
                pl.BlockSpec((tm, d), lambda i: (i, 0)),
                _mod_spec(tm, 5),
                pl.BlockSpec(memory_space=pl.ANY)]
    args = [*_split_pos(pos, t, tm), sel, x, mod_g, y]
    if final is None:
        n_p = None
        out_specs = pl.BlockSpec((tm, d), lambda i: (i, 0))
        out_shape = jax.ShapeDtypeStruct((t, d), F32)
    else:
        gain, tp = final
        n_p = tp // tm
        in_specs.append(pl.BlockSpec((1, d), lambda i: (0, 0)))
        args.append(gain.reshape(1, d))
        out_specs = [pl.BlockSpec((tm, d), lambda i: (jnp.minimum(i, n_p - 1), 0)),
                     pl.BlockSpec((tm, d), lambda i: (jnp.maximum(i - n_p, 0), 0))]
        out_shape = [jax.ShapeDtypeStruct((tp, d), F32), jax.ShapeDtypeStruct((t - tp, d), F32)]
    return pl.pallas_call(
        functools.partial(_combine_kernel, n_prompt_tiles=n_p),
        grid=(t // tm,),
        in_specs=in_specs,
        out_specs=out_specs,
        out_shape=out_shape,
        scratch_shapes=[pltpu.VMEM((2, tm // SUBLANES, SUBLANES, d), F32), pltpu.SemaphoreType.DMA(())],
        compiler_params=_cparams("arbitrary"),
        name="moe_combine",
    )(*args)


def _moe(x, mod_g, g, w_r, b_r, w1, w3, w2, final=None):
    d = x.shape[1]
    h2, sel = _router(x, mod_g, g, w_r, b_r)
    pos, te, nvalid, n_used, pad_rows = _route(sel, EXPERT_TILE)
    n_rows = te.shape[0] * EXPERT_TILE
    xs = _dispatch(h2, pos, pad_rows, n_rows)
    y = _experts(xs.reshape(-1, d), te, nvalid, n_used, w1, w3, w2)
    return _combine(y.reshape(n_rows // SUBLANES, SUBLANES, d), pos, sel, x, mod_g, final)


def _final_kernel(x_ref, g_ref, op_ref, os_ref, *, n_prompt_tiles):
    x = x_ref[...]
    y = x * lax.rsqrt(jnp.mean(x * x, axis=-1, keepdims=True) + EPS) * g_ref[...]
    i = pl.program_id(0)

    @pl.when(i < n_prompt_tiles)
    def _():
        op_ref[...] = y

    @pl.when(i >= n_prompt_tiles)
    def _():
        os_ref[...] = y


def _final_norm(x, g, tp):
    t, d = x.shape
    tm = _row_tile(np.gcd(tp, t - tp))
    n_p = tp // tm
    return pl.pallas_call(
        functools.partial(_final_kernel, n_prompt_tiles=n_p),
        grid=(t // tm,),
        in_specs=[pl.BlockSpec((tm, d), lambda i: (i, 0)), pl.BlockSpec((1, d), lambda i: (0, 0))],
        out_specs=[pl.BlockSpec((tm, d), lambda i: (jnp.minimum(i, n_p - 1), 0)),
                   pl.BlockSpec((tm, d), lambda i: (jnp.maximum(i - n_p, 0), 0))],
        out_shape=[jax.ShapeDtypeStruct((tp, d), F32), jax.ShapeDtypeStruct((t - tp, d), F32)],
        compiler_params=_cparams("arbitrary"),
        name="final_norm",
    )(x, g.reshape(1, d))


def _w_prep_kernel(w_ref, lo_ref, hi_ref, gate_ref):
    w = w_ref[...]
    gate0, gate1 = W_LO_COLS, W_LO_COLS + 2 * H_B
    lo_ref[...] = w[:, :gate0].astype(BF16)
    gate_ref[...] = w[:, gate0:gate1]
    hi_ref[...] = w[:, gate1:].astype(BF16)


def _split_w_in(w_all, layer):
    _, d, n = w_all.shape
    rows = 256
    lo, hi, gate = pl.pallas_call(
        _w_prep_kernel,
        grid=(d // rows,),
        in_specs=[pl.BlockSpec((None, rows, n), lambda i: (layer, i, 0))],
        out_specs=[pl.BlockSpec((rows, W_LO_COLS), lambda i: (i, 0)),
                   pl.BlockSpec((rows, W_HI_COLS), lambda i: (i, 0)),
                   pl.BlockSpec((rows, 2 * H_B), lambda i: (i, 0))],
        out_shape=[jax.ShapeDtypeStruct((d, W_LO_COLS), BF16), jax.ShapeDtypeStruct((d, W_HI_COLS), BF16),
                   jax.ShapeDtypeStruct((d, 2 * H_B), F32)],
        compiler_params=_cparams("parallel"),
        name="w_in_prep",
    )(w_all)
    return lo, hi, gate.T.astype(BF16)


def kernel(x_prompt, x_sample, c_prompt, c_sample, cache_a_k, cache_a_v, state_b_C, state_b_n, state_b_m, state_b_conv, cache_c_k, cache_c_v, norm1_g, norm2_g, w_ada, b_ada, w_in, b_if_b, conv_w_b, conv_b_b, norm_b_g, rel_a, sink_c, w_br_a, w_br_b, w_br_c, w_o, w_ff1, w_ff3, w_ff2, w_router, b_router, w_e1, w_e3, w_e2, norm_f_g):
    nbp, sp, d = x_prompt.shape
    nbs, ss, _ = x_sample.shape
    tp, ts = nbp * sp, nbs * ss
    x = jnp.concatenate([x_prompt.reshape(tp, d), x_sample.reshape(ts, d)], axis=0)
    cond = jnp.concatenate([c_prompt, c_sample], axis=0)
    group_batch = np.concatenate([np.repeat(np.arange(nbp), sp // CHUNK), nbp + np.repeat(np.arange(nbs), ss // CHUNK)])

    lp_s = max(ss, LANE)
    l_b = 256 if sp % 256 == 0 else CHUNK
    keep_a, keep_c = min(PRE_A, sp), min(PRE_C, sp)
    zeros_state = (jnp.zeros((nbp, H_B, DK_B, DV_B), F32), jnp.zeros((nbp, H_B, DK_B), F32), jnp.zeros((nbp, H_B), F32))
    new = {k: [] for k in ("pak", "pav", "pbc", "pbn", "pbm", "pbx", "pck", "pcv",
                           "sak", "sav", "sbc", "sbn", "sbm", "sbx", "sck", "scv")}

    for l in range(DEPTH):
        mod = _ada(cond, w_ada, b_ada, l)
        mod_g = mod[group_batch].reshape(-1, 1, 6 * d)
        u, ub, g_t = _inproj(x, mod_g, norm1_g[l], *_split_w_in(w_in, l))

        bias_t = _rel_bias_table(rel_a[l])
        ya, pak, pav = _attn_a(u, bias_t, nb=nbp, s=sp, row0=0)
        ya, sak, sav = _attn_a(u, bias_t, nb=nbs, s=ss, row0=tp,
                               prefix=(cache_a_k[l].reshape(nbs, PRE_A, WA), cache_a_v[l].reshape(nbs, PRE_A, WA)),
                               out=ya)
        yc, pck, pcv = _attn_c(u, sink_c[l], nb=nbp, s=sp, row0=0, pos0=0)
        yc, sck, scv = _attn_c(u, sink_c[l], nb=nbs, s=ss, row0=tp, pos0=PAST_LEN,
                               prefix=(cache_c_k[l].reshape(nbs, PRE_C, WC_KV),
                                       cache_c_v[l].reshape(nbs, PRE_C, WC_KV)), out=yc)
        g_p = g_t[:, :tp].reshape(2 * H_B, nbp, sp).transpose(1, 0, 2)
        g_s = g_t[:, tp:].reshape(2 * H_B, nbs, ss).transpose(1, 0, 2)
        g_s = jnp.pad(g_s, ((0, 0), (0, 0), (0, lp_s - ss)))
        hist_p = jnp.zeros((nbp, 8, 2 * WB_QK), F32)
        hist_s = jnp.pad(state_b_conv[l], ((0, 0), (8 - (CONV_B - 1), 0), (0, 0)))
        bargs = (conv_w_b[l], conv_b_b[l], b_if_b[l], norm_b_g[l])
        yb, pbc, pbn, pbm, pbx = _mlstm(u, ub, g_p, hist_p, *bargs, *zeros_state, nb=nbp, s=sp, row0=0, L=l_b)
        yb, sbc, sbn, sbm, sbx = _mlstm(u, ub, g_s, hist_s, *bargs, state_b_C[l], state_b_n[l], state_b_m[l],
                                        nb=nbs, s=ss, row0=tp, L=ss, out=yb)

        x = _merge(ya, yb, yc, ub, x, mod_g, w_br_a[l].astype(BF16), w_br_b[l].astype(BF16),
                   w_br_c[l].astype(BF16), w_o[l].astype(BF16))
        i = l // 2
        if l % 2 == 0:
            x = _ffn(x, mod_g, norm2_g[l], w_ff1[i].astype(BF16), w_ff3[i].astype(BF16), w_ff2[i].astype(BF16))
        else:
            x = _moe(x, mod_g, norm2_g[l], w_router[i], b_router[i], w_e1[i], w_e3[i], w_e2[i],
                     final=(norm_f_g, tp) if l == DEPTH - 1 else None)

        for name, val in (("pak", pak), ("pav", pav), ("pbc", pbc), ("pbn", pbn), ("pbm", pbm), ("pbx", pbx),
                          ("pck", pck), ("pcv", pcv), ("sak", sak), ("sav", sav), ("sbc", sbc), ("sbn", sbn),
                          ("sbm", sbm), ("sbx", sbx), ("sck", sck), ("scv", scv)):
            new[name].append(val)

    y_p, y_s = x if DEPTH % 2 == 0 else _final_norm(x, norm_f_g, tp)
    heads = {"pak": (H_A, DH_A), "pav": (H_A, DH_A), "sak": (H_A, DH_A), "sav": (H_A, DH_A),
             "pck": (HKV_C, DH_C), "pcv": (HKV_C, DH_C), "sck": (HKV_C, DH_C), "scv": (HKV_C, DH_C)}
    st = {k: jnp.stack(v) for k, v in new.items()}
    st = {k: v.reshape(v.shape[:-1] + heads[k]) if k in heads else v for k, v in st.items()}
    return (y_p.reshape(nbp, sp, d), y_s.reshape(nbs, ss, d),
            st["pak"], st["pav"], st["pbc"], st["pbn"], st["pbm"], st["pbx"], st["pck"], st["pcv"],
            st["sak"], st["sav"], st["sbc"], st["sbn"], st["sbm"], st["sbx"], st["sck"], st["scv"])
```
